```python
import math
import jax, jax.numpy as jnp
from jax import lax
import numpy as np

D_MODEL = 1024
BATCH = 32
SEQ = 2048
DEPTH = 2

CTX_LEN = 256
GRID_W = 64
D_MIX = D_MODEL
NORM_EPS = 1e-6
A_WIDTH = D_MIX // 4
A_GROUPS = 4
A_CHUNK = 128
A_LN_EPS = 1e-5
B_HEADS = 4
B_HEAD_DIM = D_MIX // 16
B_WIDTH = B_HEADS * 2 * B_HEAD_DIM
B_QBLOCK = 128
ROPE_THETA = 10000.0
C_HEADS = 4
C_HEAD_DIM = D_MIX // 16
C_WIDTH = C_HEADS * C_HEAD_DIM
C_CONV = 5
C_CHUNK = 64
IN_SPLITS = [A_WIDTH] * 3 + [B_WIDTH] * 4 + [C_WIDTH] * 4 + [C_HEADS] * 4
D_IN = sum(IN_SPLITS)
IN_OFFSETS = [int(o) for o in np.cumsum(IN_SPLITS)[:-1]]

kernel_name = 'hybrid_gmlp_diffattn_gdn_prefix_block'


def rms_norm(x, w, eps=NORM_EPS):
    xf = x.astype(jnp.float32)
    y = xf * lax.rsqrt(jnp.mean(xf * xf, axis=-1, keepdims=True) + eps)
    return (y * w.astype(jnp.float32)).astype(x.dtype)


def layer_norm(x, w, eps=A_LN_EPS):
    xf = x.astype(jnp.float32)
    xc = xf - jnp.mean(xf, axis=-1, keepdims=True)
    y = xc * lax.rsqrt(jnp.mean(xc * xc, axis=-1, keepdims=True) + eps)
    return (y * w.astype(jnp.float32)).astype(x.dtype)


def l2_norm(x, eps=1e-6):
    xf = x.astype(jnp.float32)
    return (xf * lax.rsqrt(jnp.sum(xf * xf, axis=-1, keepdims=True) + eps)).astype(x.dtype)


def split_cols(p):
    return jnp.split(p, IN_OFFSETS, axis=-1)


def chunk_mlp(u, v, z, ln_w, w_s, b_s):
    bsz, n, _ = u.shape
    vc = layer_norm(v, ln_w).reshape(bsz, n // A_CHUNK, A_CHUNK, A_GROUPS, A_WIDTH // A_GROUPS)
    s = jnp.einsum('gij,bcjgd->bcigd', w_s, vc) + b_s.T[None, None, :, :, None]
    return u * s.reshape(bsz, n, A_WIDTH) * jax.nn.silu(z)


def rope_1d(x, pos):
    n = x.shape[-1]
    inv_freq = ROPE_THETA ** (-jnp.arange(0, n, 2, dtype=jnp.float32) / n)
    ang = pos.astype(jnp.float32)[:, None] * inv_freq[None, :]
    ang = jnp.concatenate([ang, ang], axis=-1)[None, :, None, :]
    x1, x2 = x[..., : n // 2], x[..., n // 2:]
    rot = jnp.concatenate([-x2, x1], axis=-1)
    return x * jnp.cos(ang).astype(x.dtype) + rot * jnp.sin(ang).astype(x.dtype)


def rope_2d(x, rows, cols):
    half = x.shape[-1] // 2
    return jnp.concatenate([rope_1d(x[..., :half], rows), rope_1d(x[..., half:], cols)], axis=-1)


def diff_attn_core(q, k, v, lam):
    s = jnp.einsum('bqhd,bkhd->bhqk', q, k, preferred_element_type=jnp.float32) * (q.shape[-1] ** -0.5)
    p = jax.nn.softmax(s, axis=-1)
    bsz, _, nq, nk = p.shape
    p = p.reshape(bsz, B_HEADS, 2, nq, nk)
    p = p[:, :, 0] - lam * p[:, :, 1]
    return jnp.einsum('bhqk,bkhe->bqhe', p.astype(v.dtype), v)


def diff_attn_latent(q, k_all, v_all, lam):
    bsz, n, h2, d = q.shape
    nb = n // B_QBLOCK
    qb = q.reshape(bsz, nb, B_QBLOCK, h2, d).transpose(1, 0, 2, 3, 4)
    out = lax.map(lambda qi: diff_attn_core(qi, k_all, v_all, lam), qb)
    return out.transpose(1, 0, 2, 3, 4).reshape(bsz, n, B_HEADS, 2 * B_HEAD_DIM)


def short_conv(x, w):
    n = x.shape[1]
    pad = C_CONV // 2
    xp = jnp.pad(x, ((0, 0), (pad, pad), (0, 0)))
    out = xp[:, 0:n] * w[0]
    for j in range(1, C_CONV):
        out = out + xp[:, j:j + n] * w[j]
    return out


def gated_delta_chunked(q, k, v, g, beta, state):
    f32 = jnp.float32
    bsz, n, h, dk = q.shape
    dv = v.shape[-1]
    nc = n // C_CHUNK

    def chunks(t):
        t = t.astype(f32).reshape((bsz, nc, C_CHUNK, h) + t.shape[3:])
        return jnp.moveaxis(t, (1, 3), (0, 2))

    qc = chunks(q) * (dk ** -0.5)
    kc = chunks(k)
    vc = chunks(v)
    gc = jnp.cumsum(chunks(g), axis=-1)
    bc = chunks(beta)
    tril = jnp.tril(jnp.ones((C_CHUNK, C_CHUNK), dtype=bool))
    strict = jnp.tril(jnp.ones((C_CHUNK, C_CHUNK), dtype=bool), -1)
    diff = gc[..., :, None] - gc[..., None, :]
    decay = jnp.where(tril, jnp.exp(jnp.where(tril, diff, 0.0)), 0.0)
    kb = kc * bc[..., None]
    lmat = jnp.where(strict, jnp.einsum('...id,...jd->...ij', kb, kc) * decay, 0.0)
    eye = jnp.eye(C_CHUNK, dtype=f32)
    tmat = lax.linalg.triangular_solve(lmat + eye, jnp.broadcast_to(eye, lmat.shape),
                                       left_side=True, lower=True, unit_diagonal=True)
    u = tmat @ (vc * bc[..., None])
    w = tmat @ (kb * jnp.exp(gc)[..., None])
    attn = jnp.where(tril, jnp.einsum('...id,...jd->...ij', qc, kc) * decay, 0.0)
    q_dec = qc * jnp.exp(gc)[..., None]
    g_last = gc[..., -1]
    k_dec = kc * jnp.exp(g_last[..., None] - gc)[..., None]

    def step(s, xs):
        q_i, k_i, u_i, w_i, a_i, gl_i = xs
        v_new = u_i - w_i @ s
        o_i = q_i @ s + a_i @ v_new
        s = s * jnp.exp(gl_i)[..., None, None] + jnp.swapaxes(k_i, -1, -2) @ v_new
        return s, o_i

    s_fin, o = lax.scan(step, state.astype(f32), (q_dec, k_dec, u, w, attn, g_last))
    o = jnp.moveaxis(o, (0, 2), (1, 3)).reshape(bsz, n, h, dv)
    return o.astype(v.dtype), s_fin


def gdn_prep(q, k, v, b_f, b_b, a_f, a_b, conv_w, a_log, dt_bias):
    f32 = jnp.float32
    bsz, n, _ = q.shape
    qkv = jax.nn.silu(short_conv(jnp.concatenate([q, k, v], axis=-1), conv_w))
    q, k, v = jnp.split(qkv, 3, axis=-1)
    hs = lambda t: t.reshape(bsz, n, C_HEADS, C_HEAD_DIM)
    q, k, v = l2_norm(hs(q)), l2_norm(hs(k)), hs(v)
    dirs = []
    for d, (b, a) in enumerate(((b_f, a_f), (b_b, a_b))):
        beta = jax.nn.sigmoid(b.astype(f32))
        g = -jnp.exp(a_log[d].astype(f32)) * jax.nn.softplus(a.astype(f32) + dt_bias[d].astype(f32))
        dirs.append((g, beta))
    return q, k, v, dirs


def hybrid_layer(x, xc, c_act, cc_act, w_ada, b_ada, norm_w, w_in, w_out,
                 a_ln_w, a_ws, a_bs, b_lam, b_subln_w,
                 c_conv_w, c_a_log, c_dt_bias, c_norm_w,
                 rows, cols, layer_idx, ctx_out):
    bsz, n, _ = x.shape
    shift, scale, gate = jnp.split((c_act @ w_ada + b_ada)[:, None, :], 3, axis=-1)
    shift_c, scale_c, gate_c = jnp.split(cc_act @ w_ada + b_ada, 3, axis=-1)
    h = rms_norm(x, norm_w) * (1.0 + scale) + shift
    hc = rms_norm(xc, norm_w) * (1.0 + scale_c) + shift_c
    pa = split_cols(h @ w_in)
    pc = split_cols(hc @ w_in)

    y_a = chunk_mlp(pa[0], pa[1], pa[2], a_ln_w, a_ws, a_bs)

    lam_init = 0.8 - 0.6 * math.exp(-0.3 * layer_idx)
    lf = b_lam.astype(jnp.float32)
    lam = jnp.exp(jnp.sum(lf[0] * lf[1])) - jnp.exp(jnp.sum(lf[2] * lf[3])) + lam_init

    def heads(q, k, v):
        m = q.shape[1]
        return (q.reshape(bsz, m, 2 * B_HEADS, B_HEAD_DIM), k.reshape(bsz, m, 2 * B_HEADS, B_HEAD_DIM),
                v.reshape(bsz, m, B_HEADS, 2 * B_HEAD_DIM))

    def finish_b(o, z):
        o = rms_norm(o, b_subln_w) * (1.0 - lam_init)
        return o.reshape(o.shape[0], o.shape[1], B_WIDTH) * jax.nn.silu(z)

    q_l, k_l, v_l = heads(pa[3], pa[4], pa[5])
    q_l, k_l = rope_2d(q_l, rows, cols), rope_2d(k_l, rows, cols)
    q_c, k_c, v_c = heads(pc[3], pc[4], pc[5])
    k_all = jnp.concatenate([k_c, k_l], axis=1)
    v_all = jnp.concatenate([v_c, v_l], axis=1)
    y_b = finish_b(diff_attn_latent(q_l, k_all, v_all, lam), pa[6])

    lat = gdn_prep(pa[7], pa[8], pa[9], pa[11], pa[12], pa[13], pa[14], c_conv_w, c_a_log, c_dt_bias)
    cxt = gdn_prep(pc[7], pc[8], pc[9], pc[11], pc[12], pc[13], pc[14], c_conv_w, c_a_log, c_dt_bias)
    zero = jnp.zeros((bsz, C_HEADS, C_HEAD_DIM, C_HEAD_DIM), jnp.float32)
    o_lat, o_ctx = [], []
    for d in range(2):
        fl = (lambda t: jnp.flip(t, axis=1)) if d == 1 else (lambda t: t)
        ql, kl, vl, dl = lat
        qc_, kc_, vc_, dc = cxt
        oc, s_c = gated_delta_chunked(fl(qc_), fl(kc_), fl(vc_), fl(dc[d][0]), fl(dc[d][1]), zero)
        ol, _ = gated_delta_chunked(fl(ql), fl(kl), fl(vl), fl(dl[d][0]), fl(dl[d][1]), s_c)
        o_lat.append(fl(ol))
        o_ctx.append(fl(oc))

    def finish_c(o, z):
        o = rms_norm(o, c_norm_w)
        return o.reshape(o.shape[0], o.shape[1], C_WIDTH) * jax.nn.silu(z)

    y_c = finish_c(o_lat[0] + o_lat[1], pa[10])

    x = x + gate * (jnp.concatenate([y_a, y_b, y_c], axis=-1) @ w_out)
    if ctx_out:
        yc_a = chunk_mlp(pc[0], pc[1], pc[2], a_ln_w, a_ws, a_bs)
        yc_b = finish_b(diff_attn_core(q_c, k_c, v_c, lam), pc[6])
        yc_c = finish_c(o_ctx[0] + o_ctx[1], pc[10])
        xc = xc + gate_c * (jnp.concatenate([yc_a, yc_b, yc_c], axis=-1) @ w_out)
    return x, xc


def setup_inputs(seed: int = 0) -> dict:
    key = jax.random.key(seed)
    ks = jax.random.split(key, 20)
    f32 = jnp.float32
    nrm = lambda k, s: jax.random.normal(k, s, dtype=f32)
    dt = jnp.exp(jax.random.uniform(ks[17], (DEPTH, 2, C_HEADS), dtype=f32,
                                    minval=math.log(1e-3), maxval=math.log(1e-1)))
    return {
        'x': nrm(ks[0], (BATCH, SEQ, D_MODEL)),
        'c': nrm(ks[1], (BATCH, D_MODEL)),
        'ctx': nrm(ks[2], (BATCH, CTX_LEN, D_MODEL)),
        'c_ctx': nrm(ks[3], (D_MODEL,)),
        'w_ada': nrm(ks[4], (DEPTH, D_MODEL, 3 * D_MODEL)) * (0.5 * D_MODEL ** -0.5),
        'b_ada': nrm(ks[5], (DEPTH, 3 * D_MODEL)) * 0.02,
        'norm_w': 1.0 + 0.02 * nrm(ks[6], (DEPTH, D_MODEL)),
        'w_in': nrm(ks[7], (DEPTH, D_MODEL, D_IN)) * D_MODEL ** -0.5,
        'w_out': nrm(ks[8], (DEPTH, D_MIX, D_MODEL)) * D_MIX ** -0.5,
        'a_ln_w': 1.0 + 0.02 * nrm(ks[9], (DEPTH, A_WIDTH)),
        'a_ws': nrm(ks[10], (DEPTH, A_GROUPS, A_CHUNK, A_CHUNK)) * A_CHUNK ** -0.5,
        'a_bs': 1.0 + 0.02 * nrm(ks[11], (DEPTH, A_GROUPS, A_CHUNK)),
        'b_lam': nrm(ks[12], (DEPTH, 4, B_HEAD_DIM)) * 0.1,
        'b_subln_w': 1.0 + 0.02 * nrm(ks[13], (DEPTH, 2 * B_HEAD_DIM)),
        'c_conv_w': nrm(ks[14], (DEPTH, C_CONV, 3 * C_WIDTH)) * C_CONV ** -0.5,
        'c_a_log': jnp.log(jax.random.uniform(ks[15], (DEPTH, 2, C_HEADS), dtype=f32, minval=1.0, maxval=16.0)),
        'c_dt_bias': dt + jnp.log(-jnp.expm1(-dt)),
        'c_norm_w': 1.0 + 0.02 * nrm(ks[16], (DEPTH, C_HEAD_DIM)),
        'final_norm_w': 1.0 + 0.02 * nrm(ks[18], (D_MODEL,)),
    }


def reference(x, c, ctx, c_ctx, w_ada, b_ada, norm_w, w_in, w_out, a_ln_w, a_ws, a_bs,
              b_lam, b_subln_w, c_conv_w, c_a_log, c_dt_bias, c_norm_w, final_norm_w):
    n = x.shape[1]
    rows_count = n // GRID_W
    rows = jnp.repeat(jnp.arange(rows_count), GRID_W)
    cols = jnp.tile(jnp.arange(GRID_W), rows_count)
    c_act = jax.nn.silu(c)
    cc_act = jax.nn.silu(c_ctx)
    xc = ctx
    for i in range(DEPTH):
        x, xc = hybrid_layer(x, xc, c_act, cc_act, w_ada[i], b_ada[i], norm_w[i], w_in[i], w_out[i],
                             a_ln_w[i], a_ws[i], a_bs[i], b_lam[i], b_subln_w[i],
                             c_conv_w[i], c_a_log[i], c_dt_bias[i], c_norm_w[i],
                             rows, cols, i, i < DEPTH - 1)
    return rms_norm(x, final_norm_w)
```

```python
import functools
import math

import numpy as np
import jax
import jax.numpy as jnp
from jax import lax
from jax.experimental import pallas as pl
from jax.experimental.pallas import tpu as pltpu

F32 = jnp.float32
MXU = jnp.bfloat16

D_MODEL = 1024
NORM_EPS = 1e-6
GRID_W = 64
A_WIDTH, A_GROUPS, A_CHUNK, A_LN_EPS = 256, 4, 128, 1e-5
B_HEADS, B_HEAD_DIM, B_WIDTH = 4, 64, 512
ROPE_THETA = 10000.0
C_HEADS, C_HEAD_DIM, C_WIDTH, C_CONV, C_CHUNK = 4, 64, 256, 5, 64
N_GATE = 4 * C_HEADS
GATE_PAD = 128
OFF_A, OFF_B, OFF_C, OFF_G = 0, 3 * A_WIDTH, 3 * A_WIDTH + 4 * B_WIDTH, 3 * A_WIDTH + 4 * B_WIDTH + 4 * C_WIDTH
D_IN_PAD = OFF_G + GATE_PAD
NEG_BIG = -1e30
VMEM_LIMIT = 56 * 1024 * 1024
PREP_ROWS = 256
HALO = 8
INV_BLOCK = 16


def _mm(a, b):
    return jnp.dot(a.astype(MXU), b.astype(MXU), preferred_element_type=F32)


def _mm_nt(a, b):
    return lax.dot_general(a.astype(MXU), b.astype(MXU), (((1,), (1,)), ((), ())), preferred_element_type=F32)


def _mm_tn(a, b):
    return lax.dot_general(a.astype(MXU), b.astype(MXU), (((0,), (0,)), ((), ())), preferred_element_type=F32)


def _mm_exact(a, b):
    return jnp.dot(a, b, precision=lax.Precision.HIGHEST, preferred_element_type=F32)


def _split(a):
    hi = a.astype(MXU)
    return hi, (a - hi.astype(F32)).astype(MXU)


def _mm3(a, b):
    ah, al = _split(a)
    bh, bl = _split(b)
    d = lambda x, y: jnp.dot(x, y, preferred_element_type=F32)
    return d(ah, bh) + (d(ah, bl) + d(al, bh))


def _sigmoid(x):
    return 1.0 / (1.0 + jnp.exp(-x))


def _silu(x):
    return x * _sigmoid(x)


def _iota(shape, dim):
    return lax.broadcasted_iota(jnp.int32, shape, dim)


def _ada_kernel(c_ref, w_ref, b_ref, o_ref):
    o_ref[0] = _mm3(_silu(c_ref[...]), w_ref[0]) + b_ref[0]


def _ada(c_rows, w_ada, b_ada):
    depth, d, d3 = w_ada.shape
    r = c_rows.shape[0]
    tn = 512
    return pl.pallas_call(
        _ada_kernel,
        grid=(depth, d3 // tn),
        in_specs=[pl.BlockSpec((r, d), lambda l, j: (0, 0)),
                  pl.BlockSpec((1, d, tn), lambda l, j: (l, 0, j)),
                  pl.BlockSpec((1, 1, tn), lambda l, j: (l, 0, j))],
        out_specs=pl.BlockSpec((1, r, tn), lambda l, j: (l, 0, j)),
        out_shape=jax.ShapeDtypeStruct((depth, r, d3), F32),
        name="ada_modulation",
    )(c_rows, w_ada, b_ada.reshape(depth, 1, d3))


def _proj_kernel(x_ref, mod_ref, nw_ref, w_ref, cos_ref, sinn_ref, sinp_ref, lnw_ref, aws_ref, abias_ref,
                 ya_ref, q_ref, k_ref, v_ref, zb_ref, craw_ref, zc_ref, g_ref, *, rope):
    d = D_MODEL
    x = x_ref[0]
    mod = mod_ref[0]
    y = x * lax.rsqrt(jnp.mean(x * x, axis=-1, keepdims=True) + NORM_EPS) * nw_ref[...]
    h = (y * (1.0 + mod[:, d:2 * d]) + mod[:, :d]).astype(MXU)
    t = x.shape[0]

    pa = jnp.dot(h, w_ref[:, OFF_A:OFF_B], preferred_element_type=F32)
    u, v, z = pa[:, :A_WIDTH], pa[:, A_WIDTH:2 * A_WIDTH], pa[:, 2 * A_WIDTH:]
    vc = v - jnp.mean(v, axis=-1, keepdims=True)
    vn = (vc * lax.rsqrt(jnp.mean(vc * vc, axis=-1, keepdims=True) + A_LN_EPS) * lnw_ref[...]).astype(MXU)
    grp = _iota((A_CHUNK, A_WIDTH), 1) // (A_WIDTH // A_GROUPS)
    mixed = []
    for c in range(t // A_CHUNK):
        r = jnp.dot(aws_ref[...], vn[c * A_CHUNK:(c + 1) * A_CHUNK], preferred_element_type=F32)
        s = abias_ref[...]
        for g in range(A_GROUPS):
            s = s + jnp.where(grp == g, r[g * A_CHUNK:(g + 1) * A_CHUNK], 0.0)
        mixed.append(s)
    s_all = jnp.concatenate(mixed, axis=0) if len(mixed) > 1 else mixed[0]
    ya_ref[0] = (u * s_all * _silu(z)).astype(ya_ref.dtype)

    pb = jnp.dot(h, w_ref[:, OFF_B:OFF_C], preferred_element_type=F32)
    q, k = pb[:, :B_WIDTH], pb[:, B_WIDTH:2 * B_WIDTH]
    if rope:
        cos, sinn, sinp = cos_ref[...], sinn_ref[...], sinp_ref[...]
        half = B_HEAD_DIM // 4
        rot = lambda a: (a * cos + pltpu.roll(a, B_WIDTH - half, 1) * sinn + pltpu.roll(a, half, 1) * sinp)
        q, k = rot(q), rot(k)
    q_ref[0] = (q * (B_HEAD_DIM ** -0.5)).astype(q_ref.dtype)
    k_ref[0] = k.astype(k_ref.dtype)
    v_ref[0] = pb[:, 2 * B_WIDTH:3 * B_WIDTH].astype(v_ref.dtype)
    zb_ref[0] = pb[:, 3 * B_WIDTH:].astype(zb_ref.dtype)

    pc = jnp.dot(h, w_ref[:, OFF_C:OFF_G], preferred_element_type=F32)
    craw_ref[0] = pc[:, :3 * C_WIDTH]
    zc_ref[0] = pc[:, 3 * C_WIDTH:]
    pg = jnp.dot(h, w_ref[:, OFF_G:], preferred_element_type=F32)
    g_ref[0] = pg[:, :N_GATE]


def _proj(x, mod, nw, w_in, tabs, lnw, aws, abias, *, rope, tile):
    bsz, n, d = x.shape
    cos, sinn, sinp = tabs
    mod_map = (lambda i, b: (b, 0, 0)) if mod.shape[0] == bsz else (lambda i, b: (0, 0, 0))
    tok = lambda w: pl.BlockSpec((1, tile, w), lambda i, b: (b, i, 0))
    const = lambda a: pl.BlockSpec(a.shape, lambda i, b: (0,) * a.ndim)
    tab = pl.BlockSpec((tile, B_WIDTH), lambda i, b: (i, 0))
    outs = [(A_WIDTH, MXU), (B_WIDTH, MXU), (B_WIDTH, MXU), (B_WIDTH, MXU), (B_WIDTH, F32),
            (3 * C_WIDTH, F32), (C_WIDTH, F32), (N_GATE, F32)]
    return pl.pallas_call(
        functools.partial(_proj_kernel, rope=rope),
        grid=(n // tile, bsz),
        in_specs=[tok(d), pl.BlockSpec((1, 1, 3 * d), mod_map), const(nw), const(w_in), tab, tab, tab,
                  const(lnw), const(aws), const(abias)],
        out_specs=[tok(w) for w, _ in outs],
        out_shape=[jax.ShapeDtypeStruct((bsz, n, w), dt) for w, dt in outs],
        compiler_params=pltpu.CompilerParams(dimension_semantics=("arbitrary", "arbitrary"),
                                             vmem_limit_bytes=VMEM_LIMIT),
        name="norm_inproj_rope" if rope else "norm_inproj_ctx",
    )(x, mod, nw, w_in, cos, sinn, sinp, lnw, aws, abias)


def _attn_kernel(*refs, nseg, lam_init):
    q_ref, z_ref, lam_ref, sw_ref = refs[:4]
    k_refs, v_refs = refs[4:4 + nseg], refs[4 + nseg:4 + 2 * nseg]
    o_ref = refs[-1]
    lf = lam_ref[...]
    lam = (jnp.exp(jnp.sum(lf[0:1] * lf[1:2], axis=-1, keepdims=True))
           - jnp.exp(jnp.sum(lf[2:3] * lf[3:4], axis=-1, keepdims=True)) + lam_init)
    q = q_ref[0]
    pair = 2 * B_HEAD_DIM
    first = _iota((1, pair), 1) < B_HEAD_DIM
    zero = jnp.zeros((), q.dtype)
    heads = []
    for hh in range(B_HEADS):
        lanes = slice(hh * pair, (hh + 1) * pair)
        qp = q[:, lanes]
        acc = None
        for j in range(2):
            qm = jnp.where(first if j == 0 else jnp.logical_not(first), qp, zero)
            ss = [_mm_nt(qm, kr[0, :, lanes]) for kr in k_refs]
            m = functools.reduce(jnp.maximum, [jnp.max(s, axis=-1, keepdims=True) for s in ss])
            es = [jnp.exp(s - m) for s in ss]
            den = functools.reduce(lambda a, b: a + b, [jnp.sum(e, axis=-1, keepdims=True) for e in es])
            pv = functools.reduce(lambda a, b: a + b, [_mm(e, vr[0, :, lanes]) for e, vr in zip(es, v_refs)])
            r = pv * (1.0 / den)
            acc = r if j == 0 else acc - lam * r
        o = acc * lax.rsqrt(jnp.mean(acc * acc, axis=-1, keepdims=True) + NORM_EPS) * sw_ref[...]
        heads.append(o * (1.0 - lam_init))
    o_ref[0] = (jnp.concatenate(heads, axis=-1) * _silu(z_ref[0])).astype(o_ref.dtype)


def _attn(q, z, lam_p, subln_w, ks, vs, *, lam_init, tile):
    bsz, n, w = q.shape
    nseg = len(ks)
    tok = pl.BlockSpec((1, tile, w), lambda b, i: (b, i, 0))
    const = lambda a: pl.BlockSpec(a.shape, lambda b, i: (0,) * a.ndim)
    full = lambda a: pl.BlockSpec((1,) + a.shape[1:], lambda b, i: (b, 0, 0))
    return pl.pallas_call(
        functools.partial(_attn_kernel, nseg=nseg, lam_init=lam_init),
        grid=(bsz, n // tile),
        in_specs=[tok, tok, const(lam_p), const(subln_w)] + [full(a) for a in ks] + [full(a) for a in vs],
        out_specs=tok,
        out_shape=jax.ShapeDtypeStruct((bsz, n, w), MXU),
        compiler_params=pltpu.CompilerParams(dimension_semantics=("arbitrary", "arbitrary"),
                                             vmem_limit_bytes=VMEM_LIMIT),
        name="diff_attention_%dseg" % nseg,
    )(q, z, lam_p, subln_w, *ks, *vs)


def _head_of_lane(shape, dim):
    return _iota(shape, dim) // C_HEAD_DIM


def _expand_heads(cols):
    r = cols.shape[0]
    hl = _head_of_lane((r, C_WIDTH), 1)
    out = jnp.broadcast_to(cols[:, C_HEADS - 1:C_HEADS], (r, C_WIDTH))
    for h in range(C_HEADS - 2, -1, -1):
        out = jnp.where(hl == h, jnp.broadcast_to(cols[:, h:h + 1], (r, C_WIDTH)), out)
    return out


def _gdn_prep(raw_ref, gate_ref, n, off, convw_ref, alog_ref, dtb_ref, q_s, k_s, v_s, bf_s, bb_s, gf_s, gb_s):
    rows = min(PREP_ROWS, n)
    nblk = n // rows
    same_head = (_head_of_lane((C_WIDTH, C_WIDTH), 0) == _head_of_lane((C_WIDTH, C_WIDTH), 1)).astype(F32)
    pad = C_CONV // 2

    def block(i, carry):
        base = pl.multiple_of(i * rows, rows)
        cur = raw_ref[0, pl.ds(base, rows), :]
        prev = raw_ref[0, pl.ds(pl.multiple_of(jnp.maximum(base - HALO, 0), HALO), HALO), :]
        nxt = raw_ref[0, pl.ds(pl.multiple_of(jnp.minimum(base + rows, n - HALO), HALO), HALO), :]
        prev = jnp.where(i == 0, 0.0, prev)
        nxt = jnp.where(i == nblk - 1, 0.0, nxt)
        ext = jnp.concatenate([prev, cur, nxt], axis=0)
        w = convw_ref[...]
        acc = cur * w[pad:pad + 1]
        for j in range(C_CONV):
            if j != pad:
                acc = acc + pltpu.roll(ext, (pad - j) % (rows + 2 * HALO), 0)[HALO:HALO + rows] * w[j:j + 1]
        act = _silu(acc)
        qh, kh, vh = act[:, :C_WIDTH], act[:, C_WIDTH:2 * C_WIDTH], act[:, 2 * C_WIDTH:]
        l2 = lambda a: a * lax.rsqrt(_mm_exact(a * a, same_head) + 1e-6)
        dst = pl.ds(pl.multiple_of(off + base, rows), rows)
        q_s[dst, :] = l2(qh) * (C_HEAD_DIM ** -0.5)
        k_s[dst, :] = l2(kh)
        v_s[dst, :] = vh
        gt = gate_ref[0, pl.ds(base, rows), :]
        h = C_HEADS
        bf_s[dst, :] = _expand_heads(_sigmoid(gt[:, 0:h]))
        bb_s[dst, :] = _expand_heads(_sigmoid(gt[:, h:2 * h]))
        sp = lambda a: jnp.maximum(a, 0.0) + jnp.log1p(jnp.exp(-jnp.abs(a)))
        gf_s[dst, :] = _expand_heads(-jnp.exp(alog_ref[0:1, :]) * sp(gt[:, 2 * h:3 * h] + dtb_ref[0:1, :]))
        gb_s[dst, :] = _expand_heads(-jnp.exp(alog_ref[1:2, :]) * sp(gt[:, 3 * h:4 * h] + dtb_ref[1:2, :]))
        return carry

    lax.fori_loop(0, nblk, block, 0)


def _block_diag(a, bmask):
    return jnp.where(bmask, jnp.concatenate([a] * C_HEADS, axis=0), 0.0)


def _gdn_chunk(rows, backward, q_s, k_s, v_s, beta_s, g_s, state_ref, o_s, first_dir):
    c, wd = C_CHUNK, C_WIDTH
    q, k, v, beta, g = q_s[rows, :], k_s[rows, :], v_s[rows, :], beta_s[rows, :], g_s[rows, :]
    ii = _iota((c, wd), 0)
    jj = _iota((c, wd), 1) % c
    incl = (jj >= ii) if backward else (jj <= ii)
    strict = (jj > ii) if backward else (jj < ii)
    eye = jj == ii
    ci, cj = _iota((c, c), 0), _iota((c, c), 1)
    csum = ((cj >= ci) if backward else (cj <= ci)).astype(F32)
    gc = _mm_exact(csum, g)
    grow = jnp.sum(jnp.where(eye, gc, 0.0), axis=0, keepdims=True)
    decay = jnp.exp(jnp.where(incl, gc - grow, NEG_BIG))
    bmask = _head_of_lane((C_HEADS * c, wd), 1) == (_iota((C_HEADS * c, wd), 0) // c)
    kb = k * beta
    kbd = _block_diag(k, bmask)
    qk = _mm_nt(jnp.concatenate([kb, q], axis=0), kbd)
    lmat = jnp.where(strict, qk[:c] * decay, 0.0)
    attn = qk[c:] * decay
    dmat = jnp.where((ii // INV_BLOCK) == (jj // INV_BLOCK), lmat, 0.0)
    tmat = jnp.where(eye, 1.0, 0.0) - dmat
    pw = _mm(dmat, _block_diag(dmat, bmask))
    nsq = int(math.log2(INV_BLOCK)) - 1
    for step in range(nsq):
        bd = _block_diag(pw, bmask)
        if step < nsq - 1:
            r = _mm(jnp.concatenate([tmat, pw], axis=0), bd)
            tmat, pw = tmat + r[:c], r[c:]
        else:
            tmat = tmat + _mm(tmat, bd)
    size = INV_BLOCK
    while size < c:
        emask = ((ii // (2 * size)) == (jj // (2 * size))) & ((ii // size) != (jj // size))
        y = _mm(jnp.where(emask, lmat, 0.0), _block_diag(tmat, bmask))
        tmat = tmat - _mm(tmat, _block_diag(y, bmask))
        size *= 2
    egc = jnp.exp(gc)
    u = _mm(tmat, _block_diag(v * beta, bmask))
    w = _mm(tmat, _block_diag(kb * egc, bmask))
    glast = gc[0:1] if backward else gc[c - 1:c]
    state = state_ref[...]
    ws = _mm(jnp.concatenate([w, q * egc], axis=0), state)
    v_new = u - ws[:c]
    o = ws[c:] + _mm(attn, _block_diag(v_new, bmask))
    if first_dir:
        o_s[rows, :] = o
    else:
        o_s[rows, :] = o_s[rows, :] + o
    smask = _head_of_lane((wd, wd), 0) == _head_of_lane((wd, wd), 1)
    upd = _mm_tn(k * jnp.exp(glast - gc), v_new)
    state_ref[...] = state * jnp.exp(glast) + jnp.where(smask, upd, 0.0)


def _gdn_kernel(*refs, n_ctx, n_lat, ctx_out):
    (rawc_ref, gtc_ref, zc_ref, rawl_ref, gtl_ref, zl_ref, convw_ref, alog_ref, dtb_ref, nw_ref) = refs[:10]
    nout = 2 if ctx_out else 1
    out_refs = refs[10:10 + nout]
    q_s, k_s, v_s, bf_s, bb_s, gf_s, gb_s, of_s, ob_s, st_f, st_b = refs[10 + nout:]
    scr = (q_s, k_s, v_s, bf_s, bb_s, gf_s, gb_s)
    _gdn_prep(rawc_ref, gtc_ref, n_ctx, 0, convw_ref, alog_ref, dtb_ref, *scr)
    _gdn_prep(rawl_ref, gtl_ref, n_lat, n_ctx, convw_ref, alog_ref, dtb_ref, *scr)
    st_f[...] = jnp.zeros_like(st_f)
    st_b[...] = jnp.zeros_like(st_b)
    ncc, nlc = n_ctx // C_CHUNK, n_lat // C_CHUNK

    def step(i, carry):
        cf = i
        cb = jnp.where(i < ncc, ncc - 1 - i, 2 * ncc + nlc - 1 - i)
        rf = pl.ds(pl.multiple_of(cf * C_CHUNK, C_CHUNK), C_CHUNK)
        rb = pl.ds(pl.multiple_of(cb * C_CHUNK, C_CHUNK), C_CHUNK)
        _gdn_chunk(rf, False, q_s, k_s, v_s, bf_s, gf_s, st_f, of_s, True)
        _gdn_chunk(rb, True, q_s, k_s, v_s, bb_s, gb_s, st_b, ob_s, True)
        return carry

    lax.fori_loop(0, ncc + nlc, step, 0)

    same_head = (_head_of_lane((C_WIDTH, C_WIDTH), 0) == _head_of_lane((C_WIDTH, C_WIDTH), 1)).astype(F32)

    def finish(z_ref, out_ref, off, n):
        rows = min(PREP_ROWS, n)

        def blk(i, carry):
            base = pl.multiple_of(i * rows, rows)
            src = pl.ds(pl.multiple_of(off + base, rows), rows)
            o = of_s[src, :] + ob_s[src, :]
            ms = _mm_exact(o * o, same_head) * (1.0 / C_HEAD_DIM)
            y = o * lax.rsqrt(ms + NORM_EPS) * nw_ref[...]
            out_ref[0, pl.ds(base, rows), :] = (y * _silu(z_ref[0, pl.ds(base, rows), :])).astype(out_ref.dtype)
            return carry

        lax.fori_loop(0, n // rows, blk, 0)

    finish(zl_ref, out_refs[0], n_ctx, n_lat)
    if ctx_out:
        finish(zc_ref, out_refs[1], 0, n_ctx)


def _gdn(rawc, gtc, zc, rawl, gtl, zl, convw, alog, dtb, nw, *, ctx_out):
    bsz, n_ctx, _ = rawc.shape
    n_lat = rawl.shape[1]
    nt = n_ctx + n_lat
    seq = lambda a: pl.BlockSpec((1,) + a.shape[1:], lambda b: (b, 0, 0))
    const = lambda a: pl.BlockSpec(a.shape, lambda b: (0,) * a.ndim)
    out_shape = [jax.ShapeDtypeStruct((bsz, n_lat, C_WIDTH), MXU)]
    if ctx_out:
        out_shape.append(jax.ShapeDtypeStruct((bsz, n_ctx, C_WIDTH), MXU))
    big = lambda: pltpu.VMEM((nt, C_WIDTH), F32)
    res = pl.pallas_call(
        functools.partial(_gdn_kernel, n_ctx=n_ctx, n_lat=n_lat, ctx_out=ctx_out),
        grid=(bsz,),
        in_specs=[seq(rawc), seq(gtc), seq(zc), seq(rawl), seq(gtl), seq(zl),
                  const(convw), const(alog), const(dtb), const(nw)],
        out_specs=[seq(s) for s in out_shape],
        out_shape=out_shape,
        scratch_shapes=[big() for _ in range(9)] + [pltpu.VMEM((C_WIDTH, C_WIDTH), F32)] * 2,
        compiler_params=pltpu.CompilerParams(dimension_semantics=("arbitrary",), vmem_limit_bytes=VMEM_LIMIT),
        name="gated_deltanet_bidir",
    )(rawc, gtc, zc, rawl, gtl, zl, convw, alog, dtb, nw)
    return res


def _out_kernel(x_ref, mod_ref, ya_ref, yb_ref, yc_ref, w_ref, fw_ref, o_ref, *, final):
    d = D_MODEL
    y = (jnp.dot(ya_ref[0], w_ref[0:A_WIDTH], preferred_element_type=F32)
         + jnp.dot(yb_ref[0], w_ref[A_WIDTH:A_WIDTH + B_WIDTH], preferred_element_type=F32)
         + jnp.dot(yc_ref[0], w_ref[A_WIDTH + B_WIDTH:], preferred_element_type=F32))
    x = x_ref[0] + mod_ref[0][:, 2 * d:] * y
    if final:
        x = x * lax.rsqrt(jnp.mean(x * x, axis=-1, keepdims=True) + NORM_EPS) * fw_ref[...]
    o_ref[0] = x


def _out(x, mod, ya, yb, yc, w_out, fw, *, final, tile):
    bsz, n, d = x.shape
    mod_map = (lambda b, i: (b, 0, 0)) if mod.shape[0] == bsz else (lambda b, i: (0, 0, 0))
    tok = lambda w: pl.BlockSpec((1, tile, w), lambda b, i: (b, i, 0))
    const = lambda a: pl.BlockSpec(a.shape, lambda b, i: (0,) * a.ndim)
    return pl.pallas_call(
        functools.partial(_out_kernel, final=final),
        grid=(bsz, n // tile),
        in_specs=[tok(d), pl.BlockSpec((1, 1, 3 * d), mod_map), tok(A_WIDTH), tok(B_WIDTH), tok(C_WIDTH),
                  const(w_out), const(fw)],
        out_specs=tok(d),
        out_shape=jax.ShapeDtypeStruct((bsz, n, d), F32),
        compiler_params=pltpu.CompilerParams(dimension_semantics=("arbitrary", "arbitrary"),
                                             vmem_limit_bytes=VMEM_LIMIT),
        name="outproj_residual_final" if final else "outproj_residual",
    )(x, mod, ya, yb, yc, w_out, fw)


def _rope_tables(n):
    pos = jnp.arange(n)
    half = B_HEAD_DIM // 2
    inv_freq = ROPE_THETA ** (-jnp.arange(0, half, 2, dtype=F32) / half)
    lane = np.arange(B_WIDTH)
    dd = lane % B_HEAD_DIM
    use_col = (dd // half) == 1
    second = ((dd % half) // (half // 2)) == 1
    fidx = dd % (half // 2)
    p = jnp.where(use_col[None, :], (pos % GRID_W)[:, None], (pos // GRID_W)[:, None]).astype(F32)
    ang = p * inv_freq[fidx][None, :]
    cos, sin = jnp.cos(ang), jnp.sin(ang)
    return cos, jnp.where(second[None, :], 0.0, -sin), jnp.where(second[None, :], sin, 0.0)


def kernel(x, c, ctx, c_ctx, w_ada, b_ada, norm_w, w_in, w_out, a_ln_w, a_ws, a_bs, b_lam, b_subln_w,
           c_conv_w, c_a_log, c_dt_bias, c_norm_w, final_norm_w):
    bsz, n, d = x.shape
    n_ctx = ctx.shape[1]
    depth = w_ada.shape[0]
    assert d == D_MODEL and n % PREP_ROWS == 0 and n_ctx % A_CHUNK == 0 and n % GRID_W == 0
    tile = 256
    tile_ctx = min(tile, n_ctx)

    rows = bsz + 1
    rows_pad = -(-rows // 8) * 8
    c_rows = jnp.concatenate([c, c_ctx[None, :], jnp.zeros((rows_pad - rows, d), F32)], axis=0)
    mods = _ada(c_rows, w_ada, b_ada)

    tabs = _rope_tables(n)
    tabs_ctx = tuple(t[:tile_ctx] for t in tabs)
    gate_pad = jnp.zeros((depth, d, GATE_PAD - N_GATE), w_in.dtype)
    w_in_p = jnp.concatenate([w_in, gate_pad], axis=-1).astype(MXU)
    w_out_b = w_out.astype(MXU)
    aws = a_ws.reshape(depth, A_GROUPS * A_CHUNK, A_CHUNK).astype(MXU)
    abias = jnp.repeat(jnp.swapaxes(a_bs, 1, 2), A_WIDTH // A_GROUPS, axis=2)
    row = lambda a: a.reshape(1, -1)

    xc = ctx
    for i in range(depth):
        lam_init = 0.8 - 0.6 * math.exp(-0.3 * i)
        ctx_out = i < depth - 1
        mod_l = mods[i, :bsz][:, None, :]
        mod_c = mods[i, bsz:bsz + 1][:, None, :]
        common = (row(norm_w[i]), w_in_p[i])
        a_par = (row(a_ln_w[i]), aws[i], abias[i])
        ya, qb, kb, vb, zb, craw, zc, gt = _proj(x, mod_l, *common, tabs, *a_par, rope=True, tile=tile)
        ya_c, qb_c, kb_c, vb_c, zb_c, craw_c, zc_c, gt_c = _proj(xc, mod_c, *common, tabs_ctx, *a_par,
                                                                 rope=False, tile=tile_ctx)
        lam_p, sw = b_lam[i], row(b_subln_w[i])
        yb = _attn(qb, zb, lam_p, sw, [kb_c, kb], [vb_c, vb], lam_init=lam_init, tile=tile)
        conv_w = c_conv_w[i]
        gdn_par = (conv_w, c_a_log[i], c_dt_bias[i], jnp.tile(c_norm_w[i], C_HEADS)[None, :])
        yc_all = _gdn(craw_c, gt_c, zc_c, craw, gt, zc, *gdn_par, ctx_out=ctx_out)
        final = i == depth - 1
        x = _out(x, mod_l, ya, yb, yc_all[0], w_out_b[i], row(final_norm_w), final=final, tile=tile)
        if ctx_out:
            yb_c = _attn(qb_c, zb_c, lam_p, sw, [kb_c], [vb_c], lam_init=lam_init, tile=tile_ctx)
            xc = _out(xc, mod_c, ya_c, yb_c, yc_all[1], w_out_b[i], row(final_norm_w), final=False, tile=tile_ctx)
    return x
```

```python
import functools
import math

import numpy as np
import jax
import jax.numpy as jnp
from jax import lax
from jax.experimental import pallas as pl
from jax.experimental.pallas import tpu as pltpu

F32 = jnp.float32
MXU = jnp.bfloat16

D_MODEL = 1024
NORM_EPS = 1e-6
GRID_W = 64
A_WIDTH, A_GROUPS, A_CHUNK, A_LN_EPS = 256, 4, 128, 1e-5
B_HEADS, B_HEAD_DIM, B_WIDTH = 4, 64, 512
ROPE_THETA = 10000.0
C_HEADS, C_HEAD_DIM, C_WIDTH, C_CONV, C_CHUNK = 4, 64, 256, 5, 64
N_GATE = 4 * C_HEADS
GATE_PAD = 128
OFF_A, OFF_B, OFF_C, OFF_G = 0, 3 * A_WIDTH, 3 * A_WIDTH + 4 * B_WIDTH, 3 * A_WIDTH + 4 * B_WIDTH + 4 * C_WIDTH
D_IN_PAD = OFF_G + GATE_PAD
NEG_BIG = -1e30
VMEM_LIMIT = 56 * 1024 * 1024
GDN_BLOCK = 256
GDN_LOCKSTEP = 8
HALO = 8
INV_BLOCK = 16


def _mm(a, b):
    return jnp.dot(a.astype(MXU), b.astype(MXU), preferred_element_type=F32)


def _mm_nt(a, b):
    return lax.dot_general(a.astype(MXU), b.astype(MXU), (((1,), (1,)), ((), ())), preferred_element_type=F32)


def _mm_tn(a, b):
    return lax.dot_general(a.astype(MXU), b.astype(MXU), (((0,), (0,)), ((), ())), preferred_element_type=F32)


def _split(a):
    hi = a.astype(MXU)
    return hi, (a - hi.astype(F32)).astype(MXU)


def _mm3(a, b):
    ah, al = _split(a)
    bh, bl = _split(b)
    d = lambda x, y: jnp.dot(x, y, preferred_element_type=F32)
    return d(ah, bh) + (d(ah, bl) + d(al, bh))


def _sigmoid(x):
    return 1.0 / (1.0 + jnp.exp(-x))


def _silu(x):
    return x * _sigmoid(x)


def _iota(shape, dim):
    return lax.broadcasted_iota(jnp.int32, shape, dim)


def _ada_kernel(c_ref, w_ref, b_ref, o_ref):
    o_ref[0] = _mm3(_silu(c_ref[...]), w_ref[0]) + b_ref[0]


def _ada(c_rows, w_ada, b_ada):
    depth, d, d3 = w_ada.shape
    r = c_rows.shape[0]
    tn = 512
    return pl.pallas_call(
        _ada_kernel,
        grid=(depth, d3 // tn),
        in_specs=[pl.BlockSpec((r, d), lambda l, j: (0, 0)),
                  pl.BlockSpec((1, d, tn), lambda l, j: (l, 0, j)),
                  pl.BlockSpec((1, 1, tn), lambda l, j: (l, 0, j))],
        out_specs=pl.BlockSpec((1, r, tn), lambda l, j: (l, 0, j)),
        out_shape=jax.ShapeDtypeStruct((depth, r, d3), F32),
        name="ada_modulation",
    )(c_rows, w_ada, b_ada.reshape(depth, 1, d3))


def _proj_kernel(x_ref, mod_ref, nw_ref, w_ref, cos_ref, sinn_ref, sinp_ref, lnw_ref, aws_ref, abias_ref,
                 ya_ref, q_ref, k_ref, v_ref, zb_ref, craw_ref, zc_ref, g_ref, *, rope):
    d = D_MODEL
    x = x_ref[0]
    mod = mod_ref[0]
    y = x * lax.rsqrt(jnp.mean(x * x, axis=-1, keepdims=True) + NORM_EPS) * nw_ref[...]
    h = (y * (1.0 + mod[:, d:2 * d]) + mod[:, :d]).astype(MXU)
    t = x.shape[0]

    pa = jnp.dot(h, w_ref[:, OFF_A:OFF_B], preferred_element_type=F32)
    u, v, z = pa[:, :A_WIDTH], pa[:, A_WIDTH:2 * A_WIDTH], pa[:, 2 * A_WIDTH:]
    vc = v - jnp.mean(v, axis=-1, keepdims=True)
    vn = (vc * lax.rsqrt(jnp.mean(vc * vc, axis=-1, keepdims=True) + A_LN_EPS) * lnw_ref[...]).astype(MXU)
    grp = _iota((A_CHUNK, A_WIDTH), 1) // (A_WIDTH // A_GROUPS)
    mixed = []
    for c in range(t // A_CHUNK):
        r = jnp.dot(aws_ref[...], vn[c * A_CHUNK:(c + 1) * A_CHUNK], preferred_element_type=F32)
        s = abias_ref[...]
        for g in range(A_GROUPS):
            s = s + jnp.where(grp == g, r[g * A_CHUNK:(g + 1) * A_CHUNK], 0.0)
        mixed.append(s)
    s_all = jnp.concatenate(mixed, axis=0) if len(mixed) > 1 else mixed[0]
    ya_ref[0] = (u * s_all * _silu(z)).astype(ya_ref.dtype)

    pb = jnp.dot(h, w_ref[:, OFF_B:OFF_C], preferred_element_type=F32)
    q, k = pb[:, :B_WIDTH], pb[:, B_WIDTH:2 * B_WIDTH]
    if rope:
        cos, sinn, sinp = cos_ref[...], sinn_ref[...], sinp_ref[...]
        half = B_HEAD_DIM // 4
        rot = lambda a: (a * cos + pltpu.roll(a, B_WIDTH - half, 1) * sinn + pltpu.roll(a, half, 1) * sinp)
        q, k = rot(q), rot(k)
    q_ref[0] = (q * (B_HEAD_DIM ** -0.5)).astype(q_ref.dtype)
    k_ref[0] = k.astype(k_ref.dtype)
    v_ref[0] = pb[:, 2 * B_WIDTH:3 * B_WIDTH].astype(v_ref.dtype)
    zb_ref[0] = pb[:, 3 * B_WIDTH:].astype(zb_ref.dtype)

    pc = jnp.dot(h, w_ref[:, OFF_C:OFF_G], preferred_element_type=F32)
    craw_ref[0] = pc[:, :3 * C_WIDTH]
    zc_ref[0] = pc[:, 3 * C_WIDTH:]
    pg = jnp.dot(h, w_ref[:, OFF_G:], preferred_element_type=F32)
    g_ref[0] = pg


def _proj(x, mod, nw, w_in, tabs, lnw, aws, abias, *, rope, tile):
    bsz, n, d = x.shape
    cos, sinn, sinp = tabs
    mod_map = (lambda i, b: (b, 0, 0)) if mod.shape[0] == bsz else (lambda i, b: (0, 0, 0))
    tok = lambda w: pl.BlockSpec((1, tile, w), lambda i, b: (b, i, 0))
    const = lambda a: pl.BlockSpec(a.shape, lambda i, b: (0,) * a.ndim)
    tab = pl.BlockSpec((tile, B_WIDTH), lambda i, b: (i, 0))
    outs = [(A_WIDTH, MXU), (B_WIDTH, MXU), (B_WIDTH, MXU), (B_WIDTH, MXU), (B_WIDTH, F32),
            (3 * C_WIDTH, F32), (C_WIDTH, F32), (GATE_PAD, F32)]
    return pl.pallas_call(
        functools.partial(_proj_kernel, rope=rope),
        grid=(n // tile, bsz),
        in_specs=[tok(d), pl.BlockSpec((1, 1, 3 * d), mod_map), const(nw), const(w_in), tab, tab, tab,
                  const(lnw), const(aws), const(abias)],
        out_specs=[tok(w) for w, _ in outs],
        out_shape=[jax.ShapeDtypeStruct((bsz, n, w), dt) for w, dt in outs],
        compiler_params=pltpu.CompilerParams(dimension_semantics=("arbitrary", "arbitrary"),
                                             vmem_limit_bytes=VMEM_LIMIT),
        name="norm_inproj_rope" if rope else "norm_inproj_ctx",
    )(x, mod, nw, w_in, cos, sinn, sinp, lnw, aws, abias)


def _attn_kernel(*refs, nseg, lam_init):
    q_ref, z_ref, lam_ref, sw_ref = refs[:4]
    k_refs, v_refs = refs[4:4 + nseg], refs[4 + nseg:4 + 2 * nseg]
    o_ref = refs[-1]
    lf = lam_ref[...]
    lam = (jnp.exp(jnp.sum(lf[0:1] * lf[1:2], axis=-1, keepdims=True))
           - jnp.exp(jnp.sum(lf[2:3] * lf[3:4], axis=-1, keepdims=True)) + lam_init)
    q = q_ref[0]
    pair = 2 * B_HEAD_DIM
    first = _iota((1, pair), 1) < B_HEAD_DIM
    zero = jnp.zeros((), q.dtype)
    heads = []
    for hh in range(B_HEADS):
        lanes = slice(hh * pair, (hh + 1) * pair)
        qp = q[:, lanes]
        acc = None
        for j in range(2):
            qm = jnp.where(first if j == 0 else jnp.logical_not(first), qp, zero)
            ss = [_mm_nt(qm, kr[0, :, lanes]) for kr in k_refs]
            m = functools.reduce(jnp.maximum, [jnp.max(s, axis=-1, keepdims=True) for s in ss])
            es = [jnp.exp(s - m) for s in ss]
            den = functools.reduce(lambda a, b: a + b, [jnp.sum(e, axis=-1, keepdims=True) for e in es])
            pv = functools.reduce(lambda a, b: a + b, [_mm(e, vr[0, :, lanes]) for e, vr in zip(es, v_refs)])
            r = pv * (1.0 / den)
            acc = r if j == 0 else acc - lam * r
        o = acc * lax.rsqrt(jnp.mean(acc * acc, axis=-1, keepdims=True) + NORM_EPS) * sw_ref[...]
        heads.append(o * (1.0 - lam_init))
    o_ref[0] = (jnp.concatenate(heads, axis=-1) * _silu(z_ref[0])).astype(o_ref.dtype)


def _attn(q, z, lam_p, subln_w, ks, vs, *, lam_init, tile):
    bsz, n, w = q.shape
    nseg = len(ks)
    tok = pl.BlockSpec((1, tile, w), lambda b, i: (b, i, 0))
    const = lambda a: pl.BlockSpec(a.shape, lambda b, i: (0,) * a.ndim)
    full = lambda a: pl.BlockSpec((1,) + a.shape[1:], lambda b, i: (b, 0, 0))
    return pl.pallas_call(
        functools.partial(_attn_kernel, nseg=nseg, lam_init=lam_init),
        grid=(bsz, n // tile),
        in_specs=[tok, tok, const(lam_p), const(subln_w)] + [full(a) for a in ks] + [full(a) for a in vs],
        out_specs=tok,
        out_shape=jax.ShapeDtypeStruct((bsz, n, w), MXU),
        compiler_params=pltpu.CompilerParams(dimension_semantics=("arbitrary", "arbitrary"),
                                             vmem_limit_bytes=VMEM_LIMIT),
        name="diff_attention_%dseg" % nseg,
    )(q, z, lam_p, subln_w, *ks, *vs)


def _head_of_lane(shape, dim):
    return _iota(shape, dim) // C_HEAD_DIM


def _expand_heads(cols):
    r = cols.shape[0]
    hl = _head_of_lane((r, C_WIDTH), 1)
    out = jnp.broadcast_to(cols[:, C_HEADS - 1:C_HEADS], (r, C_WIDTH))
    for h in range(C_HEADS - 2, -1, -1):
        out = jnp.where(hl == h, jnp.broadcast_to(cols[:, h:h + 1], (r, C_WIDTH)), out)
    return out


def _split3(a):
    a1 = a.astype(MXU)
    r1 = a - a1.astype(F32)
    a2 = r1.astype(MXU)
    return a1, a2, (r1 - a2.astype(F32)).astype(MXU)


def _sel_mm(sel, a):
    d = lambda x: jnp.dot(sel, x, preferred_element_type=F32)
    a1, a2, a3 = _split3(a)
    return d(a1) + (d(a2) + d(a3))


def _mm_sel(a, sel):
    d = lambda x: jnp.dot(x, sel, preferred_element_type=F32)
    a1, a2, a3 = _split3(a)
    return d(a1) + (d(a2) + d(a3))


def _same_head():
    return (_head_of_lane((C_WIDTH, C_WIDTH), 0) == _head_of_lane((C_WIDTH, C_WIDTH), 1)).astype(MXU)


def _gdn_prep(raw_ref, gate_ref, n, off, convw_ref, alog_ref, dtb_ref, q_s, k_s, v_s, bf_s, bb_s, gf_s, gb_s):
    rows = GDN_BLOCK
    nblk = n // rows
    same_head = _same_head()
    pad = C_CONV // 2
    ri, rj = _iota((rows, rows), 0), _iota((rows, rows), 1)
    same_chunk = (ri // C_CHUNK) == (rj // C_CHUNK)
    csum_f = (same_chunk & (rj <= ri)).astype(MXU)
    csum_b = (same_chunk & (rj >= ri)).astype(MXU)

    def block(i, carry):
        base = pl.multiple_of(i * rows, rows)
        cur = raw_ref[0, pl.ds(base, rows), :]
        prev = raw_ref[0, pl.ds(pl.multiple_of(jnp.maximum(base - HALO, 0), HALO), HALO), :]
        nxt = raw_ref[0, pl.ds(pl.multiple_of(jnp.minimum(base + rows, n - HALO), HALO), HALO), :]
        prev = jnp.where(i == 0, 0.0, prev)
        nxt = jnp.where(i == nblk - 1, 0.0, nxt)
        ext = jnp.concatenate([prev, cur, nxt], axis=0)
        w = convw_ref[...]
        acc = cur * w[pad:pad + 1]
        for j in range(C_CONV):
            if j != pad:
                acc = acc + pltpu.roll(ext, (pad - j) % (rows + 2 * HALO), 0)[HALO:HALO + rows] * w[j:j + 1]
        act = _silu(acc)
        qh, kh, vh = act[:, :C_WIDTH], act[:, C_WIDTH:2 * C_WIDTH], act[:, 2 * C_WIDTH:]
        l2 = lambda a: a * lax.rsqrt(_mm_sel(a * a, same_head) + 1e-6)
        dst = pl.ds(pl.multiple_of(off + base, rows), rows)
        q_s[dst, :] = l2(qh) * (C_HEAD_DIM ** -0.5)
        k_s[dst, :] = l2(kh)
        v_s[dst, :] = vh
        gt = gate_ref[0, pl.ds(base, rows), :]
        h = C_HEADS
        beta = _sigmoid(gt)
        bf_s[dst, :] = _expand_heads(beta[:, 0:h])
        bb_s[dst, :] = _expand_heads(beta[:, h:2 * h])
        sp = jnp.maximum(gt + dtb_ref[...], 0.0) + jnp.log1p(jnp.exp(-jnp.abs(gt + dtb_ref[...])))
        g = -jnp.exp(alog_ref[...]) * sp
        gf_s[dst, :] = _expand_heads(_sel_mm(csum_f, g)[:, 2 * h:3 * h])
        gb_s[dst, :] = _expand_heads(_sel_mm(csum_b, g)[:, 3 * h:4 * h])
        return carry

    lax.fori_loop(0, nblk, block, 0)


def _block_diag(a, bmask):
    return jnp.where(bmask, jnp.concatenate([a] * C_HEADS, axis=0), 0.0)


def _gdn_local(chains, q_s, k_s, v_s, bufs):
    c, wd = C_CHUNK, C_WIDTH
    u_buf, w_buf, a_buf, qd_buf, kd_buf, eg_buf = bufs
    ii = _iota((c, wd), 0)
    jj = _iota((c, wd), 1) % c
    eye = jj == ii
    bmask = _head_of_lane((C_HEADS * c, wd), 1) == (_iota((C_HEADS * c, wd), 0) // c)
    bd = lambda a: _block_diag(a, bmask)
    ident = jnp.where(eye, 1.0, 0.0)

    st = []
    for rows, backward, beta_s, gc_s, _, _ in chains:
        q, k, v, beta, gc = q_s[rows, :], k_s[rows, :], v_s[rows, :], beta_s[rows, :], gc_s[rows, :]
        incl = (jj >= ii) if backward else (jj <= ii)
        strict = (jj > ii) if backward else (jj < ii)
        grow = jnp.sum(jnp.where(eye, gc, 0.0), axis=0, keepdims=True)
        decay = jnp.exp(jnp.where(incl, gc - grow, NEG_BIG))
        glast = gc[0:1] if backward else gc[c - 1:c]
        st.append(dict(q=q, k=k, v=v, beta=beta, gc=gc, decay=decay, strict=strict, glast=glast, kb=k * beta,
                       egc=jnp.exp(gc)))
    for s in st:
        qk = _mm_nt(jnp.concatenate([s["kb"], s["q"]], axis=0), bd(s["k"]))
        s["lmat"] = jnp.where(s["strict"], qk[:c] * s["decay"], 0.0)
        s["attn"] = qk[c:] * s["decay"]
    in_blk = (ii // INV_BLOCK) == (jj // INV_BLOCK)
    for s in st:
        dmat = jnp.where(in_blk, s["lmat"], 0.0)
        s["t"] = ident - dmat
        s["pw"] = _mm(dmat, bd(dmat))
    nsq = int(math.log2(INV_BLOCK)) - 1
    for step in range(nsq):
        for s in st:
            b = bd(s["pw"])
            if step < nsq - 1:
                r = _mm(jnp.concatenate([s["t"], s["pw"]], axis=0), b)
                s["t"], s["pw"] = s["t"] + r[:c], r[c:]
            else:
                s["t"] = s["t"] + _mm(s["t"], b)
    size = INV_BLOCK
    while size < c:
        emask = ((ii // (2 * size)) == (jj // (2 * size))) & ((ii // size) != (jj // size))
        for s in st:
            s["y"] = _mm(jnp.where(emask, s["lmat"], 0.0), bd(s["t"]))
        for s in st:
            s["t"] = s["t"] - _mm(s["t"], bd(s["y"]))
        size *= 2
    for s, (_, _, _, _, d, slot) in zip(st, chains):
        dst = slice(slot * c, (slot + 1) * c)
        u_buf[d, dst, :] = _mm(s["t"], bd(s["v"] * s["beta"]))
        w_buf[d, dst, :] = _mm(s["t"], bd(s["kb"] * s["egc"])).astype(MXU)
        a_buf[d, dst, :] = s["attn"].astype(MXU)
        qd_buf[d, dst, :] = (s["q"] * s["egc"]).astype(MXU)
        kd_buf[d, dst, :] = (s["k"] * jnp.exp(s["glast"] - s["gc"])).astype(MXU)
        eg_buf[d, slot * HALO:(slot + 1) * HALO, :] = jnp.broadcast_to(jnp.exp(s["glast"]), (HALO, wd))


def _gdn_scan(steps, states, bufs):
    c, wd = C_CHUNK, C_WIDTH
    u_buf, w_buf, a_buf, qd_buf, kd_buf, eg_buf = bufs
    bmask = _head_of_lane((C_HEADS * c, wd), 1) == (_iota((C_HEADS * c, wd), 0) // c)
    smask = _head_of_lane((wd, wd), 0) == _head_of_lane((wd, wd), 1)
    outs = []
    for slots in steps:
        src = [slice(slot * c, (slot + 1) * c) for slot in slots]
        ws = [_mm(jnp.concatenate([w_buf[d, r, :], qd_buf[d, r, :]], axis=0), states[d]) for d, r in enumerate(src)]
        v_new = [u_buf[d, r, :] - ws[d][:c] for d, r in enumerate(src)]
        outs.append([ws[d][c:] + _mm(a_buf[d, r, :], _block_diag(v_new[d], bmask)) for d, r in enumerate(src)])
        upd = [_mm_tn(kd_buf[d, r, :], v_new[d]) for d, r in enumerate(src)]
        states = [states[d] * eg_buf[d, slot * HALO:slot * HALO + 1, :] + jnp.where(smask, upd[d], 0.0)
                  for d, slot in enumerate(slots)]
    return states, outs


def _gdn_kernel(*refs, n_ctx, n_lat, ctx_out):
    (rawc_ref, gtc_ref, zc_ref, rawl_ref, gtl_ref, zl_ref, convw_ref, alog_ref, dtb_ref, nw_ref) = refs[:10]
    nout = 2 if ctx_out else 1
    out_refs = refs[10:10 + nout]
    (q_s, k_s, v_s, bf_s, bb_s, gf_s, gb_s, of_s, ob_s, st_s, u_buf, w_buf, a_buf, qd_buf, kd_buf, eg_buf) = refs[10 + nout:]
    scr = (q_s, k_s, v_s, bf_s, bb_s, gf_s, gb_s)
    bufs = (u_buf, w_buf, a_buf, qd_buf, kd_buf, eg_buf)
    _gdn_prep(rawc_ref, gtc_ref, n_ctx, 0, convw_ref, alog_ref, dtb_ref, *scr)
    _gdn_prep(rawl_ref, gtl_ref, n_lat, n_ctx, convw_ref, alog_ref, dtb_ref, *scr)
    st_s[...] = jnp.zeros_like(st_s)
    ncb, nlb = n_ctx // GDN_BLOCK, n_lat // GDN_BLOCK
    per = GDN_BLOCK // C_CHUNK

    def block(i, carry):
        bf = pl.multiple_of(i * GDN_BLOCK, GDN_BLOCK)
        bb = pl.multiple_of(jnp.where(i < ncb, ncb - 1 - i, 2 * ncb + nlb - 1 - i) * GDN_BLOCK, GDN_BLOCK)
        rows_f = [pl.ds(bf + r * C_CHUNK, C_CHUNK) for r in range(per)]
        rows_b = [pl.ds(bb + r * C_CHUNK, C_CHUNK) for r in range(per)]
        chains = ([(rows_f[r], False, bf_s, gf_s, 0, r) for r in range(per)]
                  + [(rows_b[r], True, bb_s, gb_s, 1, r) for r in range(per)])
        for lo in range(0, len(chains), GDN_LOCKSTEP):
            _gdn_local(chains[lo:lo + GDN_LOCKSTEP], q_s, k_s, v_s, bufs)
        steps = [(t, per - 1 - t) for t in range(per)]
        states, outs = _gdn_scan(steps, [st_s[0], st_s[1]], bufs)
        st_s[0], st_s[1] = states
        for (tf, tb), (o_f, o_b) in zip(steps, outs):
            of_s[rows_f[tf], :] = o_f
            ob_s[rows_b[tb], :] = o_b
        return carry

    lax.fori_loop(0, ncb + nlb, block, 0)

    same_head = _same_head()

    def finish(z_ref, out_ref, off, n):
        rows = GDN_BLOCK

        def blk(i, carry):
            base = pl.multiple_of(i * rows, rows)
            src = pl.ds(pl.multiple_of(off + base, rows), rows)
            o = of_s[src, :] + ob_s[src, :]
            ms = _mm_sel(o * o, same_head) * (1.0 / C_HEAD_DIM)
            y = o * lax.rsqrt(ms + NORM_EPS) * nw_ref[...]
            out_ref[0, pl.ds(base, rows), :] = (y * _silu(z_ref[0, pl.ds(base, rows), :])).astype(out_ref.dtype)
            return carry

        lax.fori_loop(0, n // rows, blk, 0)

    finish(zl_ref, out_refs[0], n_ctx, n_lat)
    if ctx_out:
        finish(zc_ref, out_refs[1], 0, n_ctx)


def _gdn(rawc, gtc, zc, rawl, gtl, zl, convw, alog, dtb, nw, *, ctx_out):
    bsz, n_ctx, _ = rawc.shape
    n_lat = rawl.shape[1]
    nt = n_ctx + n_lat
    seq = lambda a: pl.BlockSpec((1,) + a.shape[1:], lambda b: (b, 0, 0))
    const = lambda a: pl.BlockSpec(a.shape, lambda b: (0,) * a.ndim)
    out_shape = [jax.ShapeDtypeStruct((bsz, n_lat, C_WIDTH), MXU)]
    if ctx_out:
        out_shape.append(jax.ShapeDtypeStruct((bsz, n_ctx, C_WIDTH), MXU))
    big = lambda: pltpu.VMEM((nt, C_WIDTH), F32)
    loc = lambda dt: pltpu.VMEM((2, GDN_BLOCK, C_WIDTH), dt)
    per = GDN_BLOCK // C_CHUNK
    res = pl.pallas_call(
        functools.partial(_gdn_kernel, n_ctx=n_ctx, n_lat=n_lat, ctx_out=ctx_out),
        grid=(bsz,),
        in_specs=[seq(rawc), seq(gtc), seq(zc), seq(rawl), seq(gtl), seq(zl),
                  const(convw), const(alog), const(dtb), const(nw)],
        out_specs=[seq(s) for s in out_shape],
        out_shape=out_shape,
        scratch_shapes=[big() for _ in range(9)] + [pltpu.VMEM((2, C_WIDTH, C_WIDTH), F32)]
        + [loc(F32), loc(MXU), loc(MXU), loc(MXU), loc(MXU), pltpu.VMEM((2, per * HALO, C_WIDTH), F32)],
        compiler_params=pltpu.CompilerParams(dimension_semantics=("arbitrary",), vmem_limit_bytes=VMEM_LIMIT),
        name="gated_deltanet_bidir",
    )(rawc, gtc, zc, rawl, gtl, zl, convw, alog, dtb, nw)
    return res


def _out_kernel(x_ref, mod_ref, ya_ref, yb_ref, yc_ref, w_ref, fw_ref, o_ref, *, final):
    d = D_MODEL
    y = (jnp.dot(ya_ref[0], w_ref[0:A_WIDTH], preferred_element_type=F32)
         + jnp.dot(yb_ref[0], w_ref[A_WIDTH:A_WIDTH + B_WIDTH], preferred_element_type=F32)
         + jnp.dot(yc_ref[0], w_ref[A_WIDTH + B_WIDTH:], preferred_element_type=F32))
    x = x_ref[0] + mod_ref[0][:, 2 * d:] * y
    if final:
        x = x * lax.rsqrt(jnp.mean(x * x, axis=-1, keepdims=True) + NORM_EPS) * fw_ref[...]
    o_ref[0] = x


def _out(x, mod, ya, yb, yc, w_out, fw, *, final, tile):
    bsz, n, d = x.shape
    mod_map = (lambda b, i: (b, 0, 0)) if mod.shape[0] == bsz else (lambda b, i: (0, 0, 0))
    tok = lambda w: pl.BlockSpec((1, tile, w), lambda b, i: (b, i, 0))
    const = lambda a: pl.BlockSpec(a.shape, lambda b, i: (0,) * a.ndim)
    return pl.pallas_call(
        functools.partial(_out_kernel, final=final),
        grid=(bsz, n // tile),
        in_specs=[tok(d), pl.BlockSpec((1, 1, 3 * d), mod_map), tok(A_WIDTH), tok(B_WIDTH), tok(C_WIDTH),
                  const(w_out), const(fw)],
        out_specs=tok(d),
        out_shape=jax.ShapeDtypeStruct((bsz, n, d), F32),
        compiler_params=pltpu.CompilerParams(dimension_semantics=("arbitrary", "arbitrary"),
                                             vmem_limit_bytes=VMEM_LIMIT),
        name="outproj_residual_final" if final else "outproj_residual",
    )(x, mod, ya, yb, yc, w_out, fw)


def _rope_tables(n):
    pos = jnp.arange(n)
    half = B_HEAD_DIM // 2
    inv_freq = ROPE_THETA ** (-jnp.arange(0, half, 2, dtype=F32) / half)
    lane = np.arange(B_WIDTH)
    dd = lane % B_HEAD_DIM
    use_col = (dd // half) == 1
    second = ((dd % half) // (half // 2)) == 1
    fidx = dd % (half // 2)
    p = jnp.where(use_col[None, :], (pos % GRID_W)[:, None], (pos // GRID_W)[:, None]).astype(F32)
    ang = p * inv_freq[fidx][None, :]
    cos, sin = jnp.cos(ang), jnp.sin(ang)
    return cos, jnp.where(second[None, :], 0.0, -sin), jnp.where(second[None, :], sin, 0.0)


def kernel(x, c, ctx, c_ctx, w_ada, b_ada, norm_w, w_in, w_out, a_ln_w, a_ws, a_bs, b_lam, b_subln_w,
           c_conv_w, c_a_log, c_dt_bias, c_norm_w, final_norm_w):
    bsz, n, d = x.shape
    n_ctx = ctx.shape[1]
    depth = w_ada.shape[0]
    assert d == D_MODEL and n % GDN_BLOCK == 0 and n_ctx % GDN_BLOCK == 0 and n % GRID_W == 0
    tile = 256
    tile_ctx = min(tile, n_ctx)

    rows = bsz + 1
    rows_pad = -(-rows // 8) * 8
    c_rows = jnp.concatenate([c, c_ctx[None, :], jnp.zeros((rows_pad - rows, d), F32)], axis=0)
    mods = _ada(c_rows, w_ada, b_ada)

    tabs = _rope_tables(n)
    tabs_ctx = tuple(t[:tile_ctx] for t in tabs)
    gate_pad = jnp.zeros((depth, d, GATE_PAD - N_GATE), w_in.dtype)
    w_in_p = jnp.concatenate([w_in, gate_pad], axis=-1).astype(MXU)
    w_out_b = w_out.astype(MXU)
    aws = a_ws.reshape(depth, A_GROUPS * A_CHUNK, A_CHUNK).astype(MXU)
    abias = jnp.repeat(jnp.swapaxes(a_bs, 1, 2), A_WIDTH // A_GROUPS, axis=2)
    row = lambda a: a.reshape(1, -1)

    xc = ctx
    for i in range(depth):
        lam_init = 0.8 - 0.6 * math.exp(-0.3 * i)
        ctx_out = i < depth - 1
        mod_l = mods[i, :bsz][:, None, :]
        mod_c = mods[i, bsz:bsz + 1][:, None, :]
        common = (row(norm_w[i]), w_in_p[i])
        a_par = (row(a_ln_w[i]), aws[i], abias[i])
        ya, qb, kb, vb, zb, craw, zc, gt = _proj(x, mod_l, *common, tabs, *a_par, rope=True, tile=tile)
        ya_c, qb_c, kb_c, vb_c, zb_c, craw_c, zc_c, gt_c = _proj(xc, mod_c, *common, tabs_ctx, *a_par,
                                                                 rope=False, tile=tile_ctx)
        lam_p, sw = b_lam[i], row(b_subln_w[i])
        yb = _attn(qb, zb, lam_p, sw, [kb_c, kb], [vb_c, vb], lam_init=lam_init, tile=tile)
        conv_w = c_conv_w[i]
        gate_row = lambda p: jnp.concatenate([jnp.zeros((2 * C_HEADS,), F32), p.reshape(-1),
                                              jnp.zeros((GATE_PAD - N_GATE,), F32)])[None, :]
        gdn_par = (conv_w, gate_row(c_a_log[i]), gate_row(c_dt_bias[i]), jnp.tile(c_norm_w[i], C_HEADS)[None, :])
        yc_all = _gdn(craw_c, gt_c, zc_c, craw, gt, zc, *gdn_par, ctx_out=ctx_out)
        final = i == depth - 1
        x = _out(x, mod_l, ya, yb, yc_all[0], w_out_b[i], row(final_norm_w), final=final, tile=tile)
        if ctx_out:
            yb_c = _attn(qb_c, zb_c, lam_p, sw, [kb_c], [vb_c], lam_init=lam_init, tile=tile_ctx)
            xc = _out(xc, mod_c, ya_c, yb_c, yc_all[1], w_out_b[i], row(final_norm_w), final=False, tile=tile_ctx)
    return x
```

```python
import functools
import math

import numpy as np
import jax
import jax.numpy as jnp
from jax import lax
from jax.experimental import pallas as pl
from jax.experimental.pallas import tpu as pltpu

F32 = jnp.float32
MXU = jnp.bfloat16

D_MODEL = 1024
NORM_EPS = 1e-6
GRID_W = 64
A_WIDTH, A_GROUPS, A_CHUNK, A_LN_EPS = 256, 4, 128, 1e-5
B_HEADS, B_HEAD_DIM, B_WIDTH = 4, 64, 512
ROPE_THETA = 10000.0
C_HEADS, C_HEAD_DIM, C_WIDTH, C_CONV, C_CHUNK = 4, 64, 256, 5, 64
N_GATE = 4 * C_HEADS
GATE_PAD = 128
OFF_A, OFF_B, OFF_C, OFF_G = 0, 3 * A_WIDTH, 3 * A_WIDTH + 4 * B_WIDTH, 3 * A_WIDTH + 4 * B_WIDTH + 4 * C_WIDTH
D_IN_PAD = OFF_G + GATE_PAD
NEG_BIG = -1e30
LOG2E = math.log2(math.e)
ATT_KCHUNK = 256
VMEM_LIMIT = 56 * 1024 * 1024
GDN_BLOCK = 256
GDN_LOCKSTEP = 8
HALO = 8
INV_BLOCK = 16


def _mm(a, b):
    return jnp.dot(a.astype(MXU), b.astype(MXU), preferred_element_type=F32)


def _mm_nt(a, b):
    return lax.dot_general(a.astype(MXU), b.astype(MXU), (((1,), (1,)), ((), ())), preferred_element_type=F32)


def _mm_tn(a, b):
    return lax.dot_general(a.astype(MXU), b.astype(MXU), (((0,), (0,)), ((), ())), preferred_element_type=F32)


def _split(a):
    hi = a.astype(MXU)
    return hi, (a - hi.astype(F32)).astype(MXU)


def _mm3(a, b):
    ah, al = _split(a)
    bh, bl = _split(b)
    d = lambda x, y: jnp.dot(x, y, preferred_element_type=F32)
    return d(ah, bh) + (d(ah, bl) + d(al, bh))


def _sigmoid(x):
    return 1.0 / (1.0 + jnp.exp(-x))


def _silu(x):
    return x * _sigmoid(x)


def _iota(shape, dim):
    return lax.broadcasted_iota(jnp.int32, shape, dim)


def _ada_kernel(c_ref, w_ref, b_ref, o_ref):
    o_ref[0] = _mm3(_silu(c_ref[...]), w_ref[0]) + b_ref[0]


def _ada(c_rows, w_ada, b_ada):
    depth, d, d3 = w_ada.shape
    r = c_rows.shape[0]
    tn = 512
    return pl.pallas_call(
        _ada_kernel,
        grid=(depth, d3 // tn),
        in_specs=[pl.BlockSpec((r, d), lambda l, j: (0, 0)),
                  pl.BlockSpec((1, d, tn), lambda l, j: (l, 0, j)),
                  pl.BlockSpec((1, 1, tn), lambda l, j: (l, 0, j))],
        out_specs=pl.BlockSpec((1, r, tn), lambda l, j: (l, 0, j)),
        out_shape=jax.ShapeDtypeStruct((depth, r, d3), F32),
        name="ada_modulation",
    )(c_rows, w_ada, b_ada.reshape(depth, 1, d3))


def _proj_kernel(x_ref, mod_ref, nw_ref, w_ref, cos_ref, sinn_ref, sinp_ref, lnw_ref, aws_ref, abias_ref,
                 ya_ref, q_ref, k_ref, v_ref, zb_ref, craw_ref, zc_ref, g_ref, *, rope):
    d = D_MODEL
    x = x_ref[0]
    mod = mod_ref[0]
    y = x * lax.rsqrt(jnp.mean(x * x, axis=-1, keepdims=True) + NORM_EPS) * nw_ref[...]
    h = (y * (1.0 + mod[:, d:2 * d]) + mod[:, :d]).astype(MXU)
    t = x.shape[0]

    pa = jnp.dot(h, w_ref[:, OFF_A:OFF_B], preferred_element_type=F32)
    u, v, z = pa[:, :A_WIDTH], pa[:, A_WIDTH:2 * A_WIDTH], pa[:, 2 * A_WIDTH:]
    vc = v - jnp.mean(v, axis=-1, keepdims=True)
    vn = (vc * lax.rsqrt(jnp.mean(vc * vc, axis=-1, keepdims=True) + A_LN_EPS) * lnw_ref[...]).astype(MXU)
    grp = _iota((A_CHUNK, A_WIDTH), 1) // (A_WIDTH // A_GROUPS)
    mixed = []
    for c in range(t // A_CHUNK):
        r = jnp.dot(aws_ref[...], vn[c * A_CHUNK:(c + 1) * A_CHUNK], preferred_element_type=F32)
        s = abias_ref[...]
        for g in range(A_GROUPS):
            s = s + jnp.where(grp == g, r[g * A_CHUNK:(g + 1) * A_CHUNK], 0.0)
        mixed.append(s)
    s_all = jnp.concatenate(mixed, axis=0) if len(mixed) > 1 else mixed[0]
    ya_ref[0] = (u * s_all * _silu(z)).astype(ya_ref.dtype)

    pb = jnp.dot(h, w_ref[:, OFF_B:OFF_C], preferred_element_type=F32)
    q, k = pb[:, :B_WIDTH], pb[:, B_WIDTH:2 * B_WIDTH]
    if rope:
        cos, sinn, sinp = cos_ref[...], sinn_ref[...], sinp_ref[...]
        half = B_HEAD_DIM // 4
        rot = lambda a: (a * cos + pltpu.roll(a, B_WIDTH - half, 1) * sinn + pltpu.roll(a, half, 1) * sinp)
        q, k = rot(q), rot(k)
    q_ref[0] = (q * (B_HEAD_DIM ** -0.5 * LOG2E)).astype(q_ref.dtype)
    k_ref[0] = k.astype(k_ref.dtype)
    v_ref[0] = pb[:, 2 * B_WIDTH:3 * B_WIDTH].T.astype(v_ref.dtype)
    zb_ref[0] = pb[:, 3 * B_WIDTH:].astype(zb_ref.dtype)

    pc = jnp.dot(h, w_ref[:, OFF_C:OFF_G], preferred_element_type=F32)
    craw_ref[0] = pc[:, :3 * C_WIDTH]
    zc_ref[0] = pc[:, 3 * C_WIDTH:]
    pg = jnp.dot(h, w_ref[:, OFF_G:], preferred_element_type=F32)
    g_ref[0] = pg


def _proj(x, mod, nw, w_in, tabs, lnw, aws, abias, *, rope, tile):
    bsz, n, d = x.shape
    cos, sinn, sinp = tabs
    mod_map = (lambda i, b: (b, 0, 0)) if mod.shape[0] == bsz else (lambda i, b: (0, 0, 0))
    tok = lambda w: pl.BlockSpec((1, tile, w), lambda i, b: (b, i, 0))
    const = lambda a: pl.BlockSpec(a.shape, lambda i, b: (0,) * a.ndim)
    tab = pl.BlockSpec((tile, B_WIDTH), lambda i, b: (i, 0))
    outs = [(A_WIDTH, MXU), (B_WIDTH, MXU), (B_WIDTH, MXU), (B_WIDTH, MXU), (B_WIDTH, F32),
            (3 * C_WIDTH, F32), (C_WIDTH, F32), (GATE_PAD, F32)]
    out_specs = [tok(w) for w, _ in outs]
    out_shape = [jax.ShapeDtypeStruct((bsz, n, w), dt) for w, dt in outs]
    out_specs[3] = pl.BlockSpec((1, B_WIDTH, tile), lambda i, b: (b, 0, i))
    out_shape[3] = jax.ShapeDtypeStruct((bsz, B_WIDTH, n), MXU)
    return pl.pallas_call(
        functools.partial(_proj_kernel, rope=rope),
        grid=(n // tile, bsz),
        in_specs=[tok(d), pl.BlockSpec((1, 1, 3 * d), mod_map), const(nw), const(w_in), tab, tab, tab,
                  const(lnw), const(aws), const(abias)],
        out_specs=out_specs,
        out_shape=out_shape,
        compiler_params=pltpu.CompilerParams(dimension_semantics=("arbitrary", "arbitrary"),
                                             vmem_limit_bytes=VMEM_LIMIT),
        name="norm_inproj_rope" if rope else "norm_inproj_ctx",
    )(x, mod, nw, w_in, cos, sinn, sinp, lnw, aws, abias)


def _interleave(gens):
    gens = list(gens)
    while gens:
        for g in list(gens):
            if next(g, StopIteration) is StopIteration:
                gens.remove(g)


def _attn_kernel(*refs, seg_lens, lam_init):
    nseg = len(seg_lens)
    q_ref, z_ref, lam_ref, sw_ref = refs[:4]
    k_refs, v_refs = refs[4:4 + nseg], refs[4 + nseg:4 + 2 * nseg]
    o_ref, s_buf, e_buf = refs[4 + 2 * nseg:]
    lf = lam_ref[...]
    lam = (jnp.exp(jnp.sum(lf[0:1] * lf[1:2], axis=-1, keepdims=True))
           - jnp.exp(jnp.sum(lf[2:3] * lf[3:4], axis=-1, keepdims=True)) + lam_init)
    q = q_ref[0]
    pair = 2 * B_HEAD_DIM
    first = _iota((1, pair), 1) < B_HEAD_DIM
    zero = jnp.zeros((), q.dtype)
    chunks, row = [], 0
    for seg, n in enumerate(seg_lens):
        for c0 in range(0, n, ATT_KCHUNK):
            chunks.append((seg, c0, row))
            row += ATT_KCHUNK
    kc = ATT_KCHUNK
    maxes, heads = {}, {}

    dens = {}

    def scores(i):
        hh, j = divmod(i, 2)
        lanes = slice(hh * pair, (hh + 1) * pair)
        qm = jnp.where(first if j == 0 else jnp.logical_not(first), q[:, lanes], zero)
        m = None
        for seg, c0, r0 in chunks:
            s = _mm_nt(k_refs[seg][0, c0:c0 + kc, lanes], qm)
            s_buf[hh % 2, j, r0:r0 + kc, :] = s
            cm = jnp.max(s, axis=0, keepdims=True)
            m = cm if m is None else jnp.maximum(m, cm)
            yield
        maxes[i] = m

    def exps(i):
        hh, j = divmod(i, 2)
        den = None
        for _, _, r0 in chunks:
            e = jnp.exp2(s_buf[hh % 2, j, r0:r0 + kc, :] - maxes[i])
            e_buf[hh % 2, j, r0:r0 + kc, :] = e.astype(MXU)
            cs = jnp.sum(e, axis=0, keepdims=True)
            den = cs if den is None else den + cs
            yield
        dens[i] = den

    def values(hh):
        lanes = slice(hh * pair, (hh + 1) * pair)
        d0, d1 = dens[2 * hh], dens[2 * hh + 1]
        coef = (lam * d0 * (1.0 / d1)).astype(MXU)
        pv = None
        for seg, c0, r0 in chunks:
            p = e_buf[hh % 2, 0, r0:r0 + kc, :] - coef * e_buf[hh % 2, 1, r0:r0 + kc, :]
            t = _mm(v_refs[seg][0, lanes, c0:c0 + kc], p)
            pv = t if pv is None else pv + t
            yield
        acc = (pv * (1.0 / d0)).T
        o = acc * lax.rsqrt(jnp.mean(acc * acc, axis=-1, keepdims=True) + NORM_EPS) * sw_ref[...]
        heads[hh] = o * (1.0 - lam_init)

    nh = 2 * B_HEADS
    for t in range(nh + 2):
        stage = []
        if t < nh:
            stage.append(scores(t))
        if 1 <= t <= nh:
            stage.append(exps(t - 1))
        if t >= 3 and t % 2 == 1:
            stage.append(values((t - 3) // 2))
        _interleave(stage)
    o_all = jnp.concatenate([heads[hh] for hh in range(B_HEADS)], axis=-1)
    o_ref[0] = (o_all * _silu(z_ref[0])).astype(o_ref.dtype)


def _attn(q, z, lam_p, subln_w, ks, vs, *, lam_init, tile):
    bsz, n, w = q.shape
    seg_lens = tuple(a.shape[1] for a in ks)
    tok = pl.BlockSpec((1, tile, w), lambda b, i: (b, i, 0))
    const = lambda a: pl.BlockSpec(a.shape, lambda b, i: (0,) * a.ndim)
    full = lambda a: pl.BlockSpec((1,) + a.shape[1:], lambda b, i: (b, 0, 0))
    return pl.pallas_call(
        functools.partial(_attn_kernel, seg_lens=seg_lens, lam_init=lam_init),
        grid=(bsz, n // tile),
        in_specs=[tok, tok, const(lam_p), const(subln_w)] + [full(a) for a in ks] + [full(a) for a in vs],
        out_specs=tok,
        out_shape=jax.ShapeDtypeStruct((bsz, n, w), MXU),
        scratch_shapes=[pltpu.VMEM((2, 2, sum(seg_lens), tile), F32), pltpu.VMEM((2, 2, sum(seg_lens), tile), MXU)],
        compiler_params=pltpu.CompilerParams(dimension_semantics=("arbitrary", "arbitrary"),
                                             vmem_limit_bytes=VMEM_LIMIT),
        name="diff_attention_%dseg" % len(ks),
    )(q, z, lam_p, subln_w, *ks, *vs)


def _head_of_lane(shape, dim):
    return _iota(shape, dim) // C_HEAD_DIM


def _expand_heads(cols):
    r = cols.shape[0]
    hl = _head_of_lane((r, C_WIDTH), 1)
    out = jnp.broadcast_to(cols[:, C_HEADS - 1:C_HEADS], (r, C_WIDTH))
    for h in range(C_HEADS - 2, -1, -1):
        out = jnp.where(hl == h, jnp.broadcast_to(cols[:, h:h + 1], (r, C_WIDTH)), out)
    return out


def _split3(a):
    a1 = a.astype(MXU)
    r1 = a - a1.astype(F32)
    a2 = r1.astype(MXU)
    return a1, a2, (r1 - a2.astype(F32)).astype(MXU)


def _sel_mm(sel, a):
    d = lambda x: jnp.dot(sel, x, preferred_element_type=F32)
    a1, a2, a3 = _split3(a)
    return d(a1) + (d(a2) + d(a3))


def _mm_sel(a, sel):
    d = lambda x: jnp.dot(x, sel, preferred_element_type=F32)
    a1, a2, a3 = _split3(a)
    return d(a1) + (d(a2) + d(a3))


def _same_head():
    return (_head_of_lane((C_WIDTH, C_WIDTH), 0) == _head_of_lane((C_WIDTH, C_WIDTH), 1)).astype(MXU)


def _gdn_prep(raw_ref, gate_ref, n, off, convw_ref, alog_ref, dtb_ref, q_s, k_s, v_s, bf_s, bb_s, gf_s, gb_s):
    rows = GDN_BLOCK
    nblk = n // rows
    same_head = _same_head()
    pad = C_CONV // 2
    ri, rj = _iota((rows, rows), 0), _iota((rows, rows), 1)
    same_chunk = (ri // C_CHUNK) == (rj // C_CHUNK)
    csum_f = (same_chunk & (rj <= ri)).astype(MXU)
    csum_b = (same_chunk & (rj >= ri)).astype(MXU)

    def block(i, carry):
        base = pl.multiple_of(i * rows, rows)
        cur = raw_ref[0, pl.ds(base, rows), :]
        prev = raw_ref[0, pl.ds(pl.multiple_of(jnp.maximum(base - HALO, 0), HALO), HALO), :]
        nxt = raw_ref[0, pl.ds(pl.multiple_of(jnp.minimum(base + rows, n - HALO), HALO), HALO), :]
        prev = jnp.where(i == 0, 0.0, prev)
        nxt = jnp.where(i == nblk - 1, 0.0, nxt)
        ext = jnp.concatenate([prev, cur, nxt], axis=0)
        w = convw_ref[...]
        acc = cur * w[pad:pad + 1]
        for j in range(C_CONV):
            if j != pad:
                acc = acc + pltpu.roll(ext, (pad - j) % (rows + 2 * HALO), 0)[HALO:HALO + rows] * w[j:j + 1]
        act = _silu(acc)
        qh, kh, vh = act[:, :C_WIDTH], act[:, C_WIDTH:2 * C_WIDTH], act[:, 2 * C_WIDTH:]
        l2 = lambda a: a * lax.rsqrt(_mm_sel(a * a, same_head) + 1e-6)
        dst = pl.ds(pl.multiple_of(off + base, rows), rows)
        q_s[dst, :] = l2(qh) * (C_HEAD_DIM ** -0.5)
        k_s[dst, :] = l2(kh)
        v_s[dst, :] = vh
        gt = gate_ref[0, pl.ds(base, rows), :]
        h = C_HEADS
        beta = _sigmoid(gt)
        bf_s[dst, :] = _expand_heads(beta[:, 0:h])
        bb_s[dst, :] = _expand_heads(beta[:, h:2 * h])
        sp = jnp.maximum(gt + dtb_ref[...], 0.0) + jnp.log1p(jnp.exp(-jnp.abs(gt + dtb_ref[...])))
        g = -jnp.exp(alog_ref[...]) * sp
        gf_s[dst, :] = _expand_heads(_sel_mm(csum_f, g)[:, 2 * h:3 * h])
        gb_s[dst, :] = _expand_heads(_sel_mm(csum_b, g)[:, 3 * h:4 * h])
        return carry

    lax.fori_loop(0, nblk, block, 0)


def _block_diag(a, bmask):
    return jnp.where(bmask, jnp.concatenate([a] * C_HEADS, axis=0), 0.0)


def _gdn_local(chains, q_s, k_s, v_s, bufs):
    c, wd = C_CHUNK, C_WIDTH
    u_buf, w_buf, a_buf, qd_buf, kd_buf, eg_buf = bufs
    ii = _iota((c, wd), 0)
    jj = _iota((c, wd), 1) % c
    eye = jj == ii
    bmask = _head_of_lane((C_HEADS * c, wd), 1) == (_iota((C_HEADS * c, wd), 0) // c)
    bd = lambda a: _block_diag(a, bmask)
    ident = jnp.where(eye, 1.0, 0.0)

    st = []
    for rows, backward, beta_s, gc_s, _, _ in chains:
        q, k, v, beta, gc = q_s[rows, :], k_s[rows, :], v_s[rows, :], beta_s[rows, :], gc_s[rows, :]
        incl = (jj >= ii) if backward else (jj <= ii)
        strict = (jj > ii) if backward else (jj < ii)
        grow = jnp.sum(jnp.where(eye, gc, 0.0), axis=0, keepdims=True)
        decay = jnp.exp(jnp.where(incl, gc - grow, NEG_BIG))
        glast = gc[0:1] if backward else gc[c - 1:c]
        st.append(dict(q=q, k=k, v=v, beta=beta, gc=gc, decay=decay, strict=strict, glast=glast, kb=k * beta,
                       egc=jnp.exp(gc)))
    for s in st:
        qk = _mm_nt(jnp.concatenate([s["kb"], s["q"]], axis=0), bd(s["k"]))
        s["lmat"] = jnp.where(s["strict"], qk[:c] * s["decay"], 0.0)
        s["attn"] = qk[c:] * s["decay"]
    in_blk = (ii // INV_BLOCK) == (jj // INV_BLOCK)
    for s in st:
        dmat = jnp.where(in_blk, s["lmat"], 0.0)
        s["t"] = ident - dmat
        s["pw"] = _mm(dmat, bd(dmat))
    nsq = int(math.log2(INV_BLOCK)) - 1
    for step in range(nsq):
        for s in st:
            b = bd(s["pw"])
            if step < nsq - 1:
                r = _mm(jnp.concatenate([s["t"], s["pw"]], axis=0), b)
                s["t"], s["pw"] = s["t"] + r[:c], r[c:]
            else:
                s["t"] = s["t"] + _mm(s["t"], b)
    size = INV_BLOCK
    while size < c:
        emask = ((ii // (2 * size)) == (jj // (2 * size))) & ((ii // size) != (jj // size))
        for s in st:
            s["y"] = _mm(jnp.where(emask, s["lmat"], 0.0), bd(s["t"]))
        for s in st:
            s["t"] = s["t"] - _mm(s["t"], bd(s["y"]))
        size *= 2
    for s, (_, _, _, _, d, slot) in zip(st, chains):
        dst = slice(slot * c, (slot + 1) * c)
        u_buf[d, dst, :] = _mm(s["t"], bd(s["v"] * s["beta"]))
        w_buf[d, dst, :] = _mm(s["t"], bd(s["kb"] * s["egc"])).astype(MXU)
        a_buf[d, dst, :] = s["attn"].astype(MXU)
        qd_buf[d, dst, :] = (s["q"] * s["egc"]).astype(MXU)
        kd_buf[d, dst, :] = (s["k"] * jnp.exp(s["glast"] - s["gc"])).astype(MXU)
        eg_buf[d, slot * HALO:(slot + 1) * HALO, :] = jnp.broadcast_to(jnp.exp(s["glast"]), (HALO, wd))


def _gdn_scan(steps, states, bufs):
    c, wd = C_CHUNK, C_WIDTH
    u_buf, w_buf, a_buf, qd_buf, kd_buf, eg_buf = bufs
    bmask = _head_of_lane((C_HEADS * c, wd), 1) == (_iota((C_HEADS * c, wd), 0) // c)
    smask = _head_of_lane((wd, wd), 0) == _head_of_lane((wd, wd), 1)
    outs = []
    for slots in steps:
        src = [slice(slot * c, (slot + 1) * c) for slot in slots]
        ws = [_mm(jnp.concatenate([w_buf[d, r, :], qd_buf[d, r, :]], axis=0), states[d]) for d, r in enumerate(src)]
        v_new = [u_buf[d, r, :] - ws[d][:c] for d, r in enumerate(src)]
        outs.append([ws[d][c:] + _mm(a_buf[d, r, :], _block_diag(v_new[d], bmask)) for d, r in enumerate(src)])
        upd = [_mm_tn(kd_buf[d, r, :], v_new[d]) for d, r in enumerate(src)]
        states = [states[d] * eg_buf[d, slot * HALO:slot * HALO + 1, :] + jnp.where(smask, upd[d], 0.0)
                  for d, slot in enumerate(slots)]
    return states, outs


def _gdn_kernel(*refs, n_ctx, n_lat, ctx_out):
    (rawc_ref, gtc_ref, zc_ref, rawl_ref, gtl_ref, zl_ref, convw_ref, alog_ref, dtb_ref, nw_ref) = refs[:10]
    nout = 2 if ctx_out else 1
    out_refs = refs[10:10 + nout]
    (q_s, k_s, v_s, bf_s, bb_s, gf_s, gb_s, of_s, ob_s, st_s, u_buf, w_buf, a_buf, qd_buf, kd_buf, eg_buf) = refs[10 + nout:]
    scr = (q_s, k_s, v_s, bf_s, bb_s, gf_s, gb_s)
    bufs = (u_buf, w_buf, a_buf, qd_buf, kd_buf, eg_buf)
    _gdn_prep(rawc_ref, gtc_ref, n_ctx, 0, convw_ref, alog_ref, dtb_ref, *scr)
    _gdn_prep(rawl_ref, gtl_ref, n_lat, n_ctx, convw_ref, alog_ref, dtb_ref, *scr)
    st_s[...] = jnp.zeros_like(st_s)
    ncb, nlb = n_ctx // GDN_BLOCK, n_lat // GDN_BLOCK
    per = GDN_BLOCK // C_CHUNK

    def block(i, carry):
        bf = pl.multiple_of(i * GDN_BLOCK, GDN_BLOCK)
        bb = pl.multiple_of(jnp.where(i < ncb, ncb - 1 - i, 2 * ncb + nlb - 1 - i) * GDN_BLOCK, GDN_BLOCK)
        rows_f = [pl.ds(bf + r * C_CHUNK, C_CHUNK) for r in range(per)]
        rows_b = [pl.ds(bb + r * C_CHUNK, C_CHUNK) for r in range(per)]
        chains = ([(rows_f[r], False, bf_s, gf_s, 0, r) for r in range(per)]
                  + [(rows_b[r], True, bb_s, gb_s, 1, r) for r in range(per)])
        for lo in range(0, len(chains), GDN_LOCKSTEP):
            _gdn_local(chains[lo:lo + GDN_LOCKSTEP], q_s, k_s, v_s, bufs)
        steps = [(t, per - 1 - t) for t in range(per)]
        states, outs = _gdn_scan(steps, [st_s[0], st_s[1]], bufs)
        st_s[0], st_s[1] = states
        for (tf, tb), (o_f, o_b) in zip(steps, outs):
            of_s[rows_f[tf], :] = o_f
            ob_s[rows_b[tb], :] = o_b
        return carry

    lax.fori_loop(0, ncb + nlb, block, 0)

    same_head = _same_head()

    def finish(z_ref, out_ref, off, n):
        rows = GDN_BLOCK

        def blk(i, carry):
            base = pl.multiple_of(i * rows, rows)
            src = pl.ds(pl.multiple_of(off + base, rows), rows)
            o = of_s[src, :] + ob_s[src, :]
            ms = _mm_sel(o * o, same_head) * (1.0 / C_HEAD_DIM)
            y = o * lax.rsqrt(ms + NORM_EPS) * nw_ref[...]
            out_ref[0, pl.ds(base, rows), :] = (y * _silu(z_ref[0, pl.ds(base, rows), :])).astype(out_ref.dtype)
            return carry

        lax.fori_loop(0, n // rows, blk, 0)

    finish(zl_ref, out_refs[0], n_ctx, n_lat)
    if ctx_out:
        finish(zc_ref, out_refs[1], 0, n_ctx)


def _gdn(rawc, gtc, zc, rawl, gtl, zl, convw, alog, dtb, nw, *, ctx_out):
    bsz, n_ctx, _ = rawc.shape
    n_lat = rawl.shape[1]
    nt = n_ctx + n_lat
    seq = lambda a: pl.BlockSpec((1,) + a.shape[1:], lambda b: (b, 0, 0))
    const = lambda a: pl.BlockSpec(a.shape, lambda b: (0,) * a.ndim)
    out_shape = [jax.ShapeDtypeStruct((bsz, n_lat, C_WIDTH), MXU)]
    if ctx_out:
        out_shape.append(jax.ShapeDtypeStruct((bsz, n_ctx, C_WIDTH), MXU))
    big = lambda: pltpu.VMEM((nt, C_WIDTH), F32)
    loc = lambda dt: pltpu.VMEM((2, GDN_BLOCK, C_WIDTH), dt)
    per = GDN_BLOCK // C_CHUNK
    res = pl.pallas_call(
        functools.partial(_gdn_kernel, n_ctx=n_ctx, n_lat=n_lat, ctx_out=ctx_out),
        grid=(bsz,),
        in_specs=[seq(rawc), seq(gtc), seq(zc), seq(rawl), seq(gtl), seq(zl),
                  const(convw), const(alog), const(dtb), const(nw)],
        out_specs=[seq(s) for s in out_shape],
        out_shape=out_shape,
        scratch_shapes=[big() for _ in range(9)] + [pltpu.VMEM((2, C_WIDTH, C_WIDTH), F32)]
        + [loc(F32), loc(MXU), loc(MXU), loc(MXU), loc(MXU), pltpu.VMEM((2, per * HALO, C_WIDTH), F32)],
        compiler_params=pltpu.CompilerParams(dimension_semantics=("arbitrary",), vmem_limit_bytes=VMEM_LIMIT),
        name="gated_deltanet_bidir",
    )(rawc, gtc, zc, rawl, gtl, zl, convw, alog, dtb, nw)
    return res


def _out_kernel(x_ref, mod_ref, ya_ref, yb_ref, yc_ref, w_ref, fw_ref, o_ref, *, final):
    d = D_MODEL
    y = (jnp.dot(ya_ref[0], w_ref[0:A_WIDTH], preferred_element_type=F32)
         + jnp.dot(yb_ref[0], w_ref[A_WIDTH:A_WIDTH + B_WIDTH], preferred_element_type=F32)
         + jnp.dot(yc_ref[0], w_ref[A_WIDTH + B_WIDTH:], preferred_element_type=F32))
    x = x_ref[0] + mod_ref[0][:, 2 * d:] * y
    if final:
        x = x * lax.rsqrt(jnp.mean(x * x, axis=-1, keepdims=True) + NORM_EPS) * fw_ref[...]
    o_ref[0] = x


def _out(x, mod, ya, yb, yc, w_out, fw, *, final, tile):
    bsz, n, d = x.shape
    mod_map = (lambda b, i: (b, 0, 0)) if mod.shape[0] == bsz else (lambda b, i: (0, 0, 0))
    tok = lambda w: pl.BlockSpec((1, tile, w), lambda b, i: (b, i, 0))
    const = lambda a: pl.BlockSpec(a.shape, lambda b, i: (0,) * a.ndim)
    return pl.pallas_call(
        functools.partial(_out_kernel, final=final),
        grid=(bsz, n // tile),
        in_specs=[tok(d), pl.BlockSpec((1, 1, 3 * d), mod_map), tok(A_WIDTH), tok(B_WIDTH), tok(C_WIDTH),
                  const(w_out), const(fw)],
        out_specs=tok(d),
        out_shape=jax.ShapeDtypeStruct((bsz, n, d), F32),
        compiler_params=pltpu.CompilerParams(dimension_semantics=("arbitrary", "arbitrary"),
                                             vmem_limit_bytes=VMEM_LIMIT),
        name="outproj_residual_final" if final else "outproj_residual",
    )(x, mod, ya, yb, yc, w_out, fw)


def _rope_tables(n):
    pos = jnp.arange(n)
    half = B_HEAD_DIM // 2
    inv_freq = ROPE_THETA ** (-jnp.arange(0, half, 2, dtype=F32) / half)
    lane = np.arange(B_WIDTH)
    dd = lane % B_HEAD_DIM
    use_col = (dd // half) == 1
    second = ((dd % half) // (half // 2)) == 1
    fidx = dd % (half // 2)
    p = jnp.where(use_col[None, :], (pos % GRID_W)[:, None], (pos // GRID_W)[:, None]).astype(F32)
    ang = p * inv_freq[fidx][None, :]
    cos, sin = jnp.cos(ang), jnp.sin(ang)
    return cos, jnp.where(second[None, :], 0.0, -sin), jnp.where(second[None, :], sin, 0.0)


def kernel(x, c, ctx, c_ctx, w_ada, b_ada, norm_w, w_in, w_out, a_ln_w, a_ws, a_bs, b_lam, b_subln_w,
           c_conv_w, c_a_log, c_dt_bias, c_norm_w, final_norm_w):
    bsz, n, d = x.shape
    n_ctx = ctx.shape[1]
    depth = w_ada.shape[0]
    assert d == D_MODEL and n % GDN_BLOCK == 0 and n_ctx % GDN_BLOCK == 0 and n % GRID_W == 0
    tile = 256
    tile_ctx = min(tile, n_ctx)

    rows = bsz + 1
    rows_pad = -(-rows // 8) * 8
    c_rows = jnp.concatenate([c, c_ctx[None, :], jnp.zeros((rows_pad - rows, d), F32)], axis=0)
    mods = _ada(c_rows, w_ada, b_ada)

    tabs = _rope_tables(n)
    tabs_ctx = tuple(t[:tile_ctx] for t in tabs)
    gate_pad = jnp.zeros((depth, d, GATE_PAD - N_GATE), w_in.dtype)
    w_in_p = jnp.concatenate([w_in, gate_pad], axis=-1).astype(MXU)
    w_out_b = w_out.astype(MXU)
    aws = a_ws.reshape(depth, A_GROUPS * A_CHUNK, A_CHUNK).astype(MXU)
    abias = jnp.repeat(jnp.swapaxes(a_bs, 1, 2), A_WIDTH // A_GROUPS, axis=2)
    row = lambda a: a.reshape(1, -1)

    xc = ctx
    for i in range(depth):
        lam_init = 0.8 - 0.6 * math.exp(-0.3 * i)
        ctx_out = i < depth - 1
        mod_l = mods[i, :bsz][:, None, :]
        mod_c = mods[i, bsz:bsz + 1][:, None, :]
        common = (row(norm_w[i]), w_in_p[i])
        a_par = (row(a_ln_w[i]), aws[i], abias[i])
        ya, qb, kb, vb, zb, craw, zc, gt = _proj(x, mod_l, *common, tabs, *a_par, rope=True, tile=tile)
        ya_c, qb_c, kb_c, vb_c, zb_c, craw_c, zc_c, gt_c = _proj(xc, mod_c, *common, tabs_ctx, *a_par,
                                                                 rope=False, tile=tile_ctx)
        lam_p, sw = b_lam[i], row(b_subln_w[i])
        yb = _attn(qb, zb, lam_p, sw, [kb_c, kb], [vb_c, vb], lam_init=lam_init, tile=tile)
        conv_w = c_conv_w[i]
        gate_row = lambda p: jnp.concatenate([jnp.zeros((2 * C_HEADS,), F32), p.reshape(-1),
                                              jnp.zeros((GATE_PAD - N_GATE,), F32)])[None, :]
        gdn_par = (conv_w, gate_row(c_a_log[i]), gate_row(c_dt_bias[i]), jnp.tile(c_norm_w[i], C_HEADS)[None, :])
        yc_all = _gdn(craw_c, gt_c, zc_c, craw, gt, zc, *gdn_par, ctx_out=ctx_out)
        final = i == depth - 1
        x = _out(x, mod_l, ya, yb, yc_all[0], w_out_b[i], row(final_norm_w), final=final, tile=tile)
        if ctx_out:
            yb_c = _attn(qb_c, zb_c, lam_p, sw, [kb_c], [vb_c], lam_init=lam_init, tile=tile_ctx)
            xc = _out(xc, mod_c, ya_c, yb_c, yc_all[1], w_out_b[i], row(final_norm_w), final=False, tile=tile_ctx)
    return x
```

```python
import functools
import math

import numpy as np
import jax
import jax.numpy as jnp
from jax import lax
from jax.experimental import pallas as pl
from jax.experimental.pallas import tpu as pltpu

F32 = jnp.float32
MXU = jnp.bfloat16

D_MODEL = 1024
NORM_EPS = 1e-6
GRID_W = 64
A_WIDTH, A_GROUPS, A_CHUNK, A_LN_EPS = 256, 4, 128, 1e-5
B_HEADS, B_HEAD_DIM, B_WIDTH = 4, 64, 512
ROPE_THETA = 10000.0
C_HEADS, C_HEAD_DIM, C_WIDTH, C_CONV, C_CHUNK = 4, 64, 256, 5, 64
N_GATE = 4 * C_HEADS
GATE_PAD = 128
OFF_A, OFF_B, OFF_C, OFF_G = 0, 3 * A_WIDTH, 3 * A_WIDTH + 4 * B_WIDTH, 3 * A_WIDTH + 4 * B_WIDTH + 4 * C_WIDTH
D_IN_PAD = OFF_G + GATE_PAD
NEG_BIG = -1e30
LOG2E = math.log2(math.e)
ATT_KCHUNK = 256
ATT_QTILE = 256
PROJ_TILE = 512
VMEM_LIMIT = 56 * 1024 * 1024
GDN_BLOCK = 256
HALO = 8
INV_BLOCK = 16


def _mm(a, b):
    return jnp.dot(a.astype(MXU), b.astype(MXU), preferred_element_type=F32)


def _mm_nt(a, b):
    return lax.dot_general(a.astype(MXU), b.astype(MXU), (((1,), (1,)), ((), ())), preferred_element_type=F32)


def _mm_tn(a, b):
    return lax.dot_general(a.astype(MXU), b.astype(MXU), (((0,), (0,)), ((), ())), preferred_element_type=F32)


def _split(a):
    hi = a.astype(MXU)
    return hi, (a - hi.astype(F32)).astype(MXU)


def _mm3(a, b):
    ah, al = _split(a)
    bh, bl = _split(b)
    d = lambda x, y: jnp.dot(x, y, preferred_element_type=F32)
    return d(ah, bh) + (d(ah, bl) + d(al, bh))


def _sigmoid(x):
    return 1.0 / (1.0 + jnp.exp(-x))


def _silu(x):
    return x * _sigmoid(x)


def _iota(shape, dim):
    return lax.broadcasted_iota(jnp.int32, shape, dim)


def _ada_kernel(c_ref, w_ref, b_ref, o_ref):
    o_ref[0] = _mm3(_silu(c_ref[...]), w_ref[0]) + b_ref[0]


def _ada(c_rows, w_ada, b_ada):
    depth, d, d3 = w_ada.shape
    r = c_rows.shape[0]
    tn = 512
    return pl.pallas_call(
        _ada_kernel,
        grid=(depth, d3 // tn),
        in_specs=[pl.BlockSpec((r, d), lambda l, j: (0, 0)),
                  pl.BlockSpec((1, d, tn), lambda l, j: (l, 0, j)),
                  pl.BlockSpec((1, 1, tn), lambda l, j: (l, 0, j))],
        out_specs=pl.BlockSpec((1, r, tn), lambda l, j: (l, 0, j)),
        out_shape=jax.ShapeDtypeStruct((depth, r, d3), F32),
        name="ada_modulation",
    )(c_rows, w_ada, b_ada.reshape(depth, 1, d3))


def _proj_kernel(x_ref, mod_ref, nw_ref, w_ref, cos_ref, sinn_ref, sinp_ref, lnw_ref, aws_ref, abias_ref,
                 ya_ref, q_ref, k_ref, v_ref, zb_ref, craw_ref, zc_ref, g_ref, *, rope):
    d = D_MODEL
    x = x_ref[0]
    mod = mod_ref[0]
    y = x * lax.rsqrt(jnp.mean(x * x, axis=-1, keepdims=True) + NORM_EPS) * nw_ref[...]
    h = (y * (1.0 + mod[:, d:2 * d]) + mod[:, :d]).astype(MXU)
    t = x.shape[0]

    pa = jnp.dot(h, w_ref[:, OFF_A:OFF_B], preferred_element_type=F32)
    u, v, z = pa[:, :A_WIDTH], pa[:, A_WIDTH:2 * A_WIDTH], pa[:, 2 * A_WIDTH:]
    vc = v - jnp.mean(v, axis=-1, keepdims=True)
    vn = (vc * lax.rsqrt(jnp.mean(vc * vc, axis=-1, keepdims=True) + A_LN_EPS) * lnw_ref[...]).astype(MXU)
    grp = _iota((A_CHUNK, A_WIDTH), 1) // (A_WIDTH // A_GROUPS)
    mixed = []
    for c in range(t // A_CHUNK):
        r = jnp.dot(aws_ref[...], vn[c * A_CHUNK:(c + 1) * A_CHUNK], preferred_element_type=F32)
        s = abias_ref[...]
        for g in range(A_GROUPS):
            s = s + jnp.where(grp == g, r[g * A_CHUNK:(g + 1) * A_CHUNK], 0.0)
        mixed.append(s)
    s_all = jnp.concatenate(mixed, axis=0) if len(mixed) > 1 else mixed[0]
    ya_ref[0] = (u * s_all * _silu(z)).astype(ya_ref.dtype)

    pb = jnp.dot(h, w_ref[:, OFF_B:OFF_C], preferred_element_type=F32)
    q, k = pb[:, :B_WIDTH], pb[:, B_WIDTH:2 * B_WIDTH]
    if rope:
        cos, sinn, sinp = cos_ref[...], sinn_ref[...], sinp_ref[...]
        half = B_HEAD_DIM // 4
        rot = lambda a: (a * cos + pltpu.roll(a, B_WIDTH - half, 1) * sinn + pltpu.roll(a, half, 1) * sinp)
        q, k = rot(q), rot(k)
    q_ref[0] = (q * (B_HEAD_DIM ** -0.5 * LOG2E)).astype(q_ref.dtype)
    k_ref[0] = k.astype(k_ref.dtype)
    v_ref[0] = pb[:, 2 * B_WIDTH:3 * B_WIDTH].T.astype(v_ref.dtype)
    zb_ref[0] = pb[:, 3 * B_WIDTH:].astype(zb_ref.dtype)

    pc = jnp.dot(h, w_ref[:, OFF_C:OFF_G], preferred_element_type=F32)
    craw_ref[0] = pc[:, :3 * C_WIDTH]
    zc_ref[0] = pc[:, 3 * C_WIDTH:]
    pg = jnp.dot(h, w_ref[:, OFF_G:], preferred_element_type=F32)
    g_ref[0] = pg


def _proj(x, mod, nw, w_in, tabs, lnw, aws, abias, *, rope, tile):
    bsz, n, d = x.shape
    cos, sinn, sinp = tabs
    mod_map = (lambda i, b: (b, 0, 0)) if mod.shape[0] == bsz else (lambda i, b: (0, 0, 0))
    tok = lambda w: pl.BlockSpec((1, tile, w), lambda i, b: (b, i, 0))
    const = lambda a: pl.BlockSpec(a.shape, lambda i, b: (0,) * a.ndim)
    tab = pl.BlockSpec((tile, B_WIDTH), lambda i, b: (i, 0))
    outs = [(A_WIDTH, MXU), (B_WIDTH, MXU), (B_WIDTH, MXU), (B_WIDTH, MXU), (B_WIDTH, F32),
            (3 * C_WIDTH, F32), (C_WIDTH, F32), (GATE_PAD, F32)]
    out_specs = [tok(w) for w, _ in outs]
    out_shape = [jax.ShapeDtypeStruct((bsz, n, w), dt) for w, dt in outs]
    out_specs[3] = pl.BlockSpec((1, B_WIDTH, tile), lambda i, b: (b, 0, i))
    out_shape[3] = jax.ShapeDtypeStruct((bsz, B_WIDTH, n), MXU)
    return pl.pallas_call(
        functools.partial(_proj_kernel, rope=rope),
        grid=(n // tile, bsz),
        in_specs=[tok(d), pl.BlockSpec((1, 1, 3 * d), mod_map), const(nw), const(w_in), tab, tab, tab,
                  const(lnw), const(aws), const(abias)],
        out_specs=out_specs,
        out_shape=out_shape,
        compiler_params=pltpu.CompilerParams(dimension_semantics=("arbitrary", "arbitrary"),
                                             vmem_limit_bytes=VMEM_LIMIT),
        name="norm_inproj_rope" if rope else "norm_inproj_ctx",
    )(x, mod, nw, w_in, cos, sinn, sinp, lnw, aws, abias)


def _interleave(gens):
    gens = list(gens)
    while gens:
        for g in list(gens):
            if next(g, StopIteration) is StopIteration:
                gens.remove(g)


def _attn_kernel(*refs, seg_lens, lam_init):
    nseg = len(seg_lens)
    q_ref, z_ref, lam_ref, sw_ref = refs[:4]
    k_refs, v_refs = refs[4:4 + nseg], refs[4 + nseg:4 + 2 * nseg]
    o_ref, s_buf, e_buf = refs[4 + 2 * nseg:]
    lf = lam_ref[...]
    lam = (jnp.exp(jnp.sum(lf[0:1] * lf[1:2], axis=-1, keepdims=True))
           - jnp.exp(jnp.sum(lf[2:3] * lf[3:4], axis=-1, keepdims=True)) + lam_init)
    q = q_ref[0]
    pair = 2 * B_HEAD_DIM
    first = _iota((1, pair), 1) < B_HEAD_DIM
    zero = jnp.zeros((), q.dtype)
    chunks, row = [], 0
    for seg, n in enumerate(seg_lens):
        for c0 in range(0, n, ATT_KCHUNK):
            chunks.append((seg, c0, row))
            row += ATT_KCHUNK
    kc = ATT_KCHUNK
    maxes, heads = {}, {}

    dens = {}

    def scores(i):
        hh, j = divmod(i, 2)
        lanes = slice(hh * pair, (hh + 1) * pair)
        qm = jnp.where(first if j == 0 else jnp.logical_not(first), q[:, lanes], zero)
        m = None
        for seg, c0, r0 in chunks:
            s = _mm_nt(k_refs[seg][0, c0:c0 + kc, lanes], qm)
            s_buf[hh % 2, j, r0:r0 + kc, :] = s
            cm = jnp.max(s, axis=0, keepdims=True)
            m = cm if m is None else jnp.maximum(m, cm)
            yield
        maxes[i] = m

    def exps(i):
        hh, j = divmod(i, 2)
        den = None
        for _, _, r0 in chunks:
            e = jnp.exp2(s_buf[hh % 2, j, r0:r0 + kc, :] - maxes[i])
            e_buf[hh % 2, j, r0:r0 + kc, :] = e.astype(MXU)
            cs = jnp.sum(e, axis=0, keepdims=True)
            den = cs if den is None else den + cs
            yield
        dens[i] = den

    def values(hh):
        lanes = slice(hh * pair, (hh + 1) * pair)
        d0, d1 = dens[2 * hh], dens[2 * hh + 1]
        coef = (lam * d0 * (1.0 / d1)).astype(MXU)
        pv = None
        for seg, c0, r0 in chunks:
            p = e_buf[hh % 2, 0, r0:r0 + kc, :] - coef * e_buf[hh % 2, 1, r0:r0 + kc, :]
            t = _mm(v_refs[seg][0, lanes, c0:c0 + kc], p)
            pv = t if pv is None else pv + t
            yield
        acc = (pv * (1.0 / d0)).T
        o = acc * lax.rsqrt(jnp.mean(acc * acc, axis=-1, keepdims=True) + NORM_EPS) * sw_ref[...]
        heads[hh] = o * (1.0 - lam_init)

    nh = 2 * B_HEADS
    for t in range(nh + 2):
        stage = []
        if t < nh:
            stage.append(scores(t))
        if 1 <= t <= nh:
            stage.append(exps(t - 1))
        if t >= 3 and t % 2 == 1:
            stage.append(values((t - 3) // 2))
        _interleave(stage)
    o_all = jnp.concatenate([heads[hh] for hh in range(B_HEADS)], axis=-1)
    o_ref[0] = (o_all * _silu(z_ref[0])).astype(o_ref.dtype)


def _attn(q, z, lam_p, subln_w, ks, vs, *, lam_init, tile):
    bsz, n, w = q.shape
    seg_lens = tuple(a.shape[1] for a in ks)
    tok = pl.BlockSpec((1, tile, w), lambda b, i: (b, i, 0))
    const = lambda a: pl.BlockSpec(a.shape, lambda b, i: (0,) * a.ndim)
    full = lambda a: pl.BlockSpec((1,) + a.shape[1:], lambda b, i: (b, 0, 0))
    return pl.pallas_call(
        functools.partial(_attn_kernel, seg_lens=seg_lens, lam_init=lam_init),
        grid=(bsz, n // tile),
        in_specs=[tok, tok, const(lam_p), const(subln_w)] + [full(a) for a in ks] + [full(a) for a in vs],
        out_specs=tok,
        out_shape=jax.ShapeDtypeStruct((bsz, n, w), MXU),
        scratch_shapes=[pltpu.VMEM((2, 2, sum(seg_lens), tile), F32), pltpu.VMEM((2, 2, sum(seg_lens), tile), MXU)],
        compiler_params=pltpu.CompilerParams(dimension_semantics=("arbitrary", "arbitrary"),
                                             vmem_limit_bytes=VMEM_LIMIT),
        name="diff_attention_%dseg" % len(ks),
    )(q, z, lam_p, subln_w, *ks, *vs)


def _head_of_lane(shape, dim):
    return _iota(shape, dim) // C_HEAD_DIM


def _expand_heads(cols):
    r = cols.shape[0]
    hl = _head_of_lane((r, C_WIDTH), 1)
    out = jnp.broadcast_to(cols[:, C_HEADS - 1:C_HEADS], (r, C_WIDTH))
    for h in range(C_HEADS - 2, -1, -1):
        out = jnp.where(hl == h, jnp.broadcast_to(cols[:, h:h + 1], (r, C_WIDTH)), out)
    return out


def _split3(a):
    a1 = a.astype(MXU)
    r1 = a - a1.astype(F32)
    a2 = r1.astype(MXU)
    return a1, a2, (r1 - a2.astype(F32)).astype(MXU)


def _sel_mm(sel, a):
    d = lambda x: jnp.dot(sel, x, preferred_element_type=F32)
    a1, a2, a3 = _split3(a)
    return d(a1) + (d(a2) + d(a3))


def _mm_sel(a, sel):
    d = lambda x: jnp.dot(x, sel, preferred_element_type=F32)
    a1, a2, a3 = _split3(a)
    return d(a1) + (d(a2) + d(a3))


def _same_head():
    return (_head_of_lane((C_WIDTH, C_WIDTH), 0) == _head_of_lane((C_WIDTH, C_WIDTH), 1)).astype(MXU)


def _gdn_prep(raw_ref, gate_ref, n, off, convw_ref, alog_ref, dtb_ref, q_s, k_s, v_s, bf_s, bb_s, gf_s, gb_s):
    rows = GDN_BLOCK
    nblk = n // rows
    same_head = _same_head()
    pad = C_CONV // 2
    ri, rj = _iota((rows, rows), 0), _iota((rows, rows), 1)
    same_chunk = (ri // C_CHUNK) == (rj // C_CHUNK)
    csum_f = (same_chunk & (rj <= ri)).astype(MXU)
    csum_b = (same_chunk & (rj >= ri)).astype(MXU)

    def block(i, carry):
        base = pl.multiple_of(i * rows, rows)
        cur = raw_ref[0, pl.ds(base, rows), :]
        prev = raw_ref[0, pl.ds(pl.multiple_of(jnp.maximum(base - HALO, 0), HALO), HALO), :]
        nxt = raw_ref[0, pl.ds(pl.multiple_of(jnp.minimum(base + rows, n - HALO), HALO), HALO), :]
        prev = jnp.where(i == 0, 0.0, prev)
        nxt = jnp.where(i == nblk - 1, 0.0, nxt)
        ext = jnp.concatenate([prev, cur, nxt], axis=0)
        w = convw_ref[...]
        acc = cur * w[pad:pad + 1]
        for j in range(C_CONV):
            if j != pad:
                acc = acc + pltpu.roll(ext, (pad - j) % (rows + 2 * HALO), 0)[HALO:HALO + rows] * w[j:j + 1]
        act = _silu(acc)
        qh, kh, vh = act[:, :C_WIDTH], act[:, C_WIDTH:2 * C_WIDTH], act[:, 2 * C_WIDTH:]
        l2 = lambda a: a * lax.rsqrt(_mm_sel(a * a, same_head) + 1e-6)
        dst = pl.ds(pl.multiple_of(off + base, rows), rows)
        q_s[dst, :] = l2(qh) * (C_HEAD_DIM ** -0.5)
        k_s[dst, :] = l2(kh)
        v_s[dst, :] = vh
        gt = gate_ref[0, pl.ds(base, rows), :]
        h = C_HEADS
        beta = _sigmoid(gt)
        bf_s[dst, :] = _expand_heads(beta[:, 0:h])
        bb_s[dst, :] = _expand_heads(beta[:, h:2 * h])
        sp = jnp.maximum(gt + dtb_ref[...], 0.0) + jnp.log1p(jnp.exp(-jnp.abs(gt + dtb_ref[...])))
        g = -jnp.exp(alog_ref[...]) * sp
        gf_s[dst, :] = _expand_heads(_sel_mm(csum_f, g)[:, 2 * h:3 * h])
        gb_s[dst, :] = _expand_heads(_sel_mm(csum_b, g)[:, 3 * h:4 * h])
        return carry

    lax.fori_loop(0, nblk, block, 0)


def _block_diag(a, bmask):
    return jnp.where(bmask, jnp.concatenate([a] * C_HEADS, axis=0), 0.0)


def _gdn_local(chains, q_s, k_s, v_s, bufs, buf):
    c, wd = C_CHUNK, C_WIDTH
    u_buf, w_buf, a_buf, qd_buf, kd_buf, eg_buf = bufs
    ii = _iota((c, wd), 0)
    jj = _iota((c, wd), 1) % c
    eye = jj == ii
    bmask = _head_of_lane((C_HEADS * c, wd), 1) == (_iota((C_HEADS * c, wd), 0) // c)
    bd = lambda a: _block_diag(a, bmask)
    ident = jnp.where(eye, 1.0, 0.0)

    st = []
    for rows, backward, beta_s, gc_s, _, _ in chains:
        q, k, v, beta, gc = q_s[rows, :], k_s[rows, :], v_s[rows, :], beta_s[rows, :], gc_s[rows, :]
        incl = (jj >= ii) if backward else (jj <= ii)
        strict = (jj > ii) if backward else (jj < ii)
        grow = jnp.sum(jnp.where(eye, gc, 0.0), axis=0, keepdims=True)
        decay = jnp.exp(jnp.where(incl, gc - grow, NEG_BIG))
        glast = gc[0:1] if backward else gc[c - 1:c]
        st.append(dict(q=q, k=k, v=v, beta=beta, gc=gc, decay=decay, strict=strict, glast=glast, kb=k * beta,
                       egc=jnp.exp(gc)))
    yield
    for s in st:
        qk = _mm_nt(jnp.concatenate([s["kb"], s["q"]], axis=0), bd(s["k"]))
        s["lmat"] = jnp.where(s["strict"], qk[:c] * s["decay"], 0.0)
        s["attn"] = qk[c:] * s["decay"]
    yield
    in_blk = (ii // INV_BLOCK) == (jj // INV_BLOCK)
    for s in st:
        dmat = jnp.where(in_blk, s["lmat"], 0.0)
        s["t"] = ident - dmat
        s["pw"] = _mm(dmat, bd(dmat))
    yield
    nsq = int(math.log2(INV_BLOCK)) - 1
    for step in range(nsq):
        for s in st:
            b = bd(s["pw"])
            if step < nsq - 1:
                r = _mm(jnp.concatenate([s["t"], s["pw"]], axis=0), b)
                s["t"], s["pw"] = s["t"] + r[:c], r[c:]
            else:
                s["t"] = s["t"] + _mm(s["t"], b)
        yield
    size = INV_BLOCK
    while size < c:
        emask = ((ii // (2 * size)) == (jj // (2 * size))) & ((ii // size) != (jj // size))
        for s in st:
            s["y"] = _mm(jnp.where(emask, s["lmat"], 0.0), bd(s["t"]))
        yield
        for s in st:
            s["t"] = s["t"] - _mm(s["t"], bd(s["y"]))
        yield
        size *= 2
    for s, (_, _, _, _, d, slot) in zip(st, chains):
        dst = slice(slot * c, (slot + 1) * c)
        u_buf[buf, d, dst, :] = _mm(s["t"], bd(s["v"] * s["beta"]))
        a_buf[buf, d, dst, :] = s["attn"].astype(MXU)
        qd_buf[buf, d, dst, :] = (s["q"] * s["egc"]).astype(MXU)
    yield
    for s, (_, _, _, _, d, slot) in zip(st, chains):
        dst = slice(slot * c, (slot + 1) * c)
        w_buf[buf, d, dst, :] = _mm(s["t"], bd(s["kb"] * s["egc"])).astype(MXU)
        kd_buf[buf, d, dst, :] = (s["k"] * jnp.exp(s["glast"] - s["gc"])).astype(MXU)
        eg_buf[buf, d, slot * HALO:(slot + 1) * HALO, :] = jnp.broadcast_to(jnp.exp(s["glast"]), (HALO, wd))


def _gdn_scan(steps, rows, st_s, o_refs, bufs, buf):
    c, wd = C_CHUNK, C_WIDTH
    u_buf, w_buf, a_buf, qd_buf, kd_buf, eg_buf = bufs
    bmask = _head_of_lane((C_HEADS * c, wd), 1) == (_iota((C_HEADS * c, wd), 0) // c)
    smask = _head_of_lane((wd, wd), 0) == _head_of_lane((wd, wd), 1)
    states = [st_s[0], st_s[1]]
    for slots in steps:
        src = [slice(slot * c, (slot + 1) * c) for slot in slots]
        ws = [_mm(jnp.concatenate([w_buf[buf, d, r, :], qd_buf[buf, d, r, :]], axis=0), states[d])
              for d, r in enumerate(src)]
        yield
        v_new = [u_buf[buf, d, r, :] - ws[d][:c] for d, r in enumerate(src)]
        upd = [_mm_tn(kd_buf[buf, d, r, :], v_new[d]) for d, r in enumerate(src)]
        for d, (r, slot) in enumerate(zip(src, slots)):
            o_refs[d][rows[d][slot], :] = ws[d][c:] + _mm(a_buf[buf, d, r, :], _block_diag(v_new[d], bmask))
        yield
        states = [states[d] * eg_buf[buf, d, slot * HALO:slot * HALO + 1, :] + jnp.where(smask, upd[d], 0.0)
                  for d, slot in enumerate(slots)]
    st_s[0], st_s[1] = states


def _gdn_kernel(*refs, n_ctx, n_lat, ctx_out):
    (rawc_ref, gtc_ref, zc_ref, rawl_ref, gtl_ref, zl_ref, convw_ref, alog_ref, dtb_ref, nw_ref) = refs[:10]
    nout = 2 if ctx_out else 1
    out_refs = refs[10:10 + nout]
    (q_s, k_s, v_s, bf_s, bb_s, gf_s, gb_s, of_s, ob_s, st_s, u_buf, w_buf, a_buf, qd_buf, kd_buf, eg_buf) = refs[10 + nout:]
    scr = (q_s, k_s, v_s, bf_s, bb_s, gf_s, gb_s)
    bufs = (u_buf, w_buf, a_buf, qd_buf, kd_buf, eg_buf)
    _gdn_prep(rawc_ref, gtc_ref, n_ctx, 0, convw_ref, alog_ref, dtb_ref, *scr)
    _gdn_prep(rawl_ref, gtl_ref, n_lat, n_ctx, convw_ref, alog_ref, dtb_ref, *scr)
    st_s[...] = jnp.zeros_like(st_s)
    ncb, nlb = n_ctx // GDN_BLOCK, n_lat // GDN_BLOCK
    per = GDN_BLOCK // C_CHUNK

    nblk = ncb + nlb
    steps = [(t, per - 1 - t) for t in range(per)]

    def block_rows(i):
        i = jnp.asarray(i, jnp.int32)
        bf = pl.multiple_of(i * GDN_BLOCK, GDN_BLOCK)
        bb = pl.multiple_of(jnp.where(i < ncb, ncb - 1 - i, 2 * ncb + nlb - 1 - i) * GDN_BLOCK, GDN_BLOCK)
        return ([pl.ds(bf + r * C_CHUNK, C_CHUNK) for r in range(per)],
                [pl.ds(bb + r * C_CHUNK, C_CHUNK) for r in range(per)])

    def local(i):
        rows_f, rows_b = block_rows(i)
        chains = ([(rows_f[r], False, bf_s, gf_s, 0, r) for r in range(per)]
                  + [(rows_b[r], True, bb_s, gb_s, 1, r) for r in range(per)])
        return _gdn_local(chains, q_s, k_s, v_s, bufs, i % 2)

    def scan(i):
        return _gdn_scan(steps, block_rows(i), st_s, (of_s, ob_s), bufs, i % 2)

    _interleave([local(0)])

    def block(i, carry):
        _interleave([local(i + 1), scan(i)])
        return carry

    lax.fori_loop(0, nblk - 1, block, 0)
    _interleave([scan(nblk - 1)])

    same_head = _same_head()

    def finish(z_ref, out_ref, off, n):
        rows = GDN_BLOCK

        def blk(i, carry):
            base = pl.multiple_of(i * rows, rows)
            src = pl.ds(pl.multiple_of(off + base, rows), rows)
            o = of_s[src, :] + ob_s[src, :]
            ms = _mm_sel(o * o, same_head) * (1.0 / C_HEAD_DIM)
            y = o * lax.rsqrt(ms + NORM_EPS) * nw_ref[...]
            out_ref[0, pl.ds(base, rows), :] = (y * _silu(z_ref[0, pl.ds(base, rows), :])).astype(out_ref.dtype)
            return carry

        lax.fori_loop(0, n // rows, blk, 0)

    finish(zl_ref, out_refs[0], n_ctx, n_lat)
    if ctx_out:
        finish(zc_ref, out_refs[1], 0, n_ctx)


def _gdn(rawc, gtc, zc, rawl, gtl, zl, convw, alog, dtb, nw, *, ctx_out):
    bsz, n_ctx, _ = rawc.shape
    n_lat = rawl.shape[1]
    nt = n_ctx + n_lat
    seq = lambda a: pl.BlockSpec((1,) + a.shape[1:], lambda b: (b, 0, 0))
    const = lambda a: pl.BlockSpec(a.shape, lambda b: (0,) * a.ndim)
    out_shape = [jax.ShapeDtypeStruct((bsz, n_lat, C_WIDTH), MXU)]
    if ctx_out:
        out_shape.append(jax.ShapeDtypeStruct((bsz, n_ctx, C_WIDTH), MXU))
    big = lambda: pltpu.VMEM((nt, C_WIDTH), F32)
    loc = lambda dt: pltpu.VMEM((2, 2, GDN_BLOCK, C_WIDTH), dt)
    per = GDN_BLOCK // C_CHUNK
    res = pl.pallas_call(
        functools.partial(_gdn_kernel, n_ctx=n_ctx, n_lat=n_lat, ctx_out=ctx_out),
        grid=(bsz,),
        in_specs=[seq(rawc), seq(gtc), seq(zc), seq(rawl), seq(gtl), seq(zl),
                  const(convw), const(alog), const(dtb), const(nw)],
        out_specs=[seq(s) for s in out_shape],
        out_shape=out_shape,
        scratch_shapes=[big() for _ in range(9)] + [pltpu.VMEM((2, C_WIDTH, C_WIDTH), F32)]
        + [loc(F32), loc(MXU), loc(MXU), loc(MXU), loc(MXU), pltpu.VMEM((2, 2, per * HALO, C_WIDTH), F32)],
        compiler_params=pltpu.CompilerParams(dimension_semantics=("arbitrary",), vmem_limit_bytes=VMEM_LIMIT),
        name="gated_deltanet_bidir",
    )(rawc, gtc, zc, rawl, gtl, zl, convw, alog, dtb, nw)
    return res


def _out_kernel(x_ref, mod_ref, ya_ref, yb_ref, yc_ref, w_ref, fw_ref, o_ref, *, final):
    d = D_MODEL
    y = (jnp.dot(ya_ref[0], w_ref[0:A_WIDTH], preferred_element_type=F32)
         + jnp.dot(yb_ref[0], w_ref[A_WIDTH:A_WIDTH + B_WIDTH], preferred_element_type=F32)
         + jnp.dot(yc_ref[0], w_ref[A_WIDTH + B_WIDTH:], preferred_element_type=F32))
    x = x_ref[0] + mod_ref[0][:, 2 * d:] * y
    if final:
        x = x * lax.rsqrt(jnp.mean(x * x, axis=-1, keepdims=True) + NORM_EPS) * fw_ref[...]
    o_ref[0] = x


def _out(x, mod, ya, yb, yc, w_out, fw, *, final, tile):
    bsz, n, d = x.shape
    mod_map = (lambda b, i: (b, 0, 0)) if mod.shape[0] == bsz else (lambda b, i: (0, 0, 0))
    tok = lambda w: pl.BlockSpec((1, tile, w), lambda b, i: (b, i, 0))
    const = lambda a: pl.BlockSpec(a.shape, lambda b, i: (0,) * a.ndim)
    return pl.pallas_call(
        functools.partial(_out_kernel, final=final),
        grid=(bsz, n // tile),
        in_specs=[tok(d), pl.BlockSpec((1, 1, 3 * d), mod_map), tok(A_WIDTH), tok(B_WIDTH), tok(C_WIDTH),
                  const(w_out), const(fw)],
        out_specs=tok(d),
        out_shape=jax.ShapeDtypeStruct((bsz, n, d), F32),
        compiler_params=pltpu.CompilerParams(dimension_semantics=("arbitrary", "arbitrary"),
                                             vmem_limit_bytes=VMEM_LIMIT),
        name="outproj_residual_final" if final else "outproj_residual",
    )(x, mod, ya, yb, yc, w_out, fw)


def _rope_tables(n):
    pos = jnp.arange(n)
    half = B_HEAD_DIM // 2
    inv_freq = ROPE_THETA ** (-jnp.arange(0, half, 2, dtype=F32) / half)
    lane = np.arange(B_WIDTH)
    dd = lane % B_HEAD_DIM
    use_col = (dd // half) == 1
    second = ((dd % half) // (half // 2)) == 1
    fidx = dd % (half // 2)
    p = jnp.where(use_col[None, :], (pos % GRID_W)[:, None], (pos // GRID_W)[:, None]).astype(F32)
    ang = p * inv_freq[fidx][None, :]
    cos, sin = jnp.cos(ang), jnp.sin(ang)
    return cos, jnp.where(second[None, :], 0.0, -sin), jnp.where(second[None, :], sin, 0.0)


def kernel(x, c, ctx, c_ctx, w_ada, b_ada, norm_w, w_in, w_out, a_ln_w, a_ws, a_bs, b_lam, b_subln_w,
           c_conv_w, c_a_log, c_dt_bias, c_norm_w, final_norm_w):
    bsz, n, d = x.shape
    n_ctx = ctx.shape[1]
    depth = w_ada.shape[0]
    assert d == D_MODEL and n % GDN_BLOCK == 0 and n_ctx % GDN_BLOCK == 0 and n % GRID_W == 0
    tile = ATT_QTILE
    tile_ctx = min(tile, n_ctx)
    ptile = min(PROJ_TILE, n)

    rows = bsz + 1
    rows_pad = -(-rows // 8) * 8
    c_rows = jnp.concatenate([c, c_ctx[None, :], jnp.zeros((rows_pad - rows, d), F32)], axis=0)
    mods = _ada(c_rows, w_ada, b_ada)

    tabs = _rope_tables(n)
    tabs_ctx = tuple(t[:tile_ctx] for t in tabs)
    gate_pad = jnp.zeros((depth, d, GATE_PAD - N_GATE), w_in.dtype)
    w_in_p = jnp.concatenate([w_in, gate_pad], axis=-1).astype(MXU)
    w_out_b = w_out.astype(MXU)
    aws = a_ws.reshape(depth, A_GROUPS * A_CHUNK, A_CHUNK).astype(MXU)
    abias = jnp.repeat(jnp.swapaxes(a_bs, 1, 2), A_WIDTH // A_GROUPS, axis=2)
    row = lambda a: a.reshape(1, -1)

    xc = ctx
    for i in range(depth):
        lam_init = 0.8 - 0.6 * math.exp(-0.3 * i)
        ctx_out = i < depth - 1
        mod_l = mods[i, :bsz][:, None, :]
        mod_c = mods[i, bsz:bsz + 1][:, None, :]
        common = (row(norm_w[i]), w_in_p[i])
        a_par = (row(a_ln_w[i]), aws[i], abias[i])
        ya, qb, kb, vb, zb, craw, zc, gt = _proj(x, mod_l, *common, tabs, *a_par, rope=True, tile=ptile)
        ya_c, qb_c, kb_c, vb_c, zb_c, craw_c, zc_c, gt_c = _proj(xc, mod_c, *common, tabs_ctx, *a_par,
                                                                 rope=False, tile=tile_ctx)
        lam_p, sw = b_lam[i], row(b_subln_w[i])
        yb = _attn(qb, zb, lam_p, sw, [kb_c, kb], [vb_c, vb], lam_init=lam_init, tile=tile)
        conv_w = c_conv_w[i]
        gate_row = lambda p: jnp.concatenate([jnp.zeros((2 * C_HEADS,), F32), p.reshape(-1),
                                              jnp.zeros((GATE_PAD - N_GATE,), F32)])[None, :]
        gdn_par = (conv_w, gate_row(c_a_log[i]), gate_row(c_dt_bias[i]), jnp.tile(c_norm_w[i], C_HEADS)[None, :])
        yc_all = _gdn(craw_c, gt_c, zc_c, craw, gt, zc, *gdn_par, ctx_out=ctx_out)
        final = i == depth - 1
        x = _out(x, mod_l, ya, yb, yc_all[0], w_out_b[i], row(final_norm_w), final=final, tile=ptile)
        if ctx_out:
            yb_c = _attn(qb_c, zb_c, lam_p, sw, [kb_c], [vb_c], lam_init=lam_init, tile=tile_ctx)
            xc = _out(xc, mod_c, ya_c, yb_c, yc_all[1], w_out_b[i], row(final_norm_w), final=False, tile=tile_ctx)
    return x
```

```python
import functools
import math

import numpy as np
import jax
import jax.numpy as jnp
from jax import lax
from jax.experimental import pallas as pl
from jax.experimental.pallas import tpu as pltpu

F32 = jnp.float32
MXU = jnp.bfloat16

D_MODEL = 1024
NORM_EPS = 1e-6
GRID_W = 64
A_WIDTH, A_GROUPS, A_CHUNK, A_LN_EPS = 256, 4, 128, 1e-5
B_HEADS, B_HEAD_DIM, B_WIDTH = 4, 64, 512
ROPE_THETA = 10000.0
C_HEADS, C_HEAD_DIM, C_WIDTH, C_CONV, C_CHUNK = 4, 64, 256, 5, 64
N_GATE = 4 * C_HEADS
GATE_PAD = 128
OFF_A, OFF_B, OFF_C, OFF_G = 0, 3 * A_WIDTH, 3 * A_WIDTH + 4 * B_WIDTH, 3 * A_WIDTH + 4 * B_WIDTH + 4 * C_WIDTH
D_IN_PAD = OFF_G + GATE_PAD
NEG_BIG = -1e30
LOG2E = math.log2(math.e)
ATT_KCHUNK = 256
ATT_QTILE = 256
PROJ_TILE = 512
CONV_ROWS = 256
VMEM_LIMIT = 56 * 1024 * 1024
GDN_BLOCK = 256
HALO = 8
INV_BLOCK = 16


def _mm(a, b):
    return jnp.dot(a.astype(MXU), b.astype(MXU), preferred_element_type=F32)


def _mm_nt(a, b):
    return lax.dot_general(a.astype(MXU), b.astype(MXU), (((1,), (1,)), ((), ())), preferred_element_type=F32)


def _mm_tn(a, b):
    return lax.dot_general(a.astype(MXU), b.astype(MXU), (((0,), (0,)), ((), ())), preferred_element_type=F32)


def _split(a):
    hi = a.astype(MXU)
    return hi, (a - hi.astype(F32)).astype(MXU)


def _mm3(a, b):
    ah, al = _split(a)
    bh, bl = _split(b)
    d = lambda x, y: jnp.dot(x, y, preferred_element_type=F32)
    return d(ah, bh) + (d(ah, bl) + d(al, bh))


def _sigmoid(x):
    return 1.0 / (1.0 + jnp.exp(-x))


def _silu(x):
    return x * _sigmoid(x)


def _iota(shape, dim):
    return lax.broadcasted_iota(jnp.int32, shape, dim)


def _ada_kernel(c_ref, w_ref, b_ref, o_ref):
    o_ref[0] = _mm3(_silu(c_ref[...]), w_ref[0]) + b_ref[0]


def _ada(c_rows, w_ada, b_ada):
    depth, d, d3 = w_ada.shape
    r = c_rows.shape[0]
    tn = 512
    return pl.pallas_call(
        _ada_kernel,
        grid=(depth, d3 // tn),
        in_specs=[pl.BlockSpec((r, d), lambda l, j: (0, 0)),
                  pl.BlockSpec((1, d, tn), lambda l, j: (l, 0, j)),
                  pl.BlockSpec((1, 1, tn), lambda l, j: (l, 0, j))],
        out_specs=pl.BlockSpec((1, r, tn), lambda l, j: (l, 0, j)),
        out_shape=jax.ShapeDtypeStruct((depth, r, d3), F32),
        name="ada_modulation",
    )(c_rows, w_ada, b_ada.reshape(depth, 1, d3))


def _proj_kernel(x_ref, xp_ref, xn_ref, mod_ref, nw_ref, w_ref, cos_ref, sinn_ref, sinp_ref, lnw_ref, aws_ref,
                 abias_ref, convw_ref, ya_ref, q_ref, k_ref, v_ref, zb_ref, cqkv_ref, zc_ref, g_ref, *, rope):
    d = D_MODEL
    mod = mod_ref[0]

    def normed(x):
        y = x * lax.rsqrt(jnp.mean(x * x, axis=-1, keepdims=True) + NORM_EPS) * nw_ref[...]
        return (y * (1.0 + mod[:, d:2 * d]) + mod[:, :d]).astype(MXU)

    h = normed(x_ref[0])
    t = h.shape[0]
    proj = lambda lhs, lo, hi: jnp.dot(lhs, w_ref[:, lo:hi], preferred_element_type=F32)
    vals = {}

    def finish_a():
        pa = vals.pop("a")
        u, v, z = pa[:, :A_WIDTH], pa[:, A_WIDTH:2 * A_WIDTH], pa[:, 2 * A_WIDTH:]
        vc = v - jnp.mean(v, axis=-1, keepdims=True)
        vn = (vc * lax.rsqrt(jnp.mean(vc * vc, axis=-1, keepdims=True) + A_LN_EPS) * lnw_ref[...]).astype(MXU)
        grp = _iota((A_CHUNK, A_WIDTH), 1) // (A_WIDTH // A_GROUPS)
        for c in range(t // A_CHUNK):
            rows = slice(c * A_CHUNK, (c + 1) * A_CHUNK)
            r = jnp.dot(aws_ref[...], vn[rows], preferred_element_type=F32)
            s = abias_ref[...]
            for g in range(A_GROUPS):
                s = s + jnp.where(grp == g, r[g * A_CHUNK:(g + 1) * A_CHUNK], 0.0)
            ya_ref[0, rows, :] = (u[rows] * s * _silu(z[rows])).astype(ya_ref.dtype)

    def rot(a):
        if not rope:
            return a
        half = B_HEAD_DIM // 4
        return (a * cos_ref[...] + pltpu.roll(a, B_WIDTH - half, 1) * sinn_ref[...]
                + pltpu.roll(a, half, 1) * sinp_ref[...])

    def finish_q():
        q_ref[0] = (rot(vals.pop("q")) * (B_HEAD_DIM ** -0.5 * LOG2E)).astype(q_ref.dtype)

    def finish_k():
        k_ref[0] = rot(vals.pop("k")).astype(k_ref.dtype)

    def finish_v():
        v_ref[0] = vals.pop("v").T.astype(v_ref.dtype)

    def finish_z():
        zb_ref[0] = vals.pop("z").astype(zb_ref.dtype)

    sub = min(t, CONV_ROWS)

    def start_c():
        i, last = pl.program_id(0), pl.num_programs(0) - 1
        h_ext = jnp.concatenate([h, normed(xp_ref[0]), normed(xn_ref[0])], axis=0)
        pc = proj(h_ext, OFF_C, OFF_G)
        zc_ref[0] = pc[:t, 3 * C_WIDTH:]
        raw = pc[:, :3 * C_WIDTH]
        prev = jnp.where(i == 0, 0.0, raw[t:t + HALO])
        nxt = jnp.where(i == last, 0.0, raw[t + HALO:])
        vals["c"] = jnp.concatenate([prev, raw[:t], nxt], axis=0)

    def finish_c(r0):
        w = convw_ref[...]
        pad = C_CONV // 2
        win = vals["c"][r0:r0 + sub + 2 * HALO]
        acc = win[HALO:HALO + sub] * w[pad:pad + 1]
        for j in range(C_CONV):
            if j != pad:
                acc = acc + pltpu.roll(win, (pad - j) % (sub + 2 * HALO), 0)[HALO:HALO + sub] * w[j:j + 1]
        act = _silu(acc)
        qh, kh = act[:, :C_WIDTH], act[:, C_WIDTH:2 * C_WIDTH]
        l2 = lambda a: a * lax.rsqrt(_mm_sel(a * a, _same_head()) + 1e-6)
        cqkv_ref[0, r0:r0 + sub, 0:C_WIDTH] = l2(qh) * (C_HEAD_DIM ** -0.5)
        cqkv_ref[0, r0:r0 + sub, C_WIDTH:2 * C_WIDTH] = l2(kh)
        cqkv_ref[0, r0:r0 + sub, 2 * C_WIDTH:] = act[:, 2 * C_WIDTH:]

    def start(name, lo, hi):
        def go():
            vals[name] = proj(h, lo, hi)
        return go

    def finish_g():
        g_ref[0] = vals.pop("g")

    bq = OFF_B
    order = [start_c, start("a", OFF_A, OFF_B)]
    convs = [functools.partial(finish_c, r0) for r0 in range(0, t, sub)]
    later = [start("q", bq, bq + B_WIDTH), start("k", bq + B_WIDTH, bq + 2 * B_WIDTH),
             start("v", bq + 2 * B_WIDTH, bq + 3 * B_WIDTH), start("z", bq + 3 * B_WIDTH, OFF_C),
             start("g", OFF_G, D_IN_PAD)]
    fins = convs + [finish_a, finish_q, finish_k, finish_v, finish_z, finish_g]
    while later or fins:
        if later:
            order.append(later.pop(0))
        if fins:
            order.append(fins.pop(0))
    for step in order:
        step()


def _proj(x, mod, nw, w_in, tabs, lnw, aws, abias, convw, *, rope, tile):
    bsz, n, d = x.shape
    cos, sinn, sinp = tabs
    mod_map = (lambda i, b: (b, 0, 0)) if mod.shape[0] == bsz else (lambda i, b: (0, 0, 0))
    tok = lambda w: pl.BlockSpec((1, tile, w), lambda i, b: (b, i, 0))
    per = tile // HALO
    before = pl.BlockSpec((1, HALO, d), lambda i, b: (b, jnp.maximum(i * per - 1, 0), 0))
    after = pl.BlockSpec((1, HALO, d), lambda i, b: (b, jnp.minimum((i + 1) * per, n // HALO - 1), 0))
    const = lambda a: pl.BlockSpec(a.shape, lambda i, b: (0,) * a.ndim)
    tab = pl.BlockSpec((tile, B_WIDTH), lambda i, b: (i, 0))
    outs = [(A_WIDTH, MXU), (B_WIDTH, MXU), (B_WIDTH, MXU), (B_WIDTH, MXU), (B_WIDTH, F32),
            (3 * C_WIDTH, F32), (C_WIDTH, F32), (GATE_PAD, F32)]
    out_specs = [tok(w) for w, _ in outs]
    out_shape = [jax.ShapeDtypeStruct((bsz, n, w), dt) for w, dt in outs]
    out_specs[3] = pl.BlockSpec((1, B_WIDTH, tile), lambda i, b: (b, 0, i))
    out_shape[3] = jax.ShapeDtypeStruct((bsz, B_WIDTH, n), MXU)
    return pl.pallas_call(
        functools.partial(_proj_kernel, rope=rope),
        grid=(n // tile, bsz),
        in_specs=[tok(d), before, after, pl.BlockSpec((1, 1, 3 * d), mod_map), const(nw), const(w_in), tab, tab, tab,
                  const(lnw), const(aws), const(abias), const(convw)],
        out_specs=out_specs,
        out_shape=out_shape,
        compiler_params=pltpu.CompilerParams(dimension_semantics=("arbitrary", "arbitrary"),
                                             vmem_limit_bytes=VMEM_LIMIT),
        name="norm_inproj_rope" if rope else "norm_inproj_ctx",
    )(x, x, x, mod, nw, w_in, cos, sinn, sinp, lnw, aws, abias, convw)


def _interleave(gens):
    gens = list(gens)
    while gens:
        for g in list(gens):
            if next(g, StopIteration) is StopIteration:
                gens.remove(g)


def _attn_kernel(*refs, seg_lens, lam_init):
    nseg = len(seg_lens)
    q_ref, z_ref, lam_ref, sw_ref = refs[:4]
    k_refs, v_refs = refs[4:4 + nseg], refs[4 + nseg:4 + 2 * nseg]
    o_ref, s_buf, e_buf = refs[4 + 2 * nseg:]
    lf = lam_ref[...]
    lam = (jnp.exp(jnp.sum(lf[0:1] * lf[1:2], axis=-1, keepdims=True))
           - jnp.exp(jnp.sum(lf[2:3] * lf[3:4], axis=-1, keepdims=True)) + lam_init)
    q = q_ref[0]
    pair = 2 * B_HEAD_DIM
    first = _iota((1, pair), 1) < B_HEAD_DIM
    zero = jnp.zeros((), q.dtype)
    chunks, row = [], 0
    for seg, n in enumerate(seg_lens):
        for c0 in range(0, n, ATT_KCHUNK):
            chunks.append((seg, c0, row))
            row += ATT_KCHUNK
    kc = ATT_KCHUNK
    maxes, heads = {}, {}

    dens = {}

    def scores(i):
        hh, j = divmod(i, 2)
        lanes = slice(hh * pair, (hh + 1) * pair)
        qm = jnp.where(first if j == 0 else jnp.logical_not(first), q[:, lanes], zero)
        m = None
        for seg, c0, r0 in chunks:
            s = _mm_nt(k_refs[seg][0, c0:c0 + kc, lanes], qm)
            s_buf[hh % 2, j, r0:r0 + kc, :] = s
            cm = jnp.max(s, axis=0, keepdims=True)
            m = cm if m is None else jnp.maximum(m, cm)
            yield
        maxes[i] = m

    def exps(i):
        hh, j = divmod(i, 2)
        den = None
        for _, _, r0 in chunks:
            e = jnp.exp2(s_buf[hh % 2, j, r0:r0 + kc, :] - maxes[i])
            e_buf[hh % 2, j, r0:r0 + kc, :] = e.astype(MXU)
            cs = jnp.sum(e, axis=0, keepdims=True)
            den = cs if den is None else den + cs
            yield
        dens[i] = den

    def values(hh):
        lanes = slice(hh * pair, (hh + 1) * pair)
        d0, d1 = dens[2 * hh], dens[2 * hh + 1]
        coef = (lam * d0 * (1.0 / d1)).astype(MXU)
        pv = None
        for seg, c0, r0 in chunks:
            p = e_buf[hh % 2, 0, r0:r0 + kc, :] - coef * e_buf[hh % 2, 1, r0:r0 + kc, :]
            t = _mm(v_refs[seg][0, lanes, c0:c0 + kc], p)
            pv = t if pv is None else pv + t
            yield
        acc = (pv * (1.0 / d0)).T
        o = acc * lax.rsqrt(jnp.mean(acc * acc, axis=-1, keepdims=True) + NORM_EPS) * sw_ref[...]
        heads[hh] = o * (1.0 - lam_init)

    nh = 2 * B_HEADS
    for t in range(nh + 2):
        stage = []
        if t < nh:
            stage.append(scores(t))
        if 1 <= t <= nh:
            stage.append(exps(t - 1))
        if t >= 3 and t % 2 == 1:
            stage.append(values((t - 3) // 2))
        _interleave(stage)
    o_all = jnp.concatenate([heads[hh] for hh in range(B_HEADS)], axis=-1)
    o_ref[0] = (o_all * _silu(z_ref[0])).astype(o_ref.dtype)


def _attn(q, z, lam_p, subln_w, ks, vs, *, lam_init, tile):
    bsz, n, w = q.shape
    seg_lens = tuple(a.shape[1] for a in ks)
    tok = pl.BlockSpec((1, tile, w), lambda b, i: (b, i, 0))
    const = lambda a: pl.BlockSpec(a.shape, lambda b, i: (0,) * a.ndim)
    full = lambda a: pl.BlockSpec((1,) + a.shape[1:], lambda b, i: (b, 0, 0))
    return pl.pallas_call(
        functools.partial(_attn_kernel, seg_lens=seg_lens, lam_init=lam_init),
        grid=(bsz, n // tile),
        in_specs=[tok, tok, const(lam_p), const(subln_w)] + [full(a) for a in ks] + [full(a) for a in vs],
        out_specs=tok,
        out_shape=jax.ShapeDtypeStruct((bsz, n, w), MXU),
        scratch_shapes=[pltpu.VMEM((2, 2, sum(seg_lens), tile), F32), pltpu.VMEM((2, 2, sum(seg_lens), tile), MXU)],
        compiler_params=pltpu.CompilerParams(dimension_semantics=("arbitrary", "arbitrary"),
                                             vmem_limit_bytes=VMEM_LIMIT),
        name="diff_attention_%dseg" % len(ks),
    )(q, z, lam_p, subln_w, *ks, *vs)


def _head_of_lane(shape, dim):
    return _iota(shape, dim) // C_HEAD_DIM


def _expand_heads(cols):
    r = cols.shape[0]
    hl = _head_of_lane((r, C_WIDTH), 1)
    out = jnp.broadcast_to(cols[:, C_HEADS - 1:C_HEADS], (r, C_WIDTH))
    for h in range(C_HEADS - 2, -1, -1):
        out = jnp.where(hl == h, jnp.broadcast_to(cols[:, h:h + 1], (r, C_WIDTH)), out)
    return out


def _split3(a):
    a1 = a.astype(MXU)
    r1 = a - a1.astype(F32)
    a2 = r1.astype(MXU)
    return a1, a2, (r1 - a2.astype(F32)).astype(MXU)


def _sel_mm(sel, a):
    d = lambda x: jnp.dot(sel, x, preferred_element_type=F32)
    a1, a2, a3 = _split3(a)
    return d(a1) + (d(a2) + d(a3))


def _mm_sel(a, sel):
    d = lambda x: jnp.dot(x, sel, preferred_element_type=F32)
    a1, a2, a3 = _split3(a)
    return d(a1) + (d(a2) + d(a3))


def _same_head():
    return (_head_of_lane((C_WIDTH, C_WIDTH), 0) == _head_of_lane((C_WIDTH, C_WIDTH), 1)).astype(MXU)


def _gdn_prep(qkv_ref, gate_ref, n, off, alog_ref, dtb_ref, q_s, k_s, v_s, bf_s, bb_s, gf_s, gb_s):
    rows = GDN_BLOCK
    nblk = n // rows
    ri, rj = _iota((rows, rows), 0), _iota((rows, rows), 1)
    same_chunk = (ri // C_CHUNK) == (rj // C_CHUNK)
    csum_f = (same_chunk & (rj <= ri)).astype(MXU)
    csum_b = (same_chunk & (rj >= ri)).astype(MXU)

    def block(i, carry):
        base = pl.multiple_of(i * rows, rows)
        dst = pl.ds(pl.multiple_of(off + base, rows), rows)
        q_s[dst, :] = qkv_ref[0, pl.ds(base, rows), 0:C_WIDTH]
        k_s[dst, :] = qkv_ref[0, pl.ds(base, rows), C_WIDTH:2 * C_WIDTH]
        v_s[dst, :] = qkv_ref[0, pl.ds(base, rows), 2 * C_WIDTH:]
        gt = gate_ref[0, pl.ds(base, rows), :]
        h = C_HEADS
        beta = _sigmoid(gt)
        bf_s[dst, :] = _expand_heads(beta[:, 0:h])
        bb_s[dst, :] = _expand_heads(beta[:, h:2 * h])
        sp = jnp.maximum(gt + dtb_ref[...], 0.0) + jnp.log1p(jnp.exp(-jnp.abs(gt + dtb_ref[...])))
        g = -jnp.exp(alog_ref[...]) * sp
        gf_s[dst, :] = _expand_heads(_sel_mm(csum_f, g)[:, 2 * h:3 * h])
        gb_s[dst, :] = _expand_heads(_sel_mm(csum_b, g)[:, 3 * h:4 * h])
        return carry

    lax.fori_loop(0, nblk, block, 0)


def _block_diag(a, bmask):
    return jnp.where(bmask, jnp.concatenate([a] * C_HEADS, axis=0), 0.0)


def _gdn_local(chains, q_s, k_s, v_s, bufs, buf):
    c, wd = C_CHUNK, C_WIDTH
    u_buf, w_buf, a_buf, qd_buf, kd_buf, eg_buf = bufs
    ii = _iota((c, wd), 0)
    jj = _iota((c, wd), 1) % c
    eye = jj == ii
    bmask = _head_of_lane((C_HEADS * c, wd), 1) == (_iota((C_HEADS * c, wd), 0) // c)
    bd = lambda a: _block_diag(a, bmask)
    ident = jnp.where(eye, 1.0, 0.0)

    st = []
    for rows, backward, beta_s, gc_s, _, _ in chains:
        q, k, v, beta, gc = q_s[rows, :], k_s[rows, :], v_s[rows, :], beta_s[rows, :], gc_s[rows, :]
        incl = (jj >= ii) if backward else (jj <= ii)
        strict = (jj > ii) if backward else (jj < ii)
        grow = jnp.sum(jnp.where(eye, gc, 0.0), axis=0, keepdims=True)
        decay = jnp.exp(jnp.where(incl, gc - grow, NEG_BIG))
        glast = gc[0:1] if backward else gc[c - 1:c]
        st.append(dict(q=q, k=k, v=v, beta=beta, gc=gc, decay=decay, strict=strict, glast=glast, kb=k * beta,
                       egc=jnp.exp(gc)))
    yield
    for s in st:
        qk = _mm_nt(jnp.concatenate([s["kb"], s["q"]], axis=0), bd(s["k"]))
        s["lmat"] = jnp.where(s["strict"], qk[:c] * s["decay"], 0.0)
        s["attn"] = qk[c:] * s["decay"]
    yield
    in_blk = (ii // INV_BLOCK) == (jj // INV_BLOCK)
    for s in st:
        dmat = jnp.where(in_blk, s["lmat"], 0.0)
        s["t"] = ident - dmat
        s["pw"] = _mm(dmat, bd(dmat))
    yield
    nsq = int(math.log2(INV_BLOCK)) - 1
    for step in range(nsq):
        for s in st:
            b = bd(s["pw"])
            if step < nsq - 1:
                r = _mm(jnp.concatenate([s["t"], s["pw"]], axis=0), b)
                s["t"], s["pw"] = s["t"] + r[:c], r[c:]
            else:
                s["t"] = s["t"] + _mm(s["t"], b)
        yield
    size = INV_BLOCK
    while size < c:
        emask = ((ii // (2 * size)) == (jj // (2 * size))) & ((ii // size) != (jj // size))
        for s in st:
            s["y"] = _mm(jnp.where(emask, s["lmat"], 0.0), bd(s["t"]))
        yield
        for s in st:
            s["t"] = s["t"] - _mm(s["t"], bd(s["y"]))
        yield
        size *= 2
    for s, (_, _, _, _, d, slot) in zip(st, chains):
        dst = slice(slot * c, (slot + 1) * c)
        u_buf[buf, d, dst, :] = _mm(s["t"], bd(s["v"] * s["beta"]))
        a_buf[buf, d, dst, :] = s["attn"].astype(MXU)
        qd_buf[buf, d, dst, :] = (s["q"] * s["egc"]).astype(MXU)
    yield
    for s, (_, _, _, _, d, slot) in zip(st, chains):
        dst = slice(slot * c, (slot + 1) * c)
        w_buf[buf, d, dst, :] = _mm(s["t"], bd(s["kb"] * s["egc"])).astype(MXU)
        kd_buf[buf, d, dst, :] = (s["k"] * jnp.exp(s["glast"] - s["gc"])).astype(MXU)
        eg_buf[buf, d, slot * HALO:(slot + 1) * HALO, :] = jnp.broadcast_to(jnp.exp(s["glast"]), (HALO, wd))


def _gdn_scan(steps, rows, st_s, o_refs, bufs, buf):
    c, wd = C_CHUNK, C_WIDTH
    u_buf, w_buf, a_buf, qd_buf, kd_buf, eg_buf = bufs
    bmask = _head_of_lane((C_HEADS * c, wd), 1) == (_iota((C_HEADS * c, wd), 0) // c)
    smask = _head_of_lane((wd, wd), 0) == _head_of_lane((wd, wd), 1)
    states = [st_s[0], st_s[1]]
    for slots in steps:
        src = [slice(slot * c, (slot + 1) * c) for slot in slots]
        ws = [_mm(jnp.concatenate([w_buf[buf, d, r, :], qd_buf[buf, d, r, :]], axis=0), states[d])
              for d, r in enumerate(src)]
        yield
        v_new = [u_buf[buf, d, r, :] - ws[d][:c] for d, r in enumerate(src)]
        upd = [_mm_tn(kd_buf[buf, d, r, :], v_new[d]) for d, r in enumerate(src)]
        for d, (r, slot) in enumerate(zip(src, slots)):
            o_refs[d][rows[d][slot], :] = ws[d][c:] + _mm(a_buf[buf, d, r, :], _block_diag(v_new[d], bmask))
        yield
        states = [states[d] * eg_buf[buf, d, slot * HALO:slot * HALO + 1, :] + jnp.where(smask, upd[d], 0.0)
                  for d, slot in enumerate(slots)]
    st_s[0], st_s[1] = states


def _gdn_kernel(*refs, n_ctx, n_lat, ctx_out):
    (rawc_ref, gtc_ref, zc_ref, rawl_ref, gtl_ref, zl_ref, alog_ref, dtb_ref, nw_ref) = refs[:9]
    nout = 2 if ctx_out else 1
    out_refs = refs[9:9 + nout]
    (q_s, k_s, v_s, bf_s, bb_s, gf_s, gb_s, of_s, ob_s, st_s, u_buf, w_buf, a_buf, qd_buf, kd_buf, eg_buf) = refs[9 + nout:]
    scr = (q_s, k_s, v_s, bf_s, bb_s, gf_s, gb_s)
    bufs = (u_buf, w_buf, a_buf, qd_buf, kd_buf, eg_buf)
    _gdn_prep(rawc_ref, gtc_ref, n_ctx, 0, alog_ref, dtb_ref, *scr)
    _gdn_prep(rawl_ref, gtl_ref, n_lat, n_ctx, alog_ref, dtb_ref, *scr)
    st_s[...] = jnp.zeros_like(st_s)
    ncb, nlb = n_ctx // GDN_BLOCK, n_lat // GDN_BLOCK
    per = GDN_BLOCK // C_CHUNK

    nblk = ncb + nlb
    steps = [(t, per - 1 - t) for t in range(per)]

    def block_rows(i):
        i = jnp.asarray(i, jnp.int32)
        bf = pl.multiple_of(i * GDN_BLOCK, GDN_BLOCK)
        bb = pl.multiple_of(jnp.where(i < ncb, ncb - 1 - i, 2 * ncb + nlb - 1 - i) * GDN_BLOCK, GDN_BLOCK)
        return ([pl.ds(bf + r * C_CHUNK, C_CHUNK) for r in range(per)],
                [pl.ds(bb + r * C_CHUNK, C_CHUNK) for r in range(per)])

    def local(i):
        rows_f, rows_b = block_rows(i)
        chains = ([(rows_f[r], False, bf_s, gf_s, 0, r) for r in range(per)]
                  + [(rows_b[r], True, bb_s, gb_s, 1, r) for r in range(per)])
        return _gdn_local(chains, q_s, k_s, v_s, bufs, i % 2)

    def scan(i):
        return _gdn_scan(steps, block_rows(i), st_s, (of_s, ob_s), bufs, i % 2)

    _interleave([local(0)])

    def block(i, carry):
        _interleave([local(i + 1), scan(i)])
        return carry

    lax.fori_loop(0, nblk - 1, block, 0)
    _interleave([scan(nblk - 1)])

    same_head = _same_head()

    def finish(z_ref, out_ref, off, n):
        rows = GDN_BLOCK

        def blk(i, carry):
            base = pl.multiple_of(i * rows, rows)
            src = pl.ds(pl.multiple_of(off + base, rows), rows)
            o = of_s[src, :] + ob_s[src, :]
            ms = _mm_sel(o * o, same_head) * (1.0 / C_HEAD_DIM)
            y = o * lax.rsqrt(ms + NORM_EPS) * nw_ref[...]
            out_ref[0, pl.ds(base, rows), :] = (y * _silu(z_ref[0, pl.ds(base, rows), :])).astype(out_ref.dtype)
            return carry

        lax.fori_loop(0, n // rows, blk, 0)

    finish(zl_ref, out_refs[0], n_ctx, n_lat)
    if ctx_out:
        finish(zc_ref, out_refs[1], 0, n_ctx)


def _gdn(rawc, gtc, zc, rawl, gtl, zl, alog, dtb, nw, *, ctx_out):
    bsz, n_ctx, _ = rawc.shape
    n_lat = rawl.shape[1]
    nt = n_ctx + n_lat
    seq = lambda a: pl.BlockSpec((1,) + a.shape[1:], lambda b: (b, 0, 0))
    const = lambda a: pl.BlockSpec(a.shape, lambda b: (0,) * a.ndim)
    out_shape = [jax.ShapeDtypeStruct((bsz, n_lat, C_WIDTH), MXU)]
    if ctx_out:
        out_shape.append(jax.ShapeDtypeStruct((bsz, n_ctx, C_WIDTH), MXU))
    big = lambda: pltpu.VMEM((nt, C_WIDTH), F32)
    loc = lambda dt: pltpu.VMEM((2, 2, GDN_BLOCK, C_WIDTH), dt)
    per = GDN_BLOCK // C_CHUNK
    res = pl.pallas_call(
        functools.partial(_gdn_kernel, n_ctx=n_ctx, n_lat=n_lat, ctx_out=ctx_out),
        grid=(bsz,),
        in_specs=[seq(rawc), seq(gtc), seq(zc), seq(rawl), seq(gtl), seq(zl),
                  const(alog), const(dtb), const(nw)],
        out_specs=[seq(s) for s in out_shape],
        out_shape=out_shape,
        scratch_shapes=[big() for _ in range(9)] + [pltpu.VMEM((2, C_WIDTH, C_WIDTH), F32)]
        + [loc(F32), loc(MXU), loc(MXU), loc(MXU), loc(MXU), pltpu.VMEM((2, 2, per * HALO, C_WIDTH), F32)],
        compiler_params=pltpu.CompilerParams(dimension_semantics=("arbitrary",), vmem_limit_bytes=VMEM_LIMIT),
        name="gated_deltanet_bidir",
    )(rawc, gtc, zc, rawl, gtl, zl, alog, dtb, nw)
    return res


def _out_kernel(x_ref, mod_ref, ya_ref, yb_ref, yc_ref, w_ref, fw_ref, o_ref, *, final):
    d = D_MODEL
    y = (jnp.dot(ya_ref[0], w_ref[0:A_WIDTH], preferred_element_type=F32)
         + jnp.dot(yb_ref[0], w_ref[A_WIDTH:A_WIDTH + B_WIDTH], preferred_element_type=F32)
         + jnp.dot(yc_ref[0], w_ref[A_WIDTH + B_WIDTH:], preferred_element_type=F32))
    x = x_ref[0] + mod_ref[0][:, 2 * d:] * y
    if final:
        x = x * lax.rsqrt(jnp.mean(x * x, axis=-1, keepdims=True) + NORM_EPS) * fw_ref[...]
    o_ref[0] = x


def _out(x, mod, ya, yb, yc, w_out, fw, *, final, tile):
    bsz, n, d = x.shape
    mod_map = (lambda b, i: (b, 0, 0)) if mod.shape[0] == bsz else (lambda b, i: (0, 0, 0))
    tok = lambda w: pl.BlockSpec((1, tile, w), lambda b, i: (b, i, 0))
    const = lambda a: pl.BlockSpec(a.shape, lambda b, i: (0,) * a.ndim)
    return pl.pallas_call(
        functools.partial(_out_kernel, final=final),
        grid=(bsz, n // tile),
        in_specs=[tok(d), pl.BlockSpec((1, 1, 3 * d), mod_map), tok(A_WIDTH), tok(B_WIDTH), tok(C_WIDTH),
                  const(w_out), const(fw)],
        out_specs=tok(d),
        out_shape=jax.ShapeDtypeStruct((bsz, n, d), F32),
        compiler_params=pltpu.CompilerParams(dimension_semantics=("arbitrary", "arbitrary"),
                                             vmem_limit_bytes=VMEM_LIMIT),
        name="outproj_residual_final" if final else "outproj_residual",
    )(x, mod, ya, yb, yc, w_out, fw)


def _rope_tables(n):
    pos = jnp.arange(n)
    half = B_HEAD_DIM // 2
    inv_freq = ROPE_THETA ** (-jnp.arange(0, half, 2, dtype=F32) / half)
    lane = np.arange(B_WIDTH)
    dd = lane % B_HEAD_DIM
    use_col = (dd // half) == 1
    second = ((dd % half) // (half // 2)) == 1
    fidx = dd % (half // 2)
    p = jnp.where(use_col[None, :], (pos % GRID_W)[:, None], (pos // GRID_W)[:, None]).astype(F32)
    ang = p * inv_freq[fidx][None, :]
    cos, sin = jnp.cos(ang), jnp.sin(ang)
    return cos, jnp.where(second[None, :], 0.0, -sin), jnp.where(second[None, :], sin, 0.0)


def kernel(x, c, ctx, c_ctx, w_ada, b_ada, norm_w, w_in, w_out, a_ln_w, a_ws, a_bs, b_lam, b_subln_w,
           c_conv_w, c_a_log, c_dt_bias, c_norm_w, final_norm_w):
    bsz, n, d = x.shape
    n_ctx = ctx.shape[1]
    depth = w_ada.shape[0]
    assert d == D_MODEL and n % GDN_BLOCK == 0 and n_ctx % GDN_BLOCK == 0 and n % GRID_W == 0
    tile = ATT_QTILE
    tile_ctx = min(tile, n_ctx)
    ptile = min(PROJ_TILE, n)

    rows = bsz + 1
    rows_pad = -(-rows // 8) * 8
    c_rows = jnp.concatenate([c, c_ctx[None, :], jnp.zeros((rows_pad - rows, d), F32)], axis=0)
    mods = _ada(c_rows, w_ada, b_ada)

    tabs = _rope_tables(n)
    tabs_ctx = tuple(t[:tile_ctx] for t in tabs)
    gate_pad = jnp.zeros((depth, d, GATE_PAD - N_GATE), w_in.dtype)
    w_in_p = jnp.concatenate([w_in, gate_pad], axis=-1).astype(MXU)
    w_out_b = w_out.astype(MXU)
    aws = a_ws.reshape(depth, A_GROUPS * A_CHUNK, A_CHUNK).astype(MXU)
    abias = jnp.repeat(jnp.swapaxes(a_bs, 1, 2), A_WIDTH // A_GROUPS, axis=2)
    row = lambda a: a.reshape(1, -1)

    xc = ctx
    for i in range(depth):
        lam_init = 0.8 - 0.6 * math.exp(-0.3 * i)
        ctx_out = i < depth - 1
        mod_l = mods[i, :bsz][:, None, :]
        mod_c = mods[i, bsz:bsz + 1][:, None, :]
        common = (row(norm_w[i]), w_in_p[i])
        a_par = (row(a_ln_w[i]), aws[i], abias[i], c_conv_w[i])
        ya, qb, kb, vb, zb, craw, zc, gt = _proj(x, mod_l, *common, tabs, *a_par, rope=True, tile=ptile)
        ya_c, qb_c, kb_c, vb_c, zb_c, craw_c, zc_c, gt_c = _proj(xc, mod_c, *common, tabs_ctx, *a_par,
                                                                 rope=False, tile=tile_ctx)
        lam_p, sw = b_lam[i], row(b_subln_w[i])
        yb = _attn(qb, zb, lam_p, sw, [kb_c, kb], [vb_c, vb], lam_init=lam_init, tile=tile)
        gate_row = lambda p: jnp.concatenate([jnp.zeros((2 * C_HEADS,), F32), p.reshape(-1),
                                              jnp.zeros((GATE_PAD - N_GATE,), F32)])[None, :]
        gdn_par = (gate_row(c_a_log[i]), gate_row(c_dt_bias[i]), jnp.tile(c_norm_w[i], C_HEADS)[None, :])
        yc_all = _gdn(craw_c, gt_c, zc_c, craw, gt, zc, *gdn_par, ctx_out=ctx_out)
        final = i == depth - 1
        x = _out(x, mod_l, ya, yb, yc_all[0], w_out_b[i], row(final_norm_w), final=final, tile=ptile)
        if ctx_out:
            yb_c = _attn(qb_c, zb_c, lam_p, sw, [kb_c], [vb_c], lam_init=lam_init, tile=tile_ctx)
            xc = _out(xc, mod_c, ya_c, yb_c, yc_all[1], w_out_b[i], row(final_norm_w), final=False, tile=tile_ctx)
    return x
```

```python
import functools
import math

import numpy as np
import jax
import jax.numpy as jnp
from jax import lax
from jax.experimental import pallas as pl
from jax.experimental.pallas import tpu as pltpu

F32 = jnp.float32
MXU = jnp.bfloat16

D_MODEL = 1024
NORM_EPS = 1e-6
GRID_W = 64
A_WIDTH, A_GROUPS, A_CHUNK, A_LN_EPS = 256, 4, 128, 1e-5
B_HEADS, B_HEAD_DIM, B_WIDTH = 4, 64, 512
ROPE_THETA = 10000.0
C_HEADS, C_HEAD_DIM, C_WIDTH, C_CONV, C_CHUNK = 4, 64, 256, 5, 64
N_GATE = 4 * C_HEADS
GATE_PAD = 128
OFF_A, OFF_B, OFF_C, OFF_G = 0, 3 * A_WIDTH, 3 * A_WIDTH + 4 * B_WIDTH, 3 * A_WIDTH + 4 * B_WIDTH + 4 * C_WIDTH
D_IN_PAD = OFF_G + GATE_PAD
NEG_BIG = -1e30
LOG2E = math.log2(math.e)
ATT_KCHUNK = 256
ATT_SCHUNK = 1024
ATT_QTILE = 256
ATT_TILES = 2
PROJ_TILE = 512
CONV_ROWS = 256
VMEM_LIMIT = 56 * 1024 * 1024
GDN_BLOCK = 256
HALO = 8
INV_BLOCK = 16


def _mm(a, b):
    return jnp.dot(a.astype(MXU), b.astype(MXU), preferred_element_type=F32)


def _mm_nt(a, b):
    return lax.dot_general(a.astype(MXU), b.astype(MXU), (((1,), (1,)), ((), ())), preferred_element_type=F32)


def _mm_tn(a, b):
    return lax.dot_general(a.astype(MXU), b.astype(MXU), (((0,), (0,)), ((), ())), preferred_element_type=F32)


def _split(a):
    hi = a.astype(MXU)
    return hi, (a - hi.astype(F32)).astype(MXU)


def _mm3(a, b):
    ah, al = _split(a)
    bh, bl = _split(b)
    d = lambda x, y: jnp.dot(x, y, preferred_element_type=F32)
    return d(ah, bh) + (d(ah, bl) + d(al, bh))


def _sigmoid(x):
    return 1.0 / (1.0 + jnp.exp(-x))


def _silu(x):
    return x * _sigmoid(x)


def _iota(shape, dim):
    return lax.broadcasted_iota(jnp.int32, shape, dim)


def _ada_kernel(c_ref, w_ref, b_ref, o_ref):
    o_ref[0] = _mm3(_silu(c_ref[...]), w_ref[0]) + b_ref[0]


def _ada(c_rows, w_ada, b_ada):
    depth, d, d3 = w_ada.shape
    r = c_rows.shape[0]
    tn = 512
    return pl.pallas_call(
        _ada_kernel,
        grid=(depth, d3 // tn),
        in_specs=[pl.BlockSpec((r, d), lambda l, j: (0, 0)),
                  pl.BlockSpec((1, d, tn), lambda l, j: (l, 0, j)),
                  pl.BlockSpec((1, 1, tn), lambda l, j: (l, 0, j))],
        out_specs=pl.BlockSpec((1, r, tn), lambda l, j: (l, 0, j)),
        out_shape=jax.ShapeDtypeStruct((depth, r, d3), F32),
        name="ada_modulation",
    )(c_rows, w_ada, b_ada.reshape(depth, 1, d3))


def _proj_kernel(x_ref, xp_ref, xn_ref, mod_ref, nw_ref, w_ref, cos_ref, sinn_ref, sinp_ref, lnw_ref, aws_ref,
                 abias_ref, convw_ref, ya_ref, q_ref, k_ref, v_ref, zb_ref, cqkv_ref, zc_ref, g_ref, *, rope):
    d = D_MODEL
    mod = mod_ref[0]

    def normed(x):
        y = x * lax.rsqrt(jnp.mean(x * x, axis=-1, keepdims=True) + NORM_EPS) * nw_ref[...]
        return (y * (1.0 + mod[:, d:2 * d]) + mod[:, :d]).astype(MXU)

    h = normed(x_ref[0])
    t = h.shape[0]
    proj = lambda lhs, lo, hi: jnp.dot(lhs, w_ref[:, lo:hi], preferred_element_type=F32)
    vals = {}

    def finish_a():
        pa = vals.pop("a")
        u, v, z = pa[:, :A_WIDTH], pa[:, A_WIDTH:2 * A_WIDTH], pa[:, 2 * A_WIDTH:]
        vc = v - jnp.mean(v, axis=-1, keepdims=True)
        vn = (vc * lax.rsqrt(jnp.mean(vc * vc, axis=-1, keepdims=True) + A_LN_EPS) * lnw_ref[...]).astype(MXU)
        grp = _iota((A_CHUNK, A_WIDTH), 1) // (A_WIDTH // A_GROUPS)
        for c in range(t // A_CHUNK):
            rows = slice(c * A_CHUNK, (c + 1) * A_CHUNK)
            r = jnp.dot(aws_ref[...], vn[rows], preferred_element_type=F32)
            s = abias_ref[...]
            for g in range(A_GROUPS):
                s = s + jnp.where(grp == g, r[g * A_CHUNK:(g + 1) * A_CHUNK], 0.0)
            ya_ref[0, rows, :] = (u[rows] * s * _silu(z[rows])).astype(ya_ref.dtype)

    def rot(a):
        if not rope:
            return a
        half = B_HEAD_DIM // 4
        return (a * cos_ref[...] + pltpu.roll(a, B_WIDTH - half, 1) * sinn_ref[...]
                + pltpu.roll(a, half, 1) * sinp_ref[...])

    def finish_q():
        q_ref[0] = (rot(vals.pop("q")) * (B_HEAD_DIM ** -0.5 * LOG2E)).astype(q_ref.dtype)

    def finish_k():
        k_ref[0] = rot(vals.pop("k")).astype(k_ref.dtype)

    def finish_v():
        v_ref[0] = vals.pop("v").T.astype(v_ref.dtype)

    def finish_z():
        zb_ref[0] = vals.pop("z").astype(zb_ref.dtype)

    sub = min(t, CONV_ROWS)

    def start_c():
        i, last = pl.program_id(0), pl.num_programs(0) - 1
        h_ext = jnp.concatenate([h, normed(xp_ref[0]), normed(xn_ref[0])], axis=0)
        pc = proj(h_ext, OFF_C, OFF_G)
        zc_ref[0] = pc[:t, 3 * C_WIDTH:]
        raw = pc[:, :3 * C_WIDTH]
        prev = jnp.where(i == 0, 0.0, raw[t:t + HALO])
        nxt = jnp.where(i == last, 0.0, raw[t + HALO:])
        vals["c"] = jnp.concatenate([prev, raw[:t], nxt], axis=0)

    def finish_c(r0):
        w = convw_ref[...]
        pad = C_CONV // 2
        win = vals["c"][r0:r0 + sub + 2 * HALO]
        acc = win[HALO:HALO + sub] * w[pad:pad + 1]
        for j in range(C_CONV):
            if j != pad:
                acc = acc + pltpu.roll(win, (pad - j) % (sub + 2 * HALO), 0)[HALO:HALO + sub] * w[j:j + 1]
        act = _silu(acc)
        qh, kh = act[:, :C_WIDTH], act[:, C_WIDTH:2 * C_WIDTH]
        l2 = lambda a: a * lax.rsqrt(_mm_sel(a * a, _same_head()) + 1e-6)
        cqkv_ref[0, r0:r0 + sub, 0:C_WIDTH] = l2(qh) * (C_HEAD_DIM ** -0.5)
        cqkv_ref[0, r0:r0 + sub, C_WIDTH:2 * C_WIDTH] = l2(kh)
        cqkv_ref[0, r0:r0 + sub, 2 * C_WIDTH:] = act[:, 2 * C_WIDTH:]

    def start(name, lo, hi):
        def go():
            vals[name] = proj(h, lo, hi)
        return go

    def finish_g():
        g_ref[0] = vals.pop("g")

    bq = OFF_B
    order = [start_c, start("a", OFF_A, OFF_B)]
    convs = [functools.partial(finish_c, r0) for r0 in range(0, t, sub)]
    later = [start("q", bq, bq + B_WIDTH), start("k", bq + B_WIDTH, bq + 2 * B_WIDTH),
             start("v", bq + 2 * B_WIDTH, bq + 3 * B_WIDTH), start("z", bq + 3 * B_WIDTH, OFF_C),
             start("g", OFF_G, D_IN_PAD)]
    fins = convs + [finish_a, finish_q, finish_k, finish_v, finish_z, finish_g]
    while later or fins:
        if later:
            order.append(later.pop(0))
        if fins:
            order.append(fins.pop(0))
    for step in order:
        step()


def _proj(x, mod, nw, w_in, tabs, lnw, aws, abias, convw, *, rope, tile):
    bsz, n, d = x.shape
    cos, sinn, sinp = tabs
    mod_map = (lambda i, b: (b, 0, 0)) if mod.shape[0] == bsz else (lambda i, b: (0, 0, 0))
    tok = lambda w: pl.BlockSpec((1, tile, w), lambda i, b: (b, i, 0))
    per = tile // HALO
    before = pl.BlockSpec((1, HALO, d), lambda i, b: (b, jnp.maximum(i * per - 1, 0), 0))
    after = pl.BlockSpec((1, HALO, d), lambda i, b: (b, jnp.minimum((i + 1) * per, n // HALO - 1), 0))
    const = lambda a: pl.BlockSpec(a.shape, lambda i, b: (0,) * a.ndim)
    tab = pl.BlockSpec((tile, B_WIDTH), lambda i, b: (i, 0))
    outs = [(A_WIDTH, MXU), (B_WIDTH, MXU), (B_WIDTH, MXU), (B_WIDTH, MXU), (B_WIDTH, F32),
            (3 * C_WIDTH, F32), (C_WIDTH, F32), (GATE_PAD, F32)]
    out_specs = [tok(w) for w, _ in outs]
    out_shape = [jax.ShapeDtypeStruct((bsz, n, w), dt) for w, dt in outs]
    out_specs[3] = pl.BlockSpec((1, B_WIDTH, tile), lambda i, b: (b, 0, i))
    out_shape[3] = jax.ShapeDtypeStruct((bsz, B_WIDTH, n), MXU)
    return pl.pallas_call(
        functools.partial(_proj_kernel, rope=rope),
        grid=(n // tile, bsz),
        in_specs=[tok(d), before, after, pl.BlockSpec((1, 1, 3 * d), mod_map), const(nw), const(w_in), tab, tab, tab,
                  const(lnw), const(aws), const(abias), const(convw)],
        out_specs=out_specs,
        out_shape=out_shape,
        compiler_params=pltpu.CompilerParams(dimension_semantics=("arbitrary", "arbitrary"),
                                             vmem_limit_bytes=VMEM_LIMIT),
        name="norm_inproj_rope" if rope else "norm_inproj_ctx",
    )(x, x, x, mod, nw, w_in, cos, sinn, sinp, lnw, aws, abias, convw)


def _interleave(gens):
    gens = list(gens)
    while gens:
        for g in list(gens):
            if next(g, StopIteration) is StopIteration:
                gens.remove(g)


def _attn_kernel(*refs, seg_lens, lam_init):
    nseg = len(seg_lens)
    q_ref, z_ref, lam_ref, sw_ref = refs[:4]
    k_refs, v_refs = refs[4:4 + nseg], refs[4 + nseg:4 + 2 * nseg]
    o_ref, s_buf, e_buf = refs[4 + 2 * nseg:]
    lf = lam_ref[...]
    lam = (jnp.exp(jnp.sum(lf[0:1] * lf[1:2], axis=-1, keepdims=True))
           - jnp.exp(jnp.sum(lf[2:3] * lf[3:4], axis=-1, keepdims=True)) + lam_init)
    tq = ATT_QTILE
    ntile = q_ref.shape[1] // tq
    pair = 2 * B_HEAD_DIM
    first = _iota((1, pair), 1) < B_HEAD_DIM
    zero = jnp.zeros((), q_ref.dtype)
    chunks, row = [], 0
    for seg, n in enumerate(seg_lens):
        for c0 in range(0, n, ATT_KCHUNK):
            chunks.append((seg, c0, row))
            row += ATT_KCHUNK
    kc = ATT_KCHUNK
    maxes, heads = {}, {}

    dens = {}

    def scores(i):
        hh, j = divmod(i, 2)
        qt, hp = divmod(hh, B_HEADS)
        lanes = slice(hp * pair, (hp + 1) * pair)
        qm = jnp.where(first if j == 0 else jnp.logical_not(first), q_ref[0, qt * tq:(qt + 1) * tq, lanes], zero)
        m, row = None, 0
        for seg, n in enumerate(seg_lens):
            for c0 in range(0, n, ATT_SCHUNK):
                size = min(ATT_SCHUNK, n - c0)
                s = _mm_nt(k_refs[seg][0, c0:c0 + size, lanes], qm)
                for r in range(0, size, kc):
                    s_buf[hh % 2, j, row:row + kc, :] = s[r:r + kc]
                    cm = jnp.max(s[r:r + kc], axis=0, keepdims=True)
                    m = cm if m is None else jnp.maximum(m, cm)
                    row += kc
                    yield
        maxes[i] = m

    def exps(i):
        hh, j = divmod(i, 2)
        den = None
        for _, _, r0 in chunks:
            e = jnp.exp2(s_buf[hh % 2, j, r0:r0 + kc, :] - maxes[i])
            e_buf[hh % 2, j, r0:r0 + kc, :] = e.astype(MXU)
            cs = jnp.sum(e, axis=0, keepdims=True)
            den = cs if den is None else den + cs
            yield
        dens[i] = den

    def values(hh):
        qt, hp = divmod(hh, B_HEADS)
        lanes = slice(hp * pair, (hp + 1) * pair)
        d0, d1 = dens[2 * hh], dens[2 * hh + 1]
        coef = (lam * d0 * (1.0 / d1)).astype(MXU)
        pv = None
        for seg, c0, r0 in chunks:
            p = e_buf[hh % 2, 0, r0:r0 + kc, :] - coef * e_buf[hh % 2, 1, r0:r0 + kc, :]
            t = _mm(v_refs[seg][0, lanes, c0:c0 + kc], p)
            pv = t if pv is None else pv + t
            yield
        acc = (pv * (1.0 / d0)).T
        o = acc * lax.rsqrt(jnp.mean(acc * acc, axis=-1, keepdims=True) + NORM_EPS) * sw_ref[...]
        heads[hh] = o * (1.0 - lam_init)
        if hp == B_HEADS - 1:
            rows = slice(qt * tq, (qt + 1) * tq)
            o_all = jnp.concatenate([heads.pop(qt * B_HEADS + p) for p in range(B_HEADS)], axis=-1)
            o_ref[0, rows, :] = (o_all * _silu(z_ref[0, rows, :])).astype(o_ref.dtype)

    nh = 2 * B_HEADS * ntile
    for t in range(nh + 2):
        stage = []
        if t < nh:
            stage.append(scores(t))
        if 1 <= t <= nh:
            stage.append(exps(t - 1))
        if t >= 3 and t % 2 == 1:
            stage.append(values((t - 3) // 2))
        _interleave(stage)


def _attn(q, z, lam_p, subln_w, ks, vs, *, lam_init, tile):
    bsz, n, w = q.shape
    seg_lens = tuple(a.shape[1] for a in ks)
    tok = pl.BlockSpec((1, tile, w), lambda b, i: (b, i, 0))
    const = lambda a: pl.BlockSpec(a.shape, lambda b, i: (0,) * a.ndim)
    full = lambda a: pl.BlockSpec((1,) + a.shape[1:], lambda b, i: (b, 0, 0))
    return pl.pallas_call(
        functools.partial(_attn_kernel, seg_lens=seg_lens, lam_init=lam_init),
        grid=(bsz, n // tile),
        in_specs=[tok, tok, const(lam_p), const(subln_w)] + [full(a) for a in ks] + [full(a) for a in vs],
        out_specs=tok,
        out_shape=jax.ShapeDtypeStruct((bsz, n, w), MXU),
        scratch_shapes=[pltpu.VMEM((2, 2, sum(seg_lens), ATT_QTILE), F32),
                        pltpu.VMEM((2, 2, sum(seg_lens), ATT_QTILE), MXU)],
        compiler_params=pltpu.CompilerParams(dimension_semantics=("arbitrary", "arbitrary"),
                                             vmem_limit_bytes=VMEM_LIMIT),
        name="diff_attention_%dseg" % len(ks),
    )(q, z, lam_p, subln_w, *ks, *vs)


def _head_of_lane(shape, dim):
    return _iota(shape, dim) // C_HEAD_DIM


def _expand_heads(cols):
    r = cols.shape[0]
    hl = _head_of_lane((r, C_WIDTH), 1)
    out = jnp.broadcast_to(cols[:, C_HEADS - 1:C_HEADS], (r, C_WIDTH))
    for h in range(C_HEADS - 2, -1, -1):
        out = jnp.where(hl == h, jnp.broadcast_to(cols[:, h:h + 1], (r, C_WIDTH)), out)
    return out


def _split3(a):
    a1 = a.astype(MXU)
    r1 = a - a1.astype(F32)
    a2 = r1.astype(MXU)
    return a1, a2, (r1 - a2.astype(F32)).astype(MXU)


def _sel_mm(sel, a):
    d = lambda x: jnp.dot(sel, x, preferred_element_type=F32)
    a1, a2, a3 = _split3(a)
    return d(a1) + (d(a2) + d(a3))


def _mm_sel(a, sel):
    d = lambda x: jnp.dot(x, sel, preferred_element_type=F32)
    a1, a2, a3 = _split3(a)
    return d(a1) + (d(a2) + d(a3))


def _same_head():
    return (_head_of_lane((C_WIDTH, C_WIDTH), 0) == _head_of_lane((C_WIDTH, C_WIDTH), 1)).astype(MXU)


def _gdn_prep(qkv_ref, gate_ref, n, off, alog_ref, dtb_ref, q_s, k_s, v_s, bf_s, bb_s, gf_s, gb_s):
    rows = GDN_BLOCK
    nblk = n // rows
    ri, rj = _iota((rows, rows), 0), _iota((rows, rows), 1)
    same_chunk = (ri // C_CHUNK) == (rj // C_CHUNK)
    csum_f = (same_chunk & (rj <= ri)).astype(MXU)
    csum_b = (same_chunk & (rj >= ri)).astype(MXU)

    def block(i, carry):
        base = pl.multiple_of(i * rows, rows)
        dst = pl.ds(pl.multiple_of(off + base, rows), rows)
        q_s[dst, :] = qkv_ref[0, pl.ds(base, rows), 0:C_WIDTH]
        k_s[dst, :] = qkv_ref[0, pl.ds(base, rows), C_WIDTH:2 * C_WIDTH]
        v_s[dst, :] = qkv_ref[0, pl.ds(base, rows), 2 * C_WIDTH:]
        gt = gate_ref[0, pl.ds(base, rows), :]
        h = C_HEADS
        beta = _sigmoid(gt)
        bf_s[dst, :] = _expand_heads(beta[:, 0:h])
        bb_s[dst, :] = _expand_heads(beta[:, h:2 * h])
        sp = jnp.maximum(gt + dtb_ref[...], 0.0) + jnp.log1p(jnp.exp(-jnp.abs(gt + dtb_ref[...])))
        g = -jnp.exp(alog_ref[...]) * sp
        gf_s[dst, :] = _expand_heads(_sel_mm(csum_f, g)[:, 2 * h:3 * h])
        gb_s[dst, :] = _expand_heads(_sel_mm(csum_b, g)[:, 3 * h:4 * h])
        return carry

    lax.fori_loop(0, nblk, block, 0)


def _block_diag(a, bmask):
    return jnp.where(bmask, jnp.concatenate([a] * C_HEADS, axis=0), 0.0)


def _gdn_local(chains, q_s, k_s, v_s, bufs, buf):
    c, wd = C_CHUNK, C_WIDTH
    u_buf, w_buf, a_buf, qd_buf, kd_buf, eg_buf = bufs
    ii = _iota((c, wd), 0)
    jj = _iota((c, wd), 1) % c
    eye = jj == ii
    bmask = _head_of_lane((C_HEADS * c, wd), 1) == (_iota((C_HEADS * c, wd), 0) // c)
    bd = lambda a: _block_diag(a, bmask)
    ident = jnp.where(eye, 1.0, 0.0)

    st = []
    for rows, backward, beta_s, gc_s, _, _ in chains:
        q, k, v, beta, gc = q_s[rows, :], k_s[rows, :], v_s[rows, :], beta_s[rows, :], gc_s[rows, :]
        incl = (jj >= ii) if backward else (jj <= ii)
        strict = (jj > ii) if backward else (jj < ii)
        grow = jnp.sum(jnp.where(eye, gc, 0.0), axis=0, keepdims=True)
        decay = jnp.exp(jnp.where(incl, gc - grow, NEG_BIG))
        glast = gc[0:1] if backward else gc[c - 1:c]
        st.append(dict(q=q, k=k, v=v, beta=beta, gc=gc, decay=decay, strict=strict, glast=glast, kb=k * beta,
                       egc=jnp.exp(gc)))
    yield
    for s in st:
        qk = _mm_nt(jnp.concatenate([s["kb"], s["q"]], axis=0), bd(s["k"]))
        s["lmat"] = jnp.where(s["strict"], qk[:c] * s["decay"], 0.0)
        s["attn"] = qk[c:] * s["decay"]
    yield
    in_blk = (ii // INV_BLOCK) == (jj // INV_BLOCK)
    for s in st:
        dmat = jnp.where(in_blk, s["lmat"], 0.0)
        s["t"] = ident - dmat
        s["pw"] = _mm(dmat, bd(dmat))
    yield
    nsq = int(math.log2(INV_BLOCK)) - 1
    for step in range(nsq):
        for s in st:
            b = bd(s["pw"])
            if step < nsq - 1:
                r = _mm(jnp.concatenate([s["t"], s["pw"]], axis=0), b)
                s["t"], s["pw"] = s["t"] + r[:c], r[c:]
            else:
                s["t"] = s["t"] + _mm(s["t"], b)
        yield
    size = INV_BLOCK
    while size < c:
        emask = ((ii // (2 * size)) == (jj // (2 * size))) & ((ii // size) != (jj // size))
        for s in st:
            s["y"] = _mm(jnp.where(emask, s["lmat"], 0.0), bd(s["t"]))
        yield
        for s in st:
            s["t"] = s["t"] - _mm(s["t"], bd(s["y"]))
        yield
        size *= 2
    for s, (_, _, _, _, d, slot) in zip(st, chains):
        dst = slice(slot * c, (slot + 1) * c)
        u_buf[buf, d, dst, :] = _mm(s["t"], bd(s["v"] * s["beta"]))
        a_buf[buf, d, dst, :] = s["attn"].astype(MXU)
        qd_buf[buf, d, dst, :] = (s["q"] * s["egc"]).astype(MXU)
    yield
    for s, (_, _, _, _, d, slot) in zip(st, chains):
        dst = slice(slot * c, (slot + 1) * c)
        w_buf[buf, d, dst, :] = _mm(s["t"], bd(s["kb"] * s["egc"])).astype(MXU)
        kd_buf[buf, d, dst, :] = (s["k"] * jnp.exp(s["glast"] - s["gc"])).astype(MXU)
        eg_buf[buf, d, slot * HALO:(slot + 1) * HALO, :] = jnp.broadcast_to(jnp.exp(s["glast"]), (HALO, wd))


def _gdn_scan(steps, rows, st_s, o_refs, bufs, buf):
    c, wd = C_CHUNK, C_WIDTH
    u_buf, w_buf, a_buf, qd_buf, kd_buf, eg_buf = bufs
    bmask = _head_of_lane((C_HEADS * c, wd), 1) == (_iota((C_HEADS * c, wd), 0) // c)
    smask = _head_of_lane((wd, wd), 0) == _head_of_lane((wd, wd), 1)
    states = [st_s[0], st_s[1]]
    for slots in steps:
        src = [slice(slot * c, (slot + 1) * c) for slot in slots]
        ws = [_mm(jnp.concatenate([w_buf[buf, d, r, :], qd_buf[buf, d, r, :]], axis=0), states[d])
              for d, r in enumerate(src)]
        yield
        v_new = [u_buf[buf, d, r, :] - ws[d][:c] for d, r in enumerate(src)]
        upd = [_mm_tn(kd_buf[buf, d, r, :], v_new[d]) for d, r in enumerate(src)]
        for d, (r, slot) in enumerate(zip(src, slots)):
            o_refs[d][rows[d][slot], :] = ws[d][c:] + _mm(a_buf[buf, d, r, :], _block_diag(v_new[d], bmask))
        yield
        states = [states[d] * eg_buf[buf, d, slot * HALO:slot * HALO + 1, :] + jnp.where(smask, upd[d], 0.0)
                  for d, slot in enumerate(slots)]
    st_s[0], st_s[1] = states


def _gdn_kernel(*refs, n_ctx, n_lat, ctx_out):
    (rawc_ref, gtc_ref, zc_ref, rawl_ref, gtl_ref, zl_ref, alog_ref, dtb_ref, nw_ref) = refs[:9]
    nout = 2 if ctx_out else 1
    out_refs = refs[9:9 + nout]
    (q_s, k_s, v_s, bf_s, bb_s, gf_s, gb_s, of_s, ob_s, st_s) = refs[9 + nout:19 + nout]
    scr = (q_s, k_s, v_s, bf_s, bb_s, gf_s, gb_s)
    bufs = refs[19 + nout:]
    _gdn_prep(rawc_ref, gtc_ref, n_ctx, 0, alog_ref, dtb_ref, *scr)
    _gdn_prep(rawl_ref, gtl_ref, n_lat, n_ctx, alog_ref, dtb_ref, *scr)
    st_s[...] = jnp.zeros_like(st_s)
    ncb, nlb = n_ctx // GDN_BLOCK, n_lat // GDN_BLOCK
    per = GDN_BLOCK // C_CHUNK

    nblk = ncb + nlb
    steps = [(t, per - 1 - t) for t in range(per)]

    def block_rows(i):
        i = jnp.asarray(i, jnp.int32)
        bf = pl.multiple_of(i * GDN_BLOCK, GDN_BLOCK)
        bb = pl.multiple_of(jnp.where(i < ncb, ncb - 1 - i, 2 * ncb + nlb - 1 - i) * GDN_BLOCK, GDN_BLOCK)
        return ([pl.ds(bf + r * C_CHUNK, C_CHUNK) for r in range(per)],
                [pl.ds(bb + r * C_CHUNK, C_CHUNK) for r in range(per)])

    def local(i):
        rows_f, rows_b = block_rows(i)
        chains = ([(rows_f[r], False, bf_s, gf_s, 0, r) for r in range(per)]
                  + [(rows_b[r], True, bb_s, gb_s, 1, r) for r in range(per)])
        return _gdn_local(chains, q_s, k_s, v_s, bufs, i % 2)

    def scan(i):
        return _gdn_scan(steps, block_rows(i), st_s, (of_s, ob_s), bufs, i % 2)

    _interleave([local(0)])

    def block(i, carry):
        _interleave([local(i + 1), scan(i)])
        return carry

    lax.fori_loop(0, nblk - 1, block, 0)
    _interleave([scan(nblk - 1)])

    same_head = _same_head()

    def finish(z_ref, out_ref, off, n):
        rows = GDN_BLOCK

        def blk(i, carry):
            base = pl.multiple_of(i * rows, rows)
            src = pl.ds(pl.multiple_of(off + base, rows), rows)
            o = of_s[src, :] + ob_s[src, :]
            ms = _mm_sel(o * o, same_head) * (1.0 / C_HEAD_DIM)
            y = o * lax.rsqrt(ms + NORM_EPS) * nw_ref[...]
            out_ref[0, pl.ds(base, rows), :] = (y * _silu(z_ref[0, pl.ds(base, rows), :])).astype(out_ref.dtype)
            return carry

        lax.fori_loop(0, n // rows, blk, 0)

    finish(zl_ref, out_refs[0], n_ctx, n_lat)
    if ctx_out:
        finish(zc_ref, out_refs[1], 0, n_ctx)


def _gdn(rawc, gtc, zc, rawl, gtl, zl, alog, dtb, nw, *, ctx_out):
    bsz, n_ctx, _ = rawc.shape
    n_lat = rawl.shape[1]
    nt = n_ctx + n_lat
    seq = lambda a: pl.BlockSpec((1,) + a.shape[1:], lambda b: (b, 0, 0))
    const = lambda a: pl.BlockSpec(a.shape, lambda b: (0,) * a.ndim)
    out_shape = [jax.ShapeDtypeStruct((bsz, n_lat, C_WIDTH), MXU)]
    if ctx_out:
        out_shape.append(jax.ShapeDtypeStruct((bsz, n_ctx, C_WIDTH), MXU))
    big = lambda: pltpu.VMEM((nt, C_WIDTH), F32)
    loc = lambda dt: pltpu.VMEM((2, 2, GDN_BLOCK, C_WIDTH), dt)
    per = GDN_BLOCK // C_CHUNK
    res = pl.pallas_call(
        functools.partial(_gdn_kernel, n_ctx=n_ctx, n_lat=n_lat, ctx_out=ctx_out),
        grid=(bsz,),
        in_specs=[seq(rawc), seq(gtc), seq(zc), seq(rawl), seq(gtl), seq(zl),
                  const(alog), const(dtb), const(nw)],
        out_specs=[seq(s) for s in out_shape],
        out_shape=out_shape,
        scratch_shapes=[big() for _ in range(9)] + [pltpu.VMEM((2, C_WIDTH, C_WIDTH), F32)]
        + [loc(F32), loc(MXU), loc(MXU), loc(MXU), loc(MXU), pltpu.VMEM((2, 2, per * HALO, C_WIDTH), F32)],
        compiler_params=pltpu.CompilerParams(dimension_semantics=("arbitrary",), vmem_limit_bytes=VMEM_LIMIT),
        name="gated_deltanet_bidir",
    )(rawc, gtc, zc, rawl, gtl, zl, alog, dtb, nw)
    return res


def _out_kernel(x_ref, mod_ref, ya_ref, yb_ref, yc_ref, w_ref, fw_ref, o_ref, *, final):
    d = D_MODEL
    y = (jnp.dot(ya_ref[0], w_ref[0:A_WIDTH], preferred_element_type=F32)
         + jnp.dot(yb_ref[0], w_ref[A_WIDTH:A_WIDTH + B_WIDTH], preferred_element_type=F32)
         + jnp.dot(yc_ref[0], w_ref[A_WIDTH + B_WIDTH:], preferred_element_type=F32))
    x = x_ref[0] + mod_ref[0][:, 2 * d:] * y
    if final:
        x = x * lax.rsqrt(jnp.mean(x * x, axis=-1, keepdims=True) + NORM_EPS) * fw_ref[...]
    o_ref[0] = x


def _out(x, mod, ya, yb, yc, w_out, fw, *, final, tile):
    bsz, n, d = x.shape
    mod_map = (lambda b, i: (b, 0, 0)) if mod.shape[0] == bsz else (lambda b, i: (0, 0, 0))
    tok = lambda w: pl.BlockSpec((1, tile, w), lambda b, i: (b, i, 0))
    const = lambda a: pl.BlockSpec(a.shape, lambda b, i: (0,) * a.ndim)
    return pl.pallas_call(
        functools.partial(_out_kernel, final=final),
        grid=(bsz, n // tile),
        in_specs=[tok(d), pl.BlockSpec((1, 1, 3 * d), mod_map), tok(A_WIDTH), tok(B_WIDTH), tok(C_WIDTH),
                  const(w_out), const(fw)],
        out_specs=tok(d),
        out_shape=jax.ShapeDtypeStruct((bsz, n, d), F32),
        compiler_params=pltpu.CompilerParams(dimension_semantics=("arbitrary", "arbitrary"),
                                             vmem_limit_bytes=VMEM_LIMIT),
        name="outproj_residual_final" if final else "outproj_residual",
    )(x, mod, ya, yb, yc, w_out, fw)


def _rope_tables(n):
    pos = jnp.arange(n)
    half = B_HEAD_DIM // 2
    inv_freq = ROPE_THETA ** (-jnp.arange(0, half, 2, dtype=F32) / half)
    lane = np.arange(B_WIDTH)
    dd = lane % B_HEAD_DIM
    use_col = (dd // half) == 1
    second = ((dd % half) // (half // 2)) == 1
    fidx = dd % (half // 2)
    p = jnp.where(use_col[None, :], (pos % GRID_W)[:, None], (pos // GRID_W)[:, None]).astype(F32)
    ang = p * inv_freq[fidx][None, :]
    cos, sin = jnp.cos(ang), jnp.sin(ang)
    return cos, jnp.where(second[None, :], 0.0, -sin), jnp.where(second[None, :], sin, 0.0)


def kernel(x, c, ctx, c_ctx, w_ada, b_ada, norm_w, w_in, w_out, a_ln_w, a_ws, a_bs, b_lam, b_subln_w,
           c_conv_w, c_a_log, c_dt_bias, c_norm_w, final_norm_w):
    bsz, n, d = x.shape
    n_ctx = ctx.shape[1]
    depth = w_ada.shape[0]
    assert d == D_MODEL and n % GDN_BLOCK == 0 and n_ctx % GDN_BLOCK == 0 and n % GRID_W == 0
    tile = ATT_QTILE * ATT_TILES
    tile_ctx = min(ATT_QTILE, n_ctx)
    ptile = min(PROJ_TILE, n)

    rows = bsz + 1
    rows_pad = -(-rows // 8) * 8
    c_rows = jnp.concatenate([c, c_ctx[None, :], jnp.zeros((rows_pad - rows, d), F32)], axis=0)
    mods = _ada(c_rows, w_ada, b_ada)

    tabs = _rope_tables(n)
    tabs_ctx = tuple(t[:tile_ctx] for t in tabs)
    gate_pad = jnp.zeros((depth, d, GATE_PAD - N_GATE), w_in.dtype)
    w_in_p = jnp.concatenate([w_in, gate_pad], axis=-1).astype(MXU)
    w_out_b = w_out.astype(MXU)
    aws = a_ws.reshape(depth, A_GROUPS * A_CHUNK, A_CHUNK).astype(MXU)
    abias = jnp.repeat(jnp.swapaxes(a_bs, 1, 2), A_WIDTH // A_GROUPS, axis=2)
    row = lambda a: a.reshape(1, -1)

    xc = ctx
    for i in range(depth):
        lam_init = 0.8 - 0.6 * math.exp(-0.3 * i)
        ctx_out = i < depth - 1
        mod_l = mods[i, :bsz][:, None, :]
        mod_c = mods[i, bsz:bsz + 1][:, None, :]
        common = (row(norm_w[i]), w_in_p[i])
        a_par = (row(a_ln_w[i]), aws[i], abias[i], c_conv_w[i])
        ya, qb, kb, vb, zb, craw, zc, gt = _proj(x, mod_l, *common, tabs, *a_par, rope=True, tile=ptile)
        ya_c, qb_c, kb_c, vb_c, zb_c, craw_c, zc_c, gt_c = _proj(xc, mod_c, *common, tabs_ctx, *a_par,
                                                                 rope=False, tile=tile_ctx)
        lam_p, sw = b_lam[i], row(b_subln_w[i])
        yb = _attn(qb, zb, lam_p, sw, [kb_c, kb], [vb_c, vb], lam_init=lam_init, tile=tile)
        gate_row = lambda p: jnp.concatenate([jnp.zeros((2 * C_HEADS,), F32), p.reshape(-1),
                                              jnp.zeros((GATE_PAD - N_GATE,), F32)])[None, :]
        gdn_par = (gate_row(c_a_log[i]), gate_row(c_dt_bias[i]), jnp.tile(c_norm_w[i], C_HEADS)[None, :])
        yc_all = _gdn(craw_c, gt_c, zc_c, craw, gt, zc, *gdn_par, ctx_out=ctx_out)
        final = i == depth - 1
        x = _out(x, mod_l, ya, yb, yc_all[0], w_out_b[i], row(final_norm_w), final=final, tile=ptile)
        if ctx_out:
            yb_c = _attn(qb_c, zb_c, lam_p, sw, [kb_c], [vb_c], lam_init=lam_init, tile=tile_ctx)
            xc = _out(xc, mod_c, ya_c, yb_c, yc_all[1], w_out_b[i], row(final_norm_w), final=False, tile=tile_ctx)
    return x
```

```python
import functools
import math

import numpy as np
import jax
import jax.numpy as jnp
from jax import lax
from jax.experimental import pallas as pl
from jax.experimental.pallas import tpu as pltpu

F32 = jnp.float32
MXU = jnp.bfloat16

D_MODEL = 1024
NORM_EPS = 1e-6
GRID_W = 64
A_WIDTH, A_GROUPS, A_CHUNK, A_LN_EPS = 256, 4, 128, 1e-5
B_HEADS, B_HEAD_DIM, B_WIDTH = 4, 64, 512
ROPE_THETA = 10000.0
C_HEADS, C_HEAD_DIM, C_WIDTH, C_CONV, C_CHUNK = 4, 64, 256, 5, 64
N_GATE = 4 * C_HEADS
GATE_PAD = 128
OFF_A, OFF_B, OFF_C, OFF_G = 0, 3 * A_WIDTH, 3 * A_WIDTH + 4 * B_WIDTH, 3 * A_WIDTH + 4 * B_WIDTH + 4 * C_WIDTH
D_IN_PAD = OFF_G + GATE_PAD
NEG_BIG = -1e30
LOG2E = math.log2(math.e)
ATT_KCHUNK = 256
ATT_SCHUNK = 1024
ATT_QTILE = 256
ATT_TILES = 1
PROJ_TILE = 512
OUT_TILE = 1024
CONV_ROWS = 256
VMEM_LIMIT = 56 * 1024 * 1024
GDN_BLOCK = 256
HALO = 8
ADA_TILE = 512
INV_BLOCK = 16


def _mm(a, b):
    return jnp.dot(a.astype(MXU), b.astype(MXU), preferred_element_type=F32)


def _mm_nt(a, b):
    return lax.dot_general(a.astype(MXU), b.astype(MXU), (((1,), (1,)), ((), ())), preferred_element_type=F32)


def _mm_tn(a, b):
    return lax.dot_general(a.astype(MXU), b.astype(MXU), (((0,), (0,)), ((), ())), preferred_element_type=F32)


def _split(a):
    hi = a.astype(MXU)
    return hi, (a - hi.astype(F32)).astype(MXU)


def _mm3(a, b):
    ah, al = _split(a)
    bh, bl = _split(b)
    d = lambda x, y: jnp.dot(x, y, preferred_element_type=F32)
    return d(ah, bh) + (d(ah, bl) + d(al, bh))


def _sigmoid(x):
    return 1.0 / (1.0 + jnp.exp(-x))


def _silu(x):
    return x * _sigmoid(x)


def _iota(shape, dim):
    return lax.broadcasted_iota(jnp.int32, shape, dim)


def _ada_kernel(c_ref, w_ref, b_ref, o_ref):
    o_ref[0] = _mm3(_silu(c_ref[...]), w_ref[0]) + b_ref[0]


def _ada(c_rows, w_ada, b_ada):
    depth, d, d3 = w_ada.shape
    r = c_rows.shape[0]
    tn = ADA_TILE
    return pl.pallas_call(
        _ada_kernel,
        grid=(depth, d3 // tn),
        in_specs=[pl.BlockSpec((r, d), lambda l, j: (0, 0)),
                  pl.BlockSpec((1, d, tn), lambda l, j: (l, 0, j)),
                  pl.BlockSpec((1, 1, tn), lambda l, j: (l, 0, j))],
        out_specs=pl.BlockSpec((1, r, tn), lambda l, j: (l, 0, j)),
        out_shape=jax.ShapeDtypeStruct((depth, r, d3), F32),
        name="ada_modulation",
    )(c_rows, w_ada, b_ada.reshape(depth, 1, d3))


def _proj_kernel(x_ref, xp_ref, xn_ref, mod_ref, nw_ref, w_ref, cos_ref, sinn_ref, sinp_ref, lnw_ref, aws_ref,
                 abias_ref, convw_ref, ya_ref, q_ref, k_ref, v_ref, zb_ref, cqkv_ref, zc_ref, g_ref, *, rope):
    d = D_MODEL
    mod = mod_ref[0]

    def normed(x):
        y = x * lax.rsqrt(jnp.mean(x * x, axis=-1, keepdims=True) + NORM_EPS) * nw_ref[...]
        return (y * (1.0 + mod[:, d:2 * d]) + mod[:, :d]).astype(MXU)

    t = x_ref.shape[1]
    half_t = t // 2
    proj = lambda lhs, lo, hi: jnp.dot(lhs, w_ref[:, lo:hi], preferred_element_type=F32)
    vals = {}
    h0 = normed(x_ref[0, :half_t])
    pc0 = proj(jnp.concatenate([h0, normed(xp_ref[0])], axis=0), OFF_C, OFF_G)
    h1 = normed(x_ref[0, half_t:])
    pc1 = proj(jnp.concatenate([h1, normed(xn_ref[0])], axis=0), OFF_C, OFF_G)
    h = jnp.concatenate([h0, h1], axis=0)

    def finish_a():
        pa = vals.pop("a")
        u, v, z = pa[:, :A_WIDTH], pa[:, A_WIDTH:2 * A_WIDTH], pa[:, 2 * A_WIDTH:]
        vc = v - jnp.mean(v, axis=-1, keepdims=True)
        vn = (vc * lax.rsqrt(jnp.mean(vc * vc, axis=-1, keepdims=True) + A_LN_EPS) * lnw_ref[...]).astype(MXU)
        grp = _iota((A_CHUNK, A_WIDTH), 1) // (A_WIDTH // A_GROUPS)
        for c in range(t // A_CHUNK):
            rows = slice(c * A_CHUNK, (c + 1) * A_CHUNK)
            r = jnp.dot(aws_ref[...], vn[rows], preferred_element_type=F32)
            s = abias_ref[...]
            for g in range(A_GROUPS):
                s = s + jnp.where(grp == g, r[g * A_CHUNK:(g + 1) * A_CHUNK], 0.0)
            ya_ref[0, rows, :] = (u[rows] * s * _silu(z[rows])).astype(ya_ref.dtype)

    def rot(a):
        if not rope:
            return a
        half = B_HEAD_DIM // 4
        return (a * cos_ref[...] + pltpu.roll(a, B_WIDTH - half, 1) * sinn_ref[...]
                + pltpu.roll(a, half, 1) * sinp_ref[...])

    def finish_q():
        q_ref[0] = (rot(vals.pop("q")) * (B_HEAD_DIM ** -0.5 * LOG2E)).astype(q_ref.dtype)

    def finish_k():
        k_ref[0] = rot(vals.pop("k")).astype(k_ref.dtype)

    def finish_v():
        v_ref[0] = vals.pop("v").T.astype(v_ref.dtype)

    def finish_z():
        zb_ref[0] = vals.pop("z").astype(zb_ref.dtype)

    sub = min(t, CONV_ROWS)

    def start_c():
        i, last = pl.program_id(0), pl.num_programs(0) - 1
        zc_ref[0, :half_t, :] = pc0[:half_t, 3 * C_WIDTH:]
        zc_ref[0, half_t:, :] = pc1[:half_t, 3 * C_WIDTH:]
        prev = jnp.where(i == 0, 0.0, pc0[half_t:, :3 * C_WIDTH])
        nxt = jnp.where(i == last, 0.0, pc1[half_t:, :3 * C_WIDTH])
        vals["c"] = jnp.concatenate([prev, pc0[:half_t, :3 * C_WIDTH], pc1[:half_t, :3 * C_WIDTH], nxt], axis=0)

    def finish_c(r0):
        w = convw_ref[...]
        pad = C_CONV // 2
        win = vals["c"][r0:r0 + sub + 2 * HALO]
        acc = win[HALO:HALO + sub] * w[pad:pad + 1]
        for j in range(C_CONV):
            if j != pad:
                acc = acc + pltpu.roll(win, (pad - j) % (sub + 2 * HALO), 0)[HALO:HALO + sub] * w[j:j + 1]
        act = _silu(acc)
        qh, kh = act[:, :C_WIDTH], act[:, C_WIDTH:2 * C_WIDTH]
        l2 = lambda a: a * lax.rsqrt(_mm_sel(a * a, _same_head()) + 1e-6)
        cqkv_ref[0, r0:r0 + sub, 0:C_WIDTH] = l2(qh) * (C_HEAD_DIM ** -0.5)
        cqkv_ref[0, r0:r0 + sub, C_WIDTH:2 * C_WIDTH] = l2(kh)
        cqkv_ref[0, r0:r0 + sub, 2 * C_WIDTH:] = act[:, 2 * C_WIDTH:]

    def start(name, lo, hi):
        def go():
            vals[name] = proj(h, lo, hi)
        return go

    def finish_g():
        g_ref[0] = vals.pop("g")

    bq = OFF_B
    order = [start_c, start("a", OFF_A, OFF_B)]
    convs = [functools.partial(finish_c, r0) for r0 in range(0, t, sub)]
    later = [start("q", bq, bq + B_WIDTH), start("k", bq + B_WIDTH, bq + 2 * B_WIDTH),
             start("v", bq + 2 * B_WIDTH, bq + 3 * B_WIDTH), start("z", bq + 3 * B_WIDTH, OFF_C),
             start("g", OFF_G, D_IN_PAD)]
    fins = convs + [finish_a, finish_q, finish_k, finish_v, finish_z, finish_g]
    while later or fins:
        if later:
            order.append(later.pop(0))
        if fins:
            order.append(fins.pop(0))
    for step in order:
        step()


def _proj(x, mod, nw, w_in, tabs, lnw, aws, abias, convw, *, rope, tile):
    bsz, n, d = x.shape
    cos, sinn, sinp = tabs
    mod_map = (lambda i, b: (b, 0, 0)) if mod.shape[0] == bsz else (lambda i, b: (0, 0, 0))
    tok = lambda w: pl.BlockSpec((1, tile, w), lambda i, b: (b, i, 0))
    per = tile // HALO
    before = pl.BlockSpec((1, HALO, d), lambda i, b: (b, jnp.maximum(i * per - 1, 0), 0))
    after = pl.BlockSpec((1, HALO, d), lambda i, b: (b, jnp.minimum((i + 1) * per, n // HALO - 1), 0))
    const = lambda a: pl.BlockSpec(a.shape, lambda i, b: (0,) * a.ndim)
    tab = pl.BlockSpec((tile, B_WIDTH), lambda i, b: (i, 0))
    outs = [(A_WIDTH, MXU), (B_WIDTH, MXU), (B_WIDTH, MXU), (B_WIDTH, MXU), (B_WIDTH, F32),
            (3 * C_WIDTH, F32), (C_WIDTH, F32), (GATE_PAD, F32)]
    out_specs = [tok(w) for w, _ in outs]
    out_shape = [jax.ShapeDtypeStruct((bsz, n, w), dt) for w, dt in outs]
    out_specs[3] = pl.BlockSpec((1, B_WIDTH, tile), lambda i, b: (b, 0, i))
    out_shape[3] = jax.ShapeDtypeStruct((bsz, B_WIDTH, n), MXU)
    return pl.pallas_call(
        functools.partial(_proj_kernel, rope=rope),
        grid=(n // tile, bsz),
        in_specs=[tok(d), before, after, pl.BlockSpec((1, 1, 3 * d), mod_map), const(nw), const(w_in), tab, tab, tab,
                  const(lnw), const(aws), const(abias), const(convw)],
        out_specs=out_specs,
        out_shape=out_shape,
        compiler_params=pltpu.CompilerParams(dimension_semantics=("arbitrary", "arbitrary"),
                                             vmem_limit_bytes=VMEM_LIMIT),
        name="norm_inproj_rope" if rope else "norm_inproj_ctx",
    )(x, x, x, mod, nw, w_in, cos, sinn, sinp, lnw, aws, abias, convw)


def _interleave(gens):
    gens = list(gens)
    while gens:
        for g in list(gens):
            if next(g, StopIteration) is StopIteration:
                gens.remove(g)


def _attn_kernel(*refs, seg_lens, lam_init):
    nseg = len(seg_lens)
    q_ref, z_ref, lam_ref, sw_ref = refs[:4]
    k_refs, v_refs = refs[4:4 + nseg], refs[4 + nseg:4 + 2 * nseg]
    o_ref, s_buf, e_buf = refs[4 + 2 * nseg:]
    lf = lam_ref[...]
    lam = (jnp.exp(jnp.sum(lf[0:1] * lf[1:2], axis=-1, keepdims=True))
           - jnp.exp(jnp.sum(lf[2:3] * lf[3:4], axis=-1, keepdims=True)) + lam_init)
    tq = ATT_QTILE
    ntile = q_ref.shape[1] // tq
    pair = 2 * B_HEAD_DIM
    first = _iota((1, pair), 1) < B_HEAD_DIM
    zero = jnp.zeros((), q_ref.dtype)
    chunks, row = [], 0
    for seg, n in enumerate(seg_lens):
        for c0 in range(0, n, ATT_KCHUNK):
            chunks.append((seg, c0, row))
            row += ATT_KCHUNK
    kc = ATT_KCHUNK
    maxes, heads = {}, {}

    dens = {}

    def scores(i):
        hh, j = divmod(i, 2)
        qt, hp = divmod(hh, B_HEADS)
        lanes = slice(hp * pair, (hp + 1) * pair)
        qm = jnp.where(first if j == 0 else jnp.logical_not(first), q_ref[0, qt * tq:(qt + 1) * tq, lanes], zero)
        m, row = None, 0
        for seg, n in enumerate(seg_lens):
            for c0 in range(0, n, ATT_SCHUNK):
                size = min(ATT_SCHUNK, n - c0)
                s = _mm_nt(k_refs[seg][0, c0:c0 + size, lanes], qm)
                for r in range(0, size, kc):
                    s_buf[hh % 2, j, row:row + kc, :] = s[r:r + kc]
                    cm = jnp.max(s[r:r + kc], axis=0, keepdims=True)
                    m = cm if m is None else jnp.maximum(m, cm)
                    row += kc
                    yield
        maxes[i] = m

    def exps(i):
        hh, j = divmod(i, 2)
        den = None
        for _, _, r0 in chunks:
            e = jnp.exp2(s_buf[hh % 2, j, r0:r0 + kc, :] - maxes[i])
            e_buf[hh % 2, j, r0:r0 + kc, :] = e.astype(MXU)
            cs = jnp.sum(e, axis=0, keepdims=True)
            den = cs if den is None else den + cs
            yield
        dens[i] = den

    def values(hh):
        qt, hp = divmod(hh, B_HEADS)
        lanes = slice(hp * pair, (hp + 1) * pair)
        d0, d1 = dens[2 * hh], dens[2 * hh + 1]
        coef = (lam * d0 * (1.0 / d1)).astype(MXU)
        pv = None
        for seg, c0, r0 in chunks:
            p = e_buf[hh % 2, 0, r0:r0 + kc, :] - coef * e_buf[hh % 2, 1, r0:r0 + kc, :]
            t = _mm(v_refs[seg][0, lanes, c0:c0 + kc], p)
            pv = t if pv is None else pv + t
            yield
        acc = (pv * (1.0 / d0)).T
        o = acc * lax.rsqrt(jnp.mean(acc * acc, axis=-1, keepdims=True) + NORM_EPS) * sw_ref[...]
        heads[hh] = o * (1.0 - lam_init)
        if hp == B_HEADS - 1:
            rows = slice(qt * tq, (qt + 1) * tq)
            o_all = jnp.concatenate([heads.pop(qt * B_HEADS + p) for p in range(B_HEADS)], axis=-1)
            o_ref[0, rows, :] = (o_all * _silu(z_ref[0, rows, :])).astype(o_ref.dtype)

    nh = 2 * B_HEADS * ntile
    for t in range(nh + 2):
        stage = []
        if t < nh:
            stage.append(scores(t))
        if 1 <= t <= nh:
            stage.append(exps(t - 1))
        if t >= 3 and t % 2 == 1:
            stage.append(values((t - 3) // 2))
        _interleave(stage)


def _attn(q, z, lam_p, subln_w, ks, vs, *, lam_init, tile):
    bsz, n, w = q.shape
    seg_lens = tuple(a.shape[1] for a in ks)
    tok = pl.BlockSpec((1, tile, w), lambda b, i: (b, i, 0))
    const = lambda a: pl.BlockSpec(a.shape, lambda b, i: (0,) * a.ndim)
    full = lambda a: pl.BlockSpec((1,) + a.shape[1:], lambda b, i: (b, 0, 0))
    return pl.pallas_call(
        functools.partial(_attn_kernel, seg_lens=seg_lens, lam_init=lam_init),
        grid=(bsz, n // tile),
        in_specs=[tok, tok, const(lam_p), const(subln_w)] + [full(a) for a in ks] + [full(a) for a in vs],
        out_specs=tok,
        out_shape=jax.ShapeDtypeStruct((bsz, n, w), MXU),
        scratch_shapes=[pltpu.VMEM((2, 2, sum(seg_lens), ATT_QTILE), F32),
                        pltpu.VMEM((2, 2, sum(seg_lens), ATT_QTILE), MXU)],
        compiler_params=pltpu.CompilerParams(dimension_semantics=("arbitrary", "arbitrary"),
                                             vmem_limit_bytes=VMEM_LIMIT),
        name="diff_attention_%dseg" % len(ks),
    )(q, z, lam_p, subln_w, *ks, *vs)


def _head_of_lane(shape, dim):
    return _iota(shape, dim) // C_HEAD_DIM


def _expand_heads(cols):
    r = cols.shape[0]
    hl = _head_of_lane((r, C_WIDTH), 1)
    out = jnp.broadcast_to(cols[:, C_HEADS - 1:C_HEADS], (r, C_WIDTH))
    for h in range(C_HEADS - 2, -1, -1):
        out = jnp.where(hl == h, jnp.broadcast_to(cols[:, h:h + 1], (r, C_WIDTH)), out)
    return out


def _split3(a):
    a1 = a.astype(MXU)
    r1 = a - a1.astype(F32)
    a2 = r1.astype(MXU)
    return a1, a2, (r1 - a2.astype(F32)).astype(MXU)


def _sel_mm(sel, a):
    d = lambda x: jnp.dot(sel, x, preferred_element_type=F32)
    a1, a2, a3 = _split3(a)
    return d(a1) + (d(a2) + d(a3))


def _mm_sel(a, sel):
    d = lambda x: jnp.dot(x, sel, preferred_element_type=F32)
    a1, a2, a3 = _split3(a)
    return d(a1) + (d(a2) + d(a3))


def _same_head():
    return (_head_of_lane((C_WIDTH, C_WIDTH), 0) == _head_of_lane((C_WIDTH, C_WIDTH), 1)).astype(MXU)


def _gdn_prep(qkv_ref, gate_ref, n, off, alog_ref, dtb_ref, q_s, k_s, v_s, bf_s, bb_s, gf_s, gb_s):
    rows = GDN_BLOCK
    nblk = n // rows
    ri, rj = _iota((rows, rows), 0), _iota((rows, rows), 1)
    same_chunk = (ri // C_CHUNK) == (rj // C_CHUNK)
    csum_f = (same_chunk & (rj <= ri)).astype(MXU)
    csum_b = (same_chunk & (rj >= ri)).astype(MXU)

    def block(i, carry):
        base = pl.multiple_of(i * rows, rows)
        dst = pl.ds(pl.multiple_of(off + base, rows), rows)
        q_s[dst, :] = qkv_ref[0, pl.ds(base, rows), 0:C_WIDTH]
        k_s[dst, :] = qkv_ref[0, pl.ds(base, rows), C_WIDTH:2 * C_WIDTH]
        v_s[dst, :] = qkv_ref[0, pl.ds(base, rows), 2 * C_WIDTH:]
        gt = gate_ref[0, pl.ds(base, rows), :]
        h = C_HEADS
        beta = _sigmoid(gt)
        bf_s[dst, :] = _expand_heads(beta[:, 0:h])
        bb_s[dst, :] = _expand_heads(beta[:, h:2 * h])
        sp = jnp.maximum(gt + dtb_ref[...], 0.0) + jnp.log1p(jnp.exp(-jnp.abs(gt + dtb_ref[...])))
        g = -jnp.exp(alog_ref[...]) * sp
        gf_s[dst, :] = _expand_heads(_sel_mm(csum_f, g)[:, 2 * h:3 * h])
        gb_s[dst, :] = _expand_heads(_sel_mm(csum_b, g)[:, 3 * h:4 * h])
        return carry

    lax.fori_loop(0, nblk, block, 0)


def _block_diag(a, bmask):
    return jnp.where(bmask, jnp.concatenate([a] * C_HEADS, axis=0), 0.0)


def _gdn_local(chains, q_s, k_s, v_s, bufs, buf):
    c, wd = C_CHUNK, C_WIDTH
    u_buf, w_buf, a_buf, qd_buf, kd_buf, eg_buf = bufs
    ii = _iota((c, wd), 0)
    jj = _iota((c, wd), 1) % c
    eye = jj == ii
    bmask = _head_of_lane((C_HEADS * c, wd), 1) == (_iota((C_HEADS * c, wd), 0) // c)
    bd = lambda a: _block_diag(a, bmask)
    ident = jnp.where(eye, 1.0, 0.0)

    st = []
    for rows, backward, beta_s, gc_s, _, _ in chains:
        q, k, v, beta, gc = q_s[rows, :], k_s[rows, :], v_s[rows, :], beta_s[rows, :], gc_s[rows, :]
        incl = (jj >= ii) if backward else (jj <= ii)
        strict = (jj > ii) if backward else (jj < ii)
        grow = jnp.sum(jnp.where(eye, gc, 0.0), axis=0, keepdims=True)
        decay = jnp.exp(jnp.where(incl, gc - grow, NEG_BIG))
        glast = gc[0:1] if backward else gc[c - 1:c]
        st.append(dict(q=q, k=k, v=v, beta=beta, gc=gc, decay=decay, strict=strict, glast=glast, kb=k * beta,
                       egc=jnp.exp(gc)))
    yield
    for s in st:
        qk = _mm_nt(jnp.concatenate([s["kb"], s["q"]], axis=0), bd(s["k"]))
        s["lmat"] = jnp.where(s["strict"], qk[:c] * s["decay"], 0.0)
        s["attn"] = qk[c:] * s["decay"]
    yield
    in_blk = (ii // INV_BLOCK) == (jj // INV_BLOCK)
    for s in st:
        dmat = jnp.where(in_blk, s["lmat"], 0.0)
        s["t"] = ident - dmat
        s["pw"] = _mm(dmat, bd(dmat))
    yield
    nsq = int(math.log2(INV_BLOCK)) - 1
    for step in range(nsq):
        for s in st:
            b = bd(s["pw"])
            if step < nsq - 1:
                r = _mm(jnp.concatenate([s["t"], s["pw"]], axis=0), b)
                s["t"], s["pw"] = s["t"] + r[:c], r[c:]
            else:
                s["t"] = s["t"] + _mm(s["t"], b)
        yield
    size = INV_BLOCK
    while size < c:
        emask = ((ii // (2 * size)) == (jj // (2 * size))) & ((ii // size) != (jj // size))
        for s in st:
            s["y"] = _mm(jnp.where(emask, s["lmat"], 0.0), bd(s["t"]))
        yield
        for s in st:
            s["t"] = s["t"] - _mm(s["t"], bd(s["y"]))
        yield
        size *= 2
    for s, (_, _, _, _, d, slot) in zip(st, chains):
        dst = slice(slot * c, (slot + 1) * c)
        u_buf[buf, d, dst, :] = _mm(s["t"], bd(s["v"] * s["beta"]))
        a_buf[buf, d, dst, :] = s["attn"].astype(MXU)
        qd_buf[buf, d, dst, :] = (s["q"] * s["egc"]).astype(MXU)
    yield
    for s, (_, _, _, _, d, slot) in zip(st, chains):
        dst = slice(slot * c, (slot + 1) * c)
        w_buf[buf, d, dst, :] = _mm(s["t"], bd(s["kb"] * s["egc"])).astype(MXU)
        kd_buf[buf, d, dst, :] = (s["k"] * jnp.exp(s["glast"] - s["gc"])).astype(MXU)
        eg_buf[buf, d, slot * HALO:(slot + 1) * HALO, :] = jnp.broadcast_to(jnp.exp(s["glast"]), (HALO, wd))


def _gdn_scan(steps, rows, st_s, o_refs, bufs, buf):
    c, wd = C_CHUNK, C_WIDTH
    u_buf, w_buf, a_buf, qd_buf, kd_buf, eg_buf = bufs
    bmask = _head_of_lane((C_HEADS * c, wd), 1) == (_iota((C_HEADS * c, wd), 0) // c)
    smask = _head_of_lane((wd, wd), 0) == _head_of_lane((wd, wd), 1)
    states = [st_s[0], st_s[1]]
    for slots in steps:
        src = [slice(slot * c, (slot + 1) * c) for slot in slots]
        ws = [_mm(jnp.concatenate([w_buf[buf, d, r, :], qd_buf[buf, d, r, :]], axis=0), states[d])
              for d, r in enumerate(src)]
        yield
        v_new = [u_buf[buf, d, r, :] - ws[d][:c] for d, r in enumerate(src)]
        upd = [_mm_tn(kd_buf[buf, d, r, :], v_new[d]) for d, r in enumerate(src)]
        for d, (r, slot) in enumerate(zip(src, slots)):
            o_refs[d][rows[d][slot], :] = ws[d][c:] + _mm(a_buf[buf, d, r, :], _block_diag(v_new[d], bmask))
        yield
        states = [states[d] * eg_buf[buf, d, slot * HALO:slot * HALO + 1, :] + jnp.where(smask, upd[d], 0.0)
                  for d, slot in enumerate(slots)]
    st_s[0], st_s[1] = states


def _gdn_kernel(*refs, n_ctx, n_lat, ctx_out):
    (rawc_ref, gtc_ref, zc_ref, rawl_ref, gtl_ref, zl_ref, alog_ref, dtb_ref, nw_ref) = refs[:9]
    nout = 2 if ctx_out else 1
    out_refs = refs[9:9 + nout]
    (q_s, k_s, v_s, bf_s, bb_s, gf_s, gb_s, of_s, ob_s, st_s) = refs[9 + nout:19 + nout]
    scr = (q_s, k_s, v_s, bf_s, bb_s, gf_s, gb_s)
    bufs = refs[19 + nout:]
    _gdn_prep(rawc_ref, gtc_ref, n_ctx, 0, alog_ref, dtb_ref, *scr)
    _gdn_prep(rawl_ref, gtl_ref, n_lat, n_ctx, alog_ref, dtb_ref, *scr)
    st_s[...] = jnp.zeros_like(st_s)
    ncb, nlb = n_ctx // GDN_BLOCK, n_lat // GDN_BLOCK
    per = GDN_BLOCK // C_CHUNK

    nblk = ncb + nlb
    steps = [(t, per - 1 - t) for t in range(per)]

    def block_rows(i):
        i = jnp.asarray(i, jnp.int32)
        bf = pl.multiple_of(i * GDN_BLOCK, GDN_BLOCK)
        bb = pl.multiple_of(jnp.where(i < ncb, ncb - 1 - i, 2 * ncb + nlb - 1 - i) * GDN_BLOCK, GDN_BLOCK)
        return ([pl.ds(bf + r * C_CHUNK, C_CHUNK) for r in range(per)],
                [pl.ds(bb + r * C_CHUNK, C_CHUNK) for r in range(per)])

    def local(i):
        rows_f, rows_b = block_rows(i)
        chains = ([(rows_f[r], False, bf_s, gf_s, 0, r) for r in range(per)]
                  + [(rows_b[r], True, bb_s, gb_s, 1, r) for r in range(per)])
        return _gdn_local(chains, q_s, k_s, v_s, bufs, i % 2)

    def scan(i):
        return _gdn_scan(steps, block_rows(i), st_s, (of_s, ob_s), bufs, i % 2)

    _interleave([local(0)])

    def block(i, carry):
        _interleave([local(i + 1), scan(i)])
        return carry

    lax.fori_loop(0, nblk - 1, block, 0)
    _interleave([scan(nblk - 1)])

    same_head = _same_head()

    def finish(z_ref, out_ref, off, n):
        rows = GDN_BLOCK

        def blk(i, carry):
            base = pl.multiple_of(i * rows, rows)
            src = pl.ds(pl.multiple_of(off + base, rows), rows)
            o = of_s[src, :] + ob_s[src, :]
            ms = _mm_sel(o * o, same_head) * (1.0 / C_HEAD_DIM)
            y = o * lax.rsqrt(ms + NORM_EPS) * nw_ref[...]
            out_ref[0, pl.ds(base, rows), :] = (y * _silu(z_ref[0, pl.ds(base, rows), :])).astype(out_ref.dtype)
            return carry

        lax.fori_loop(0, n // rows, blk, 0)

    finish(zl_ref, out_refs[0], n_ctx, n_lat)
    if ctx_out:
        finish(zc_ref, out_refs[1], 0, n_ctx)


def _gdn(rawc, gtc, zc, rawl, gtl, zl, alog, dtb, nw, *, ctx_out):
    bsz, n_ctx, _ = rawc.shape
    n_lat = rawl.shape[1]
    nt = n_ctx + n_lat
    seq = lambda a: pl.BlockSpec((1,) + a.shape[1:], lambda b: (b, 0, 0))
    const = lambda a: pl.BlockSpec(a.shape, lambda b: (0,) * a.ndim)
    out_shape = [jax.ShapeDtypeStruct((bsz, n_lat, C_WIDTH), MXU)]
    if ctx_out:
        out_shape.append(jax.ShapeDtypeStruct((bsz, n_ctx, C_WIDTH), MXU))
    big = lambda: pltpu.VMEM((nt, C_WIDTH), F32)
    loc = lambda dt: pltpu.VMEM((2, 2, GDN_BLOCK, C_WIDTH), dt)
    per = GDN_BLOCK // C_CHUNK
    res = pl.pallas_call(
        functools.partial(_gdn_kernel, n_ctx=n_ctx, n_lat=n_lat, ctx_out=ctx_out),
        grid=(bsz,),
        in_specs=[seq(rawc), seq(gtc), seq(zc), seq(rawl), seq(gtl), seq(zl),
                  const(alog), const(dtb), const(nw)],
        out_specs=[seq(s) for s in out_shape],
        out_shape=out_shape,
        scratch_shapes=[big() for _ in range(9)] + [pltpu.VMEM((2, C_WIDTH, C_WIDTH), F32)]
        + [loc(F32), loc(MXU), loc(MXU), loc(MXU), loc(MXU), pltpu.VMEM((2, 2, per * HALO, C_WIDTH), F32)],
        compiler_params=pltpu.CompilerParams(dimension_semantics=("arbitrary",), vmem_limit_bytes=VMEM_LIMIT),
        name="gated_deltanet_bidir",
    )(rawc, gtc, zc, rawl, gtl, zl, alog, dtb, nw)
    return res


def _out_kernel(x_ref, mod_ref, ya_ref, yb_ref, yc_ref, w_ref, fw_ref, o_ref, *, final):
    d = D_MODEL
    y = (jnp.dot(ya_ref[0], w_ref[0:A_WIDTH], preferred_element_type=F32)
         + jnp.dot(yb_ref[0], w_ref[A_WIDTH:A_WIDTH + B_WIDTH], preferred_element_type=F32)
         + jnp.dot(yc_ref[0], w_ref[A_WIDTH + B_WIDTH:], preferred_element_type=F32))
    x = x_ref[0] + mod_ref[0][:, 2 * d:] * y
    if final:
        x = x * lax.rsqrt(jnp.mean(x * x, axis=-1, keepdims=True) + NORM_EPS) * fw_ref[...]
    o_ref[0] = x


def _out(x, mod, ya, yb, yc, w_out, fw, *, final, tile):
    bsz, n, d = x.shape
    mod_map = (lambda b, i: (b, 0, 0)) if mod.shape[0] == bsz else (lambda b, i: (0, 0, 0))
    tok = lambda w: pl.BlockSpec((1, tile, w), lambda b, i: (b, i, 0))
    const = lambda a: pl.BlockSpec(a.shape, lambda b, i: (0,) * a.ndim)
    return pl.pallas_call(
        functools.partial(_out_kernel, final=final),
        grid=(bsz, n // tile),
        in_specs=[tok(d), pl.BlockSpec((1, 1, 3 * d), mod_map), tok(A_WIDTH), tok(B_WIDTH), tok(C_WIDTH),
                  const(w_out), const(fw)],
        out_specs=tok(d),
        out_shape=jax.ShapeDtypeStruct((bsz, n, d), F32),
        compiler_params=pltpu.CompilerParams(dimension_semantics=("arbitrary", "arbitrary"),
                                             vmem_limit_bytes=VMEM_LIMIT),
        name="outproj_residual_final" if final else "outproj_residual",
    )(x, mod, ya, yb, yc, w_out, fw)


def _rope_tables(n):
    pos = jnp.arange(n)
    half = B_HEAD_DIM // 2
    inv_freq = ROPE_THETA ** (-jnp.arange(0, half, 2, dtype=F32) / half)
    lane = np.arange(B_WIDTH)
    dd = lane % B_HEAD_DIM
    use_col = (dd // half) == 1
    second = ((dd % half) // (half // 2)) == 1
    fidx = dd % (half // 2)
    p = jnp.where(use_col[None, :], (pos % GRID_W)[:, None], (pos // GRID_W)[:, None]).astype(F32)
    ang = p * inv_freq[fidx][None, :]
    cos, sin = jnp.cos(ang), jnp.sin(ang)
    return cos, jnp.where(second[None, :], 0.0, -sin), jnp.where(second[None, :], sin, 0.0)


def kernel(x, c, ctx, c_ctx, w_ada, b_ada, norm_w, w_in, w_out, a_ln_w, a_ws, a_bs, b_lam, b_subln_w,
           c_conv_w, c_a_log, c_dt_bias, c_norm_w, final_norm_w):
    bsz, n, d = x.shape
    n_ctx = ctx.shape[1]
    depth = w_ada.shape[0]
    assert d == D_MODEL and n % GDN_BLOCK == 0 and n_ctx % GDN_BLOCK == 0 and n % GRID_W == 0
    tile = ATT_QTILE * ATT_TILES
    tile_ctx = min(ATT_QTILE, n_ctx)
    ptile = min(PROJ_TILE, n)
    otile = next(t for t in (OUT_TILE, PROJ_TILE, GDN_BLOCK) if n % t == 0)
    assert n % ptile == 0 and n % tile == 0

    rows = bsz + 1
    rows_pad = -(-rows // 8) * 8
    c_rows = jnp.concatenate([c, c_ctx[None, :], jnp.zeros((rows_pad - rows, d), F32)], axis=0)
    mods = _ada(c_rows, w_ada, b_ada)

    tabs = _rope_tables(n)
    tabs_ctx = tuple(t[:tile_ctx] for t in tabs)
    gate_pad = jnp.zeros((depth, d, GATE_PAD - N_GATE), w_in.dtype)
    w_in_p = jnp.concatenate([w_in, gate_pad], axis=-1).astype(MXU)
    w_out_b = w_out.astype(MXU)
    aws = a_ws.reshape(depth, A_GROUPS * A_CHUNK, A_CHUNK).astype(MXU)
    abias = jnp.repeat(jnp.swapaxes(a_bs, 1, 2), A_WIDTH // A_GROUPS, axis=2)
    row = lambda a: a.reshape(1, -1)

    xc = ctx
    for i in range(depth):
        lam_init = 0.8 - 0.6 * math.exp(-0.3 * i)
        ctx_out = i < depth - 1
        mod_l = mods[i, :bsz][:, None, :]
        mod_c = mods[i, bsz:bsz + 1][:, None, :]
        common = (row(norm_w[i]), w_in_p[i])
        a_par = (row(a_ln_w[i]), aws[i], abias[i], c_conv_w[i])
        ya, qb, kb, vb, zb, craw, zc, gt = _proj(x, mod_l, *common, tabs, *a_par, rope=True, tile=ptile)
        ya_c, qb_c, kb_c, vb_c, zb_c, craw_c, zc_c, gt_c = _proj(xc, mod_c, *common, tabs_ctx, *a_par,
                                                                 rope=False, tile=tile_ctx)
        lam_p, sw = b_lam[i], row(b_subln_w[i])
        yb = _attn(qb, zb, lam_p, sw, [kb_c, kb], [vb_c, vb], lam_init=lam_init, tile=tile)
        gate_row = lambda p: jnp.concatenate([jnp.zeros((2 * C_HEADS,), F32), p.reshape(-1),
                                              jnp.zeros((GATE_PAD - N_GATE,), F32)])[None, :]
        gdn_par = (gate_row(c_a_log[i]), gate_row(c_dt_bias[i]), jnp.tile(c_norm_w[i], C_HEADS)[None, :])
        yc_all = _gdn(craw_c, gt_c, zc_c, craw, gt, zc, *gdn_par, ctx_out=ctx_out)
        final = i == depth - 1
        x = _out(x, mod_l, ya, yb, yc_all[0], w_out_b[i], row(final_norm_w), final=final, tile=otile)
        if ctx_out:
            yb_c = _attn(qb_c, zb_c, lam_p, sw, [kb_c], [vb_c], lam_init=lam_init, tile=tile_ctx)
            xc = _out(xc, mod_c, ya_c, yb_c, yc_all[1], w_out_b[i], row(final_norm_w), final=False, tile=tile_ctx)
    return x
```

```python
import functools
import math

import numpy as np
import jax
import jax.numpy as jnp
from jax import lax
from jax.experimental import pallas as pl
from jax.experimental.pallas import tpu as pltpu

F32 = jnp.float32
MXU = jnp.bfloat16

D_MODEL = 1024
NORM_EPS = 1e-6
GRID_W = 64
A_WIDTH, A_GROUPS, A_CHUNK, A_LN_EPS = 256, 4, 128, 1e-5
B_HEADS, B_HEAD_DIM, B_WIDTH = 4, 64, 512
ROPE_THETA = 10000.0
C_HEADS, C_HEAD_DIM, C_WIDTH, C_CONV, C_CHUNK = 4, 64, 256, 5, 64
N_GATE = 4 * C_HEADS
GATE_PAD = 128
OFF_A, OFF_B, OFF_C, OFF_G = 0, 3 * A_WIDTH, 3 * A_WIDTH + 4 * B_WIDTH, 3 * A_WIDTH + 4 * B_WIDTH + 4 * C_WIDTH
D_IN_PAD = OFF_G + GATE_PAD
NEG_BIG = -1e30
LOG2E = math.log2(math.e)
ATT_KCHUNK = 256
ATT_SCHUNK = 256
ATT_QTILE = 256
ATT_TILES = 1
PROJ_TILE = 512
OUT_TILE = 2048
CONV_ROWS = 256
VMEM_LIMIT = 56 * 1024 * 1024
GDN_BLOCK = 256
HALO = 8
ADA_TILE = 512
INV_BLOCK = 16


def _mm(a, b):
    return jnp.dot(a.astype(MXU), b.astype(MXU), preferred_element_type=F32)


def _mm_nt(a, b):
    return lax.dot_general(a.astype(MXU), b.astype(MXU), (((1,), (1,)), ((), ())), preferred_element_type=F32)


def _mm_tn(a, b):
    return lax.dot_general(a.astype(MXU), b.astype(MXU), (((0,), (0,)), ((), ())), preferred_element_type=F32)


def _split(a):
    hi = a.astype(MXU)
    return hi, (a - hi.astype(F32)).astype(MXU)


def _mm3(a, b):
    ah, al = _split(a)
    bh, bl = _split(b)
    d = lambda x, y: jnp.dot(x, y, preferred_element_type=F32)
    return d(ah, bh) + (d(ah, bl) + d(al, bh))


def _sigmoid(x):
    return 1.0 / (1.0 + jnp.exp(-x))


def _silu(x):
    return x * _sigmoid(x)


def _iota(shape, dim):
    return lax.broadcasted_iota(jnp.int32, shape, dim)


def _ada_kernel(c_ref, w_ref, b_ref, o_ref):
    o_ref[0] = _mm3(_silu(c_ref[...]), w_ref[0]) + b_ref[0]


def _ada(c_rows, w_ada, b_ada):
    depth, d, d3 = w_ada.shape
    r = c_rows.shape[0]
    tn = ADA_TILE
    return pl.pallas_call(
        _ada_kernel,
        grid=(depth, d3 // tn),
        in_specs=[pl.BlockSpec((r, d), lambda l, j: (0, 0)),
                  pl.BlockSpec((1, d, tn), lambda l, j: (l, 0, j)),
                  pl.BlockSpec((1, 1, tn), lambda l, j: (l, 0, j))],
        out_specs=pl.BlockSpec((1, r, tn), lambda l, j: (l, 0, j)),
        out_shape=jax.ShapeDtypeStruct((depth, r, d3), F32),
        name="ada_modulation",
    )(c_rows, w_ada, b_ada.reshape(depth, 1, d3))


def _proj_kernel(x_ref, xp_ref, xn_ref, mod_ref, nw_ref, w_ref, cos_ref, sinn_ref, sinp_ref, lnw_ref, aws_ref,
                 abias_ref, convw_ref, ya_ref, q_ref, k_ref, v_ref, zb_ref, cqkv_ref, zc_ref, g_ref, *, rope):
    d = D_MODEL
    mod = mod_ref[0]

    def normed(x):
        y = x * lax.rsqrt(jnp.mean(x * x, axis=-1, keepdims=True) + NORM_EPS) * nw_ref[...]
        return (y * (1.0 + mod[:, d:2 * d]) + mod[:, :d]).astype(MXU)

    t = x_ref.shape[1]
    half_t = t // 2
    proj = lambda lhs, lo, hi: jnp.dot(lhs, w_ref[:, lo:hi], preferred_element_type=F32)
    vals = {}
    h0 = normed(x_ref[0, :half_t])
    pc0 = proj(jnp.concatenate([h0, normed(xp_ref[0])], axis=0), OFF_C, OFF_G)
    h1 = normed(x_ref[0, half_t:])
    pc1 = proj(jnp.concatenate([h1, normed(xn_ref[0])], axis=0), OFF_C, OFF_G)
    h = jnp.concatenate([h0, h1], axis=0)

    def finish_a():
        pa = vals.pop("a")
        u, v, z = pa[:, :A_WIDTH], pa[:, A_WIDTH:2 * A_WIDTH], pa[:, 2 * A_WIDTH:]
        vc = v - jnp.mean(v, axis=-1, keepdims=True)
        vn = (vc * lax.rsqrt(jnp.mean(vc * vc, axis=-1, keepdims=True) + A_LN_EPS) * lnw_ref[...]).astype(MXU)
        grp = _iota((A_CHUNK, A_WIDTH), 1) // (A_WIDTH // A_GROUPS)
        for c in range(t // A_CHUNK):
            rows = slice(c * A_CHUNK, (c + 1) * A_CHUNK)
            r = jnp.dot(aws_ref[...], vn[rows], preferred_element_type=F32)
            s = abias_ref[...]
            for g in range(A_GROUPS):
                s = s + jnp.where(grp == g, r[g * A_CHUNK:(g + 1) * A_CHUNK], 0.0)
            ya_ref[0, rows, :] = (u[rows] * s * _silu(z[rows])).astype(ya_ref.dtype)

    def rot(a):
        if not rope:
            return a
        half = B_HEAD_DIM // 4
        return (a * cos_ref[...] + pltpu.roll(a, B_WIDTH - half, 1) * sinn_ref[...]
                + pltpu.roll(a, half, 1) * sinp_ref[...])

    def finish_q():
        q_ref[0] = (rot(vals.pop("q")) * (B_HEAD_DIM ** -0.5 * LOG2E)).astype(q_ref.dtype)

    def finish_k():
        k_ref[0] = rot(vals.pop("k")).astype(k_ref.dtype)

    def finish_v():
        v_ref[0] = vals.pop("v").T.astype(v_ref.dtype)

    def finish_z():
        zb_ref[0] = vals.pop("z").astype(zb_ref.dtype)

    sub = min(t, CONV_ROWS)

    def start_c():
        i, last = pl.program_id(0), pl.num_programs(0) - 1
        zc_ref[0, :half_t, :] = pc0[:half_t, 3 * C_WIDTH:]
        zc_ref[0, half_t:, :] = pc1[:half_t, 3 * C_WIDTH:]
        prev = jnp.where(i == 0, 0.0, pc0[half_t:, :3 * C_WIDTH])
        nxt = jnp.where(i == last, 0.0, pc1[half_t:, :3 * C_WIDTH])
        vals["c"] = jnp.concatenate([prev, pc0[:half_t, :3 * C_WIDTH], pc1[:half_t, :3 * C_WIDTH], nxt], axis=0)

    def finish_c(r0):
        w = convw_ref[...]
        pad = C_CONV // 2
        win = vals["c"][r0:r0 + sub + 2 * HALO]
        acc = win[HALO:HALO + sub] * w[pad:pad + 1]
        for j in range(C_CONV):
            if j != pad:
                acc = acc + pltpu.roll(win, (pad - j) % (sub + 2 * HALO), 0)[HALO:HALO + sub] * w[j:j + 1]
        act = _silu(acc)
        qh, kh = act[:, :C_WIDTH], act[:, C_WIDTH:2 * C_WIDTH]
        l2 = lambda a: a * lax.rsqrt(_mm_sel(a * a, _same_head()) + 1e-6)
        cqkv_ref[0, r0:r0 + sub, 0:C_WIDTH] = l2(qh) * (C_HEAD_DIM ** -0.5)
        cqkv_ref[0, r0:r0 + sub, C_WIDTH:2 * C_WIDTH] = l2(kh)
        cqkv_ref[0, r0:r0 + sub, 2 * C_WIDTH:] = act[:, 2 * C_WIDTH:]

    def start(name, lo, hi):
        def go():
            vals[name] = proj(h, lo, hi)
        return go

    def finish_g():
        g_ref[0] = vals.pop("g")

    bq = OFF_B
    order = [start_c, start("a", OFF_A, OFF_B)]
    convs = [functools.partial(finish_c, r0) for r0 in range(0, t, sub)]
    later = [start("q", bq, bq + B_WIDTH), start("k", bq + B_WIDTH, bq + 2 * B_WIDTH),
             start("v", bq + 2 * B_WIDTH, bq + 3 * B_WIDTH), start("z", bq + 3 * B_WIDTH, OFF_C),
             start("g", OFF_G, D_IN_PAD)]
    fins = convs + [finish_a, finish_q, finish_k, finish_v, finish_z, finish_g]
    while later or fins:
        if later:
            order.append(later.pop(0))
        if fins:
            order.append(fins.pop(0))
    for step in order:
        step()


def _proj(x, mod, nw, w_in, tabs, lnw, aws, abias, convw, *, rope, tile):
    bsz, n, d = x.shape
    cos, sinn, sinp = tabs
    mod_map = (lambda i, b: (b, 0, 0)) if mod.shape[0] == bsz else (lambda i, b: (0, 0, 0))
    tok = lambda w: pl.BlockSpec((1, tile, w), lambda i, b: (b, i, 0))
    per = tile // HALO
    before = pl.BlockSpec((1, HALO, d), lambda i, b: (b, jnp.maximum(i * per - 1, 0), 0))
    after = pl.BlockSpec((1, HALO, d), lambda i, b: (b, jnp.minimum((i + 1) * per, n // HALO - 1), 0))
    const = lambda a: pl.BlockSpec(a.shape, lambda i, b: (0,) * a.ndim)
    tab = pl.BlockSpec((tile, B_WIDTH), lambda i, b: (i, 0))
    outs = [(A_WIDTH, MXU), (B_WIDTH, MXU), (B_WIDTH, MXU), (B_WIDTH, MXU), (B_WIDTH, F32),
            (3 * C_WIDTH, F32), (C_WIDTH, F32), (GATE_PAD, F32)]
    out_specs = [tok(w) for w, _ in outs]
    out_shape = [jax.ShapeDtypeStruct((bsz, n, w), dt) for w, dt in outs]
    out_specs[3] = pl.BlockSpec((1, B_WIDTH, tile), lambda i, b: (b, 0, i))
    out_shape[3] = jax.ShapeDtypeStruct((bsz, B_WIDTH, n), MXU)
    return pl.pallas_call(
        functools.partial(_proj_kernel, rope=rope),
        grid=(n // tile, bsz),
        in_specs=[tok(d), before, after, pl.BlockSpec((1, 1, 3 * d), mod_map), const(nw), const(w_in), tab, tab, tab,
                  const(lnw), const(aws), const(abias), const(convw)],
        out_specs=out_specs,
        out_shape=out_shape,
        compiler_params=pltpu.CompilerParams(dimension_semantics=("arbitrary", "arbitrary"),
                                             vmem_limit_bytes=VMEM_LIMIT),
        name="norm_inproj_rope" if rope else "norm_inproj_ctx",
    )(x, x, x, mod, nw, w_in, cos, sinn, sinp, lnw, aws, abias, convw)


def _interleave(gens):
    gens = list(gens)
    while gens:
        for g in list(gens):
            if next(g, StopIteration) is StopIteration:
                gens.remove(g)


def _attn_kernel(*refs, seg_lens, lam_init):
    nseg = len(seg_lens)
    q_ref, z_ref, lam_ref, sw_ref = refs[:4]
    k_refs, v_refs = refs[4:4 + nseg], refs[4 + nseg:4 + 2 * nseg]
    o_ref, s_buf, e_buf = refs[4 + 2 * nseg:]
    lf = lam_ref[...]
    lam = (jnp.exp(jnp.sum(lf[0:1] * lf[1:2], axis=-1, keepdims=True))
           - jnp.exp(jnp.sum(lf[2:3] * lf[3:4], axis=-1, keepdims=True)) + lam_init)
    tq = ATT_QTILE
    ntile = q_ref.shape[1] // tq
    pair = 2 * B_HEAD_DIM
    first = _iota((1, pair), 1) < B_HEAD_DIM
    zero = jnp.zeros((), q_ref.dtype)
    chunks, row = [], 0
    for seg, n in enumerate(seg_lens):
        for c0 in range(0, n, ATT_KCHUNK):
            chunks.append((seg, c0, row))
            row += ATT_KCHUNK
    kc = ATT_KCHUNK
    maxes, heads = {}, {}

    dens = {}

    def scores(i):
        hh, j = divmod(i, 2)
        qt, hp = divmod(hh, B_HEADS)
        lanes = slice(hp * pair, (hp + 1) * pair)
        qm = jnp.where(first if j == 0 else jnp.logical_not(first), q_ref[0, qt * tq:(qt + 1) * tq, lanes], zero)
        m, row = None, 0
        for seg, n in enumerate(seg_lens):
            for c0 in range(0, n, ATT_SCHUNK):
                size = min(ATT_SCHUNK, n - c0)
                s = _mm_nt(k_refs[seg][0, c0:c0 + size, lanes], qm)
                for r in range(0, size, kc):
                    s_buf[hh % 2, j, row:row + kc, :] = s[r:r + kc]
                    cm = jnp.max(s[r:r + kc], axis=0, keepdims=True)
                    m = cm if m is None else jnp.maximum(m, cm)
                    row += kc
                    yield
        maxes[i] = m

    def exps(i):
        hh, j = divmod(i, 2)
        den = None
        for _, _, r0 in chunks:
            e = jnp.exp2(s_buf[hh % 2, j, r0:r0 + kc, :] - maxes[i])
            e_buf[hh % 2, j, r0:r0 + kc, :] = e.astype(MXU)
            cs = jnp.sum(e, axis=0, keepdims=True)
            den = cs if den is None else den + cs
            yield
        dens[i] = den

    def values(hh):
        qt, hp = divmod(hh, B_HEADS)
        lanes = slice(hp * pair, (hp + 1) * pair)
        d0, d1 = dens[2 * hh], dens[2 * hh + 1]
        coef = (lam * d0 * (1.0 / d1)).astype(MXU)
        pv = None
        for seg, c0, r0 in chunks:
            p = e_buf[hh % 2, 0, r0:r0 + kc, :] - coef * e_buf[hh % 2, 1, r0:r0 + kc, :]
            t = _mm(v_refs[seg][0, lanes, c0:c0 + kc], p)
            pv = t if pv is None else pv + t
            yield
        acc = (pv * (1.0 / d0)).T
        o = acc * lax.rsqrt(jnp.mean(acc * acc, axis=-1, keepdims=True) + NORM_EPS) * sw_ref[...]
        heads[hh] = o * (1.0 - lam_init)
        if hp == B_HEADS - 1:
            rows = slice(qt * tq, (qt + 1) * tq)
            o_all = jnp.concatenate([heads.pop(qt * B_HEADS + p) for p in range(B_HEADS)], axis=-1)
            o_ref[0, rows, :] = (o_all * _silu(z_ref[0, rows, :])).astype(o_ref.dtype)

    nh = 2 * B_HEADS * ntile
    for t in range(nh + 2):
        stage = []
        if t < nh:
            stage.append(scores(t))
        if 1 <= t <= nh:
            stage.append(exps(t - 1))
        if t >= 3 and t % 2 == 1:
            stage.append(values((t - 3) // 2))
        _interleave(stage)


def _attn(q, z, lam_p, subln_w, ks, vs, *, lam_init, tile):
    bsz, n, w = q.shape
    seg_lens = tuple(a.shape[1] for a in ks)
    tok = pl.BlockSpec((1, tile, w), lambda b, i: (b, i, 0))
    const = lambda a: pl.BlockSpec(a.shape, lambda b, i: (0,) * a.ndim)
    full = lambda a: pl.BlockSpec((1,) + a.shape[1:], lambda b, i: (b, 0, 0))
    return pl.pallas_call(
        functools.partial(_attn_kernel, seg_lens=seg_lens, lam_init=lam_init),
        grid=(bsz, n // tile),
        in_specs=[tok, tok, const(lam_p), const(subln_w)] + [full(a) for a in ks] + [full(a) for a in vs],
        out_specs=tok,
        out_shape=jax.ShapeDtypeStruct((bsz, n, w), MXU),
        scratch_shapes=[pltpu.VMEM((2, 2, sum(seg_lens), ATT_QTILE), F32),
                        pltpu.VMEM((2, 2, sum(seg_lens), ATT_QTILE), MXU)],
        compiler_params=pltpu.CompilerParams(dimension_semantics=("arbitrary", "arbitrary"),
                                             vmem_limit_bytes=VMEM_LIMIT),
        name="diff_attention_%dseg" % len(ks),
    )(q, z, lam_p, subln_w, *ks, *vs)


def _head_of_lane(shape, dim):
    return _iota(shape, dim) // C_HEAD_DIM


def _expand_heads(cols):
    r = cols.shape[0]
    hl = _head_of_lane((r, C_WIDTH), 1)
    out = jnp.broadcast_to(cols[:, C_HEADS - 1:C_HEADS], (r, C_WIDTH))
    for h in range(C_HEADS - 2, -1, -1):
        out = jnp.where(hl == h, jnp.broadcast_to(cols[:, h:h + 1], (r, C_WIDTH)), out)
    return out


def _split3(a):
    a1 = a.astype(MXU)
    r1 = a - a1.astype(F32)
    a2 = r1.astype(MXU)
    return a1, a2, (r1 - a2.astype(F32)).astype(MXU)


def _sel_mm(sel, a):
    d = lambda x: jnp.dot(sel, x, preferred_element_type=F32)
    a1, a2, a3 = _split3(a)
    return d(a1) + (d(a2) + d(a3))


def _mm_sel(a, sel):
    d = lambda x: jnp.dot(x, sel, preferred_element_type=F32)
    a1, a2, a3 = _split3(a)
    return d(a1) + (d(a2) + d(a3))


def _same_head():
    return (_head_of_lane((C_WIDTH, C_WIDTH), 0) == _head_of_lane((C_WIDTH, C_WIDTH), 1)).astype(MXU)


def _gdn_prep(qkv_ref, gate_ref, n, off, alog_ref, dtb_ref, q_s, k_s, v_s, bf_s, bb_s, gf_s, gb_s):
    rows = GDN_BLOCK
    nblk = n // rows
    ri, rj = _iota((rows, rows), 0), _iota((rows, rows), 1)
    same_chunk = (ri // C_CHUNK) == (rj // C_CHUNK)
    csum_f = (same_chunk & (rj <= ri)).astype(MXU)
    csum_b = (same_chunk & (rj >= ri)).astype(MXU)

    def block(i, carry):
        base = pl.multiple_of(i * rows, rows)
        dst = pl.ds(pl.multiple_of(off + base, rows), rows)
        q_s[dst, :] = qkv_ref[0, pl.ds(base, rows), 0:C_WIDTH]
        k_s[dst, :] = qkv_ref[0, pl.ds(base, rows), C_WIDTH:2 * C_WIDTH]
        v_s[dst, :] = qkv_ref[0, pl.ds(base, rows), 2 * C_WIDTH:]
        gt = gate_ref[0, pl.ds(base, rows), :]
        h = C_HEADS
        beta = _sigmoid(gt)
        bf_s[dst, :] = _expand_heads(beta[:, 0:h])
        bb_s[dst, :] = _expand_heads(beta[:, h:2 * h])
        sp = jnp.maximum(gt + dtb_ref[...], 0.0) + jnp.log1p(jnp.exp(-jnp.abs(gt + dtb_ref[...])))
        g = -jnp.exp(alog_ref[...]) * sp
        gf_s[dst, :] = _expand_heads(_sel_mm(csum_f, g)[:, 2 * h:3 * h])
        gb_s[dst, :] = _expand_heads(_sel_mm(csum_b, g)[:, 3 * h:4 * h])
        return carry

    lax.fori_loop(0, nblk, block, 0)


def _block_diag(a, bmask):
    return jnp.where(bmask, jnp.concatenate([a] * C_HEADS, axis=0), 0.0)


def _gdn_local(chains, q_s, k_s, v_s, bufs, buf):
    c, wd = C_CHUNK, C_WIDTH
    u_buf, w_buf, a_buf, qd_buf, kd_buf, eg_buf = bufs
    ii = _iota((c, wd), 0)
    jj = _iota((c, wd), 1) % c
    eye = jj == ii
    bmask = _head_of_lane((C_HEADS * c, wd), 1) == (_iota((C_HEADS * c, wd), 0) // c)
    bd = lambda a: _block_diag(a, bmask)
    ident = jnp.where(eye, 1.0, 0.0)

    st = []
    for rows, backward, beta_s, gc_s, _, _ in chains:
        q, k, v, beta, gc = q_s[rows, :], k_s[rows, :], v_s[rows, :], beta_s[rows, :], gc_s[rows, :]
        incl = (jj >= ii) if backward else (jj <= ii)
        strict = (jj > ii) if backward else (jj < ii)
        grow = jnp.sum(jnp.where(eye, gc, 0.0), axis=0, keepdims=True)
        decay = jnp.exp(jnp.where(incl, gc - grow, NEG_BIG))
        glast = gc[0:1] if backward else gc[c - 1:c]
        st.append(dict(q=q, k=k, v=v, beta=beta, gc=gc, decay=decay, strict=strict, glast=glast, kb=k * beta,
                       egc=jnp.exp(gc)))
    yield
    for s in st:
        qk = _mm_nt(jnp.concatenate([s["kb"], s["q"]], axis=0), bd(s["k"]))
        s["lmat"] = jnp.where(s["strict"], qk[:c] * s["decay"], 0.0)
        s["attn"] = qk[c:] * s["decay"]
    yield
    in_blk = (ii // INV_BLOCK) == (jj // INV_BLOCK)
    for s in st:
        dmat = jnp.where(in_blk, s["lmat"], 0.0)
        s["t"] = ident - dmat
        s["pw"] = _mm(dmat, bd(dmat))
    yield
    nsq = int(math.log2(INV_BLOCK)) - 1
    for step in range(nsq):
        for s in st:
            b = bd(s["pw"])
            if step < nsq - 1:
                r = _mm(jnp.concatenate([s["t"], s["pw"]], axis=0), b)
                s["t"], s["pw"] = s["t"] + r[:c], r[c:]
            else:
                s["t"] = s["t"] + _mm(s["t"], b)
        yield
    size = INV_BLOCK
    while size < c:
        emask = ((ii // (2 * size)) == (jj // (2 * size))) & ((ii // size) != (jj // size))
        for s in st:
            s["y"] = _mm(jnp.where(emask, s["lmat"], 0.0), bd(s["t"]))
        yield
        for s in st:
            s["t"] = s["t"] - _mm(s["t"], bd(s["y"]))
        yield
        size *= 2
    for s, (_, _, _, _, d, slot) in zip(st, chains):
        dst = slice(slot * c, (slot + 1) * c)
        u_buf[buf, d, dst, :] = _mm(s["t"], bd(s["v"] * s["beta"]))
        a_buf[buf, d, dst, :] = s["attn"].astype(MXU)
        qd_buf[buf, d, dst, :] = (s["q"] * s["egc"]).astype(MXU)
    yield
    for s, (_, _, _, _, d, slot) in zip(st, chains):
        dst = slice(slot * c, (slot + 1) * c)
        w_buf[buf, d, dst, :] = _mm(s["t"], bd(s["kb"] * s["egc"])).astype(MXU)
        kd_buf[buf, d, dst, :] = (s["k"] * jnp.exp(s["glast"] - s["gc"])).astype(MXU)
        eg_buf[buf, d, slot * HALO:(slot + 1) * HALO, :] = jnp.broadcast_to(jnp.exp(s["glast"]), (HALO, wd))


def _gdn_scan(steps, rows, st_s, o_refs, bufs, buf):
    c, wd = C_CHUNK, C_WIDTH
    u_buf, w_buf, a_buf, qd_buf, kd_buf, eg_buf = bufs
    bmask = _head_of_lane((C_HEADS * c, wd), 1) == (_iota((C_HEADS * c, wd), 0) // c)
    smask = _head_of_lane((wd, wd), 0) == _head_of_lane((wd, wd), 1)
    states = [st_s[0], st_s[1]]
    for slots in steps:
        src = [slice(slot * c, (slot + 1) * c) for slot in slots]
        ws = [_mm(jnp.concatenate([w_buf[buf, d, r, :], qd_buf[buf, d, r, :]], axis=0), states[d])
              for d, r in enumerate(src)]
        yield
        v_new = [u_buf[buf, d, r, :] - ws[d][:c] for d, r in enumerate(src)]
        upd = [_mm_tn(kd_buf[buf, d, r, :], v_new[d]) for d, r in enumerate(src)]
        for d, (r, slot) in enumerate(zip(src, slots)):
            o_refs[d][rows[d][slot], :] = ws[d][c:] + _mm(a_buf[buf, d, r, :], _block_diag(v_new[d], bmask))
        yield
        states = [states[d] * eg_buf[buf, d, slot * HALO:slot * HALO + 1, :] + jnp.where(smask, upd[d], 0.0)
                  for d, slot in enumerate(slots)]
    st_s[0], st_s[1] = states


def _gdn_kernel(*refs, n_ctx, n_lat, ctx_out):
    (rawc_ref, gtc_ref, zc_ref, rawl_ref, gtl_ref, zl_ref, alog_ref, dtb_ref, nw_ref) = refs[:9]
    nout = 2 if ctx_out else 1
    out_refs = refs[9:9 + nout]
    (q_s, k_s, v_s, bf_s, bb_s, gf_s, gb_s, of_s, ob_s, st_s) = refs[9 + nout:19 + nout]
    scr = (q_s, k_s, v_s, bf_s, bb_s, gf_s, gb_s)
    bufs = refs[19 + nout:]
    _gdn_prep(rawc_ref, gtc_ref, n_ctx, 0, alog_ref, dtb_ref, *scr)
    _gdn_prep(rawl_ref, gtl_ref, n_lat, n_ctx, alog_ref, dtb_ref, *scr)
    st_s[...] = jnp.zeros_like(st_s)
    ncb, nlb = n_ctx // GDN_BLOCK, n_lat // GDN_BLOCK
    per = GDN_BLOCK // C_CHUNK

    nblk = ncb + nlb
    steps = [(t, per - 1 - t) for t in range(per)]

    def block_rows(i):
        i = jnp.asarray(i, jnp.int32)
        bf = pl.multiple_of(i * GDN_BLOCK, GDN_BLOCK)
        bb = pl.multiple_of(jnp.where(i < ncb, ncb - 1 - i, 2 * ncb + nlb - 1 - i) * GDN_BLOCK, GDN_BLOCK)
        return ([pl.ds(bf + r * C_CHUNK, C_CHUNK) for r in range(per)],
                [pl.ds(bb + r * C_CHUNK, C_CHUNK) for r in range(per)])

    def local(i):
        rows_f, rows_b = block_rows(i)
        chains = ([(rows_f[r], False, bf_s, gf_s, 0, r) for r in range(per)]
                  + [(rows_b[r], True, bb_s, gb_s, 1, r) for r in range(per)])
        return _gdn_local(chains, q_s, k_s, v_s, bufs, i % 2)

    def scan(i):
        return _gdn_scan(steps, block_rows(i), st_s, (of_s, ob_s), bufs, i % 2)

    _interleave([local(0)])

    def block(i, carry):
        _interleave([local(i + 1), scan(i)])
        return carry

    lax.fori_loop(0, nblk - 1, block, 0)
    _interleave([scan(nblk - 1)])

    same_head = _same_head()

    def finish(z_ref, out_ref, off, n):
        rows = GDN_BLOCK

        def blk(i, carry):
            base = pl.multiple_of(i * rows, rows)
            src = pl.ds(pl.multiple_of(off + base, rows), rows)
            o = of_s[src, :] + ob_s[src, :]
            ms = _mm_sel(o * o, same_head) * (1.0 / C_HEAD_DIM)
            y = o * lax.rsqrt(ms + NORM_EPS) * nw_ref[...]
            out_ref[0, pl.ds(base, rows), :] = (y * _silu(z_ref[0, pl.ds(base, rows), :])).astype(out_ref.dtype)
            return carry

        lax.fori_loop(0, n // rows, blk, 0)

    finish(zl_ref, out_refs[0], n_ctx, n_lat)
    if ctx_out:
        finish(zc_ref, out_refs[1], 0, n_ctx)


def _gdn(rawc, gtc, zc, rawl, gtl, zl, alog, dtb, nw, *, ctx_out):
    bsz, n_ctx, _ = rawc.shape
    n_lat = rawl.shape[1]
    nt = n_ctx + n_lat
    seq = lambda a: pl.BlockSpec((1,) + a.shape[1:], lambda b: (b, 0, 0))
    const = lambda a: pl.BlockSpec(a.shape, lambda b: (0,) * a.ndim)
    out_shape = [jax.ShapeDtypeStruct((bsz, n_lat, C_WIDTH), MXU)]
    if ctx_out:
        out_shape.append(jax.ShapeDtypeStruct((bsz, n_ctx, C_WIDTH), MXU))
    big = lambda: pltpu.VMEM((nt, C_WIDTH), F32)
    loc = lambda dt: pltpu.VMEM((2, 2, GDN_BLOCK, C_WIDTH), dt)
    per = GDN_BLOCK // C_CHUNK
    res = pl.pallas_call(
        functools.partial(_gdn_kernel, n_ctx=n_ctx, n_lat=n_lat, ctx_out=ctx_out),
        grid=(bsz,),
        in_specs=[seq(rawc), seq(gtc), seq(zc), seq(rawl), seq(gtl), seq(zl),
                  const(alog), const(dtb), const(nw)],
        out_specs=[seq(s) for s in out_shape],
        out_shape=out_shape,
        scratch_shapes=[big() for _ in range(9)] + [pltpu.VMEM((2, C_WIDTH, C_WIDTH), F32)]
        + [loc(F32), loc(MXU), loc(MXU), loc(MXU), loc(MXU), pltpu.VMEM((2, 2, per * HALO, C_WIDTH), F32)],
        compiler_params=pltpu.CompilerParams(dimension_semantics=("arbitrary",), vmem_limit_bytes=VMEM_LIMIT),
        name="gated_deltanet_bidir",
    )(rawc, gtc, zc, rawl, gtl, zl, alog, dtb, nw)
    return res


def _out_kernel(x_ref, mod_ref, ya_ref, yb_ref, yc_ref, w_ref, fw_ref, o_ref, *, final):
    d = D_MODEL
    y = (jnp.dot(ya_ref[0], w_ref[0:A_WIDTH], preferred_element_type=F32)
         + jnp.dot(yb_ref[0], w_ref[A_WIDTH:A_WIDTH + B_WIDTH], preferred_element_type=F32)
         + jnp.dot(yc_ref[0], w_ref[A_WIDTH + B_WIDTH:], preferred_element_type=F32))
    x = x_ref[0] + mod_ref[0][:, 2 * d:] * y
    if final:
        x = x * lax.rsqrt(jnp.mean(x * x, axis=-1, keepdims=True) + NORM_EPS) * fw_ref[...]
    o_ref[0] = x


def _out(x, mod, ya, yb, yc, w_out, fw, *, final, tile):
    bsz, n, d = x.shape
    mod_map = (lambda b, i: (b, 0, 0)) if mod.shape[0] == bsz else (lambda b, i: (0, 0, 0))
    tok = lambda w: pl.BlockSpec((1, tile, w), lambda b, i: (b, i, 0))
    const = lambda a: pl.BlockSpec(a.shape, lambda b, i: (0,) * a.ndim)
    return pl.pallas_call(
        functools.partial(_out_kernel, final=final),
        grid=(bsz, n // tile),
        in_specs=[tok(d), pl.BlockSpec((1, 1, 3 * d), mod_map), tok(A_WIDTH), tok(B_WIDTH), tok(C_WIDTH),
                  const(w_out), const(fw)],
        out_specs=tok(d),
        out_shape=jax.ShapeDtypeStruct((bsz, n, d), F32),
        compiler_params=pltpu.CompilerParams(dimension_semantics=("arbitrary", "arbitrary"),
                                             vmem_limit_bytes=VMEM_LIMIT),
        name="outproj_residual_final" if final else "outproj_residual",
    )(x, mod, ya, yb, yc, w_out, fw)


def _rope_tables(n):
    pos = jnp.arange(n)
    half = B_HEAD_DIM // 2
    inv_freq = ROPE_THETA ** (-jnp.arange(0, half, 2, dtype=F32) / half)
    lane = np.arange(B_WIDTH)
    dd = lane % B_HEAD_DIM
    use_col = (dd // half) == 1
    second = ((dd % half) // (half // 2)) == 1
    fidx = dd % (half // 2)
    p = jnp.where(use_col[None, :], (pos % GRID_W)[:, None], (pos // GRID_W)[:, None]).astype(F32)
    ang = p * inv_freq[fidx][None, :]
    cos, sin = jnp.cos(ang), jnp.sin(ang)
    return cos, jnp.where(second[None, :], 0.0, -sin), jnp.where(second[None, :], sin, 0.0)


def kernel(x, c, ctx, c_ctx, w_ada, b_ada, norm_w, w_in, w_out, a_ln_w, a_ws, a_bs, b_lam, b_subln_w,
           c_conv_w, c_a_log, c_dt_bias, c_norm_w, final_norm_w):
    bsz, n, d = x.shape
    n_ctx = ctx.shape[1]
    depth = w_ada.shape[0]
    assert d == D_MODEL and n % GDN_BLOCK == 0 and n_ctx % GDN_BLOCK == 0 and n % GRID_W == 0
    tile = ATT_QTILE * ATT_TILES
    tile_ctx = min(ATT_QTILE, n_ctx)
    ptile = min(PROJ_TILE, n)
    otile = next(t for t in (OUT_TILE, PROJ_TILE, GDN_BLOCK) if n % t == 0)
    assert n % ptile == 0 and n % tile == 0

    rows = bsz + 1
    rows_pad = -(-rows // 8) * 8
    c_rows = jnp.concatenate([c, c_ctx[None, :], jnp.zeros((rows_pad - rows, d), F32)], axis=0)
    mods = _ada(c_rows, w_ada, b_ada)

    tabs = _rope_tables(n)
    tabs_ctx = tuple(t[:tile_ctx] for t in tabs)
    gate_pad = jnp.zeros((depth, d, GATE_PAD - N_GATE), w_in.dtype)
    w_in_p = jnp.concatenate([w_in, gate_pad], axis=-1).astype(MXU)
    w_out_b = w_out.astype(MXU)
    aws = a_ws.reshape(depth, A_GROUPS * A_CHUNK, A_CHUNK).astype(MXU)
    abias = jnp.repeat(jnp.swapaxes(a_bs, 1, 2), A_WIDTH // A_GROUPS, axis=2)
    row = lambda a: a.reshape(1, -1)

    xc = ctx
    for i in range(depth):
        lam_init = 0.8 - 0.6 * math.exp(-0.3 * i)
        ctx_out = i < depth - 1
        mod_l = mods[i, :bsz][:, None, :]
        mod_c = mods[i, bsz:bsz + 1][:, None, :]
        common = (row(norm_w[i]), w_in_p[i])
        a_par = (row(a_ln_w[i]), aws[i], abias[i], c_conv_w[i])
        ya, qb, kb, vb, zb, craw, zc, gt = _proj(x, mod_l, *common, tabs, *a_par, rope=True, tile=ptile)
        ya_c, qb_c, kb_c, vb_c, zb_c, craw_c, zc_c, gt_c = _proj(xc, mod_c, *common, tabs_ctx, *a_par,
                                                                 rope=False, tile=tile_ctx)
        lam_p, sw = b_lam[i], row(b_subln_w[i])
        yb = _attn(qb, zb, lam_p, sw, [kb_c, kb], [vb_c, vb], lam_init=lam_init, tile=tile)
        gate_row = lambda p: jnp.concatenate([jnp.zeros((2 * C_HEADS,), F32), p.reshape(-1),
                                              jnp.zeros((GATE_PAD - N_GATE,), F32)])[None, :]
        gdn_par = (gate_row(c_a_log[i]), gate_row(c_dt_bias[i]), jnp.tile(c_norm_w[i], C_HEADS)[None, :])
        yc_all = _gdn(craw_c, gt_c, zc_c, craw, gt, zc, *gdn_par, ctx_out=ctx_out)
        final = i == depth - 1
        x = _out(x, mod_l, ya, yb, yc_all[0], w_out_b[i], row(final_norm_w), final=final, tile=otile)
        if ctx_out:
            yb_c = _attn(qb_c, zb_c, lam_p, sw, [kb_c], [vb_c], lam_init=lam_init, tile=tile_ctx)
            xc = _out(xc, mod_c, ya_c, yb_c, yc_all[1], w_out_b[i], row(final_norm_w), final=False, tile=tile_ctx)
    return x
```

```python
import functools
import math

import numpy as np
import jax
import jax.numpy as jnp
from jax import lax
from jax.experimental import pallas as pl
from jax.experimental.pallas import tpu as pltpu

F32 = jnp.float32
MXU = jnp.bfloat16

D_MODEL = 1024
NORM_EPS = 1e-6
GRID_W = 64
A_WIDTH, A_GROUPS, A_CHUNK, A_LN_EPS = 256, 4, 128, 1e-5
B_HEADS, B_HEAD_DIM, B_WIDTH = 4, 64, 512
ROPE_THETA = 10000.0
C_HEADS, C_HEAD_DIM, C_WIDTH, C_CONV, C_CHUNK = 4, 64, 256, 5, 64
N_GATE = 4 * C_HEADS
GATE_PAD = 128
OFF_A, OFF_B, OFF_C, OFF_G = 0, 3 * A_WIDTH, 3 * A_WIDTH + 4 * B_WIDTH, 3 * A_WIDTH + 4 * B_WIDTH + 4 * C_WIDTH
D_IN_PAD = OFF_G + GATE_PAD
NEG_BIG = -1e30
LOG2E = math.log2(math.e)
ATT_KCHUNK = 256
ATT_SCHUNK = 256
ATT_QTILE = 256
ATT_TILES = 1
PROJ_TILE = 512
OUT_TILE = 2048
CONV_ROWS = 256
VMEM_LIMIT = 56 * 1024 * 1024
GDN_BLOCK = 256
GDN_LOCKSTEP = 8
HALO = 8
ADA_TILE = 512
INV_BLOCK = 16


def _mm(a, b):
    return jnp.dot(a.astype(MXU), b.astype(MXU), preferred_element_type=F32)


def _mm_nt(a, b):
    return lax.dot_general(a.astype(MXU), b.astype(MXU), (((1,), (1,)), ((), ())), preferred_element_type=F32)


def _mm_tn(a, b):
    return lax.dot_general(a.astype(MXU), b.astype(MXU), (((0,), (0,)), ((), ())), preferred_element_type=F32)


def _split(a):
    hi = a.astype(MXU)
    return hi, (a - hi.astype(F32)).astype(MXU)


def _mm3(a, b):
    ah, al = _split(a)
    bh, bl = _split(b)
    d = lambda x, y: jnp.dot(x, y, preferred_element_type=F32)
    return d(ah, bh) + (d(ah, bl) + d(al, bh))


def _sigmoid(x):
    return 1.0 / (1.0 + jnp.exp(-x))


def _silu(x):
    return x * _sigmoid(x)


def _iota(shape, dim):
    return lax.broadcasted_iota(jnp.int32, shape, dim)


def _ada_kernel(c_ref, w_ref, b_ref, o_ref):
    o_ref[0] = _mm3(_silu(c_ref[...]), w_ref[0]) + b_ref[0]


def _ada(c_rows, w_ada, b_ada):
    depth, d, d3 = w_ada.shape
    r = c_rows.shape[0]
    tn = ADA_TILE
    return pl.pallas_call(
        _ada_kernel,
        grid=(depth, d3 // tn),
        in_specs=[pl.BlockSpec((r, d), lambda l, j: (0, 0)),
                  pl.BlockSpec((1, d, tn), lambda l, j: (l, 0, j)),
                  pl.BlockSpec((1, 1, tn), lambda l, j: (l, 0, j))],
        out_specs=pl.BlockSpec((1, r, tn), lambda l, j: (l, 0, j)),
        out_shape=jax.ShapeDtypeStruct((depth, r, d3), F32),
        name="ada_modulation",
    )(c_rows, w_ada, b_ada.reshape(depth, 1, d3))


def _proj_kernel(x_ref, xp_ref, xn_ref, mod_ref, nw_ref, w_ref, cos_ref, sinn_ref, sinp_ref, lnw_ref, aws_ref,
                 abias_ref, convw_ref, ya_ref, q_ref, k_ref, v_ref, zb_ref, cqkv_ref, zc_ref, g_ref, *, rope):
    d = D_MODEL
    mod = mod_ref[0]

    def normed(x):
        y = x * lax.rsqrt(jnp.mean(x * x, axis=-1, keepdims=True) + NORM_EPS) * nw_ref[...]
        return (y * (1.0 + mod[:, d:2 * d]) + mod[:, :d]).astype(MXU)

    t = x_ref.shape[1]
    half_t = t // 2
    proj = lambda lhs, lo, hi: jnp.dot(lhs, w_ref[:, lo:hi], preferred_element_type=F32)
    vals = {}
    h0 = normed(x_ref[0, :half_t])
    pc0 = proj(jnp.concatenate([h0, normed(xp_ref[0])], axis=0), OFF_C, OFF_G)
    h1 = normed(x_ref[0, half_t:])
    pc1 = proj(jnp.concatenate([h1, normed(xn_ref[0])], axis=0), OFF_C, OFF_G)
    h = jnp.concatenate([h0, h1], axis=0)

    def finish_a():
        pa = vals.pop("a")
        u, v, z = pa[:, :A_WIDTH], pa[:, A_WIDTH:2 * A_WIDTH], pa[:, 2 * A_WIDTH:]
        vc = v - jnp.mean(v, axis=-1, keepdims=True)
        vn = (vc * lax.rsqrt(jnp.mean(vc * vc, axis=-1, keepdims=True) + A_LN_EPS) * lnw_ref[...]).astype(MXU)
        grp = _iota((A_CHUNK, A_WIDTH), 1) // (A_WIDTH // A_GROUPS)
        for c in range(t // A_CHUNK):
            rows = slice(c * A_CHUNK, (c + 1) * A_CHUNK)
            r = jnp.dot(aws_ref[...], vn[rows], preferred_element_type=F32)
            s = abias_ref[...]
            for g in range(A_GROUPS):
                s = s + jnp.where(grp == g, r[g * A_CHUNK:(g + 1) * A_CHUNK], 0.0)
            ya_ref[0, rows, :] = (u[rows] * s * _silu(z[rows])).astype(ya_ref.dtype)

    def rot(a):
        if not rope:
            return a
        half = B_HEAD_DIM // 4
        return (a * cos_ref[...] + pltpu.roll(a, B_WIDTH - half, 1) * sinn_ref[...]
                + pltpu.roll(a, half, 1) * sinp_ref[...])

    def finish_q():
        q_ref[0] = (rot(vals.pop("q")) * (B_HEAD_DIM ** -0.5 * LOG2E)).astype(q_ref.dtype)

    def finish_k():
        k_ref[0] = rot(vals.pop("k")).astype(k_ref.dtype)

    def finish_v():
        v_ref[0] = vals.pop("v").T.astype(v_ref.dtype)

    def finish_z():
        zb_ref[0] = vals.pop("z").astype(zb_ref.dtype)

    sub = min(t, CONV_ROWS)

    def start_c():
        i, last = pl.program_id(0), pl.num_programs(0) - 1
        zc_ref[0, :half_t, :] = pc0[:half_t, 3 * C_WIDTH:]
        zc_ref[0, half_t:, :] = pc1[:half_t, 3 * C_WIDTH:]
        prev = jnp.where(i == 0, 0.0, pc0[half_t:, :3 * C_WIDTH])
        nxt = jnp.where(i == last, 0.0, pc1[half_t:, :3 * C_WIDTH])
        vals["c"] = jnp.concatenate([prev, pc0[:half_t, :3 * C_WIDTH], pc1[:half_t, :3 * C_WIDTH], nxt], axis=0)

    def finish_c(r0):
        w = convw_ref[...]
        pad = C_CONV // 2
        win = vals["c"][r0:r0 + sub + 2 * HALO]
        acc = win[HALO:HALO + sub] * w[pad:pad + 1]
        for j in range(C_CONV):
            if j != pad:
                acc = acc + pltpu.roll(win, (pad - j) % (sub + 2 * HALO), 0)[HALO:HALO + sub] * w[j:j + 1]
        act = _silu(acc)
        qh, kh = act[:, :C_WIDTH], act[:, C_WIDTH:2 * C_WIDTH]
        l2 = lambda a: a * lax.rsqrt(_mm_sel(a * a, _same_head()) + 1e-6)
        cqkv_ref[0, r0:r0 + sub, 0:C_WIDTH] = l2(qh) * (C_HEAD_DIM ** -0.5)
        cqkv_ref[0, r0:r0 + sub, C_WIDTH:2 * C_WIDTH] = l2(kh)
        cqkv_ref[0, r0:r0 + sub, 2 * C_WIDTH:] = act[:, 2 * C_WIDTH:]

    def start(name, lo, hi):
        def go():
            vals[name] = proj(h, lo, hi)
        return go

    def finish_g():
        g_ref[0] = vals.pop("g")

    bq = OFF_B
    order = [start_c, start("a", OFF_A, OFF_B)]
    convs = [functools.partial(finish_c, r0) for r0 in range(0, t, sub)]
    later = [start("q", bq, bq + B_WIDTH), start("k", bq + B_WIDTH, bq + 2 * B_WIDTH),
             start("v", bq + 2 * B_WIDTH, bq + 3 * B_WIDTH), start("z", bq + 3 * B_WIDTH, OFF_C),
             start("g", OFF_G, D_IN_PAD)]
    fins = convs + [finish_a, finish_q, finish_k, finish_v, finish_z, finish_g]
    while later or fins:
        if later:
            order.append(later.pop(0))
        if fins:
            order.append(fins.pop(0))
    for step in order:
        step()


def _proj(x, mod, nw, w_in, tabs, lnw, aws, abias, convw, *, rope, tile):
    bsz, n, d = x.shape
    cos, sinn, sinp = tabs
    mod_map = (lambda i, b: (b, 0, 0)) if mod.shape[0] == bsz else (lambda i, b: (0, 0, 0))
    tok = lambda w: pl.BlockSpec((1, tile, w), lambda i, b: (b, i, 0))
    per = tile // HALO
    before = pl.BlockSpec((1, HALO, d), lambda i, b: (b, jnp.maximum(i * per - 1, 0), 0))
    after = pl.BlockSpec((1, HALO, d), lambda i, b: (b, jnp.minimum((i + 1) * per, n // HALO - 1), 0))
    const = lambda a: pl.BlockSpec(a.shape, lambda i, b: (0,) * a.ndim)
    tab = pl.BlockSpec((tile, B_WIDTH), lambda i, b: (i, 0))
    outs = [(A_WIDTH, MXU), (B_WIDTH, MXU), (B_WIDTH, MXU), (B_WIDTH, MXU), (B_WIDTH, F32),
            (3 * C_WIDTH, F32), (C_WIDTH, F32), (GATE_PAD, F32)]
    out_specs = [tok(w) for w, _ in outs]
    out_shape = [jax.ShapeDtypeStruct((bsz, n, w), dt) for w, dt in outs]
    out_specs[3] = pl.BlockSpec((1, B_WIDTH, tile), lambda i, b: (b, 0, i))
    out_shape[3] = jax.ShapeDtypeStruct((bsz, B_WIDTH, n), MXU)
    return pl.pallas_call(
        functools.partial(_proj_kernel, rope=rope),
        grid=(n // tile, bsz),
        in_specs=[tok(d), before, after, pl.BlockSpec((1, 1, 3 * d), mod_map), const(nw), const(w_in), tab, tab, tab,
                  const(lnw), const(aws), const(abias), const(convw)],
        out_specs=out_specs,
        out_shape=out_shape,
        compiler_params=pltpu.CompilerParams(dimension_semantics=("arbitrary", "arbitrary"),
                                             vmem_limit_bytes=VMEM_LIMIT),
        name="norm_inproj_rope" if rope else "norm_inproj_ctx",
    )(x, x, x, mod, nw, w_in, cos, sinn, sinp, lnw, aws, abias, convw)


def _interleave(gens):
    gens = list(gens)
    while gens:
        for g in list(gens):
            if next(g, StopIteration) is StopIteration:
                gens.remove(g)


def _attn_kernel(*refs, seg_lens, lam_init):
    nseg = len(seg_lens)
    q_ref, z_ref, lam_ref, sw_ref = refs[:4]
    k_refs, v_refs = refs[4:4 + nseg], refs[4 + nseg:4 + 2 * nseg]
    o_ref, s_buf, e_buf = refs[4 + 2 * nseg:]
    lf = lam_ref[...]
    lam = (jnp.exp(jnp.sum(lf[0:1] * lf[1:2], axis=-1, keepdims=True))
           - jnp.exp(jnp.sum(lf[2:3] * lf[3:4], axis=-1, keepdims=True)) + lam_init)
    tq = ATT_QTILE
    ntile = q_ref.shape[1] // tq
    pair = 2 * B_HEAD_DIM
    first = _iota((1, pair), 1) < B_HEAD_DIM
    zero = jnp.zeros((), q_ref.dtype)
    chunks, row = [], 0
    for seg, n in enumerate(seg_lens):
        for c0 in range(0, n, ATT_KCHUNK):
            chunks.append((seg, c0, row))
            row += ATT_KCHUNK
    kc = ATT_KCHUNK
    maxes, heads = {}, {}

    dens = {}

    def scores(i):
        hh, j = divmod(i, 2)
        qt, hp = divmod(hh, B_HEADS)
        lanes = slice(hp * pair, (hp + 1) * pair)
        qm = jnp.where(first if j == 0 else jnp.logical_not(first), q_ref[0, qt * tq:(qt + 1) * tq, lanes], zero)
        m, row = None, 0
        for seg, n in enumerate(seg_lens):
            for c0 in range(0, n, ATT_SCHUNK):
                size = min(ATT_SCHUNK, n - c0)
                s = _mm_nt(k_refs[seg][0, c0:c0 + size, lanes], qm)
                for r in range(0, size, kc):
                    s_buf[hh % 2, j, row:row + kc, :] = s[r:r + kc]
                    cm = jnp.max(s[r:r + kc], axis=0, keepdims=True)
                    m = cm if m is None else jnp.maximum(m, cm)
                    row += kc
                    yield
        maxes[i] = m

    def exps(i):
        hh, j = divmod(i, 2)
        den = None
        for _, _, r0 in chunks:
            e = jnp.exp2(s_buf[hh % 2, j, r0:r0 + kc, :] - maxes[i])
            e_buf[hh % 2, j, r0:r0 + kc, :] = e.astype(MXU)
            cs = jnp.sum(e, axis=0, keepdims=True)
            den = cs if den is None else den + cs
            yield
        dens[i] = den

    def values(hh):
        qt, hp = divmod(hh, B_HEADS)
        lanes = slice(hp * pair, (hp + 1) * pair)
        d0, d1 = dens[2 * hh], dens[2 * hh + 1]
        coef = (lam * d0 * (1.0 / d1)).astype(MXU)
        pv = None
        for seg, c0, r0 in chunks:
            p = e_buf[hh % 2, 0, r0:r0 + kc, :] - coef * e_buf[hh % 2, 1, r0:r0 + kc, :]
            t = _mm(v_refs[seg][0, lanes, c0:c0 + kc], p)
            pv = t if pv is None else pv + t
            yield
        acc = (pv * (1.0 / d0)).T
        o = acc * lax.rsqrt(jnp.mean(acc * acc, axis=-1, keepdims=True) + NORM_EPS) * sw_ref[...]
        heads[hh] = o * (1.0 - lam_init)
        if hp == B_HEADS - 1:
            rows = slice(qt * tq, (qt + 1) * tq)
            o_all = jnp.concatenate([heads.pop(qt * B_HEADS + p) for p in range(B_HEADS)], axis=-1)
            o_ref[0, rows, :] = (o_all * _silu(z_ref[0, rows, :])).astype(o_ref.dtype)

    nh = 2 * B_HEADS * ntile
    for t in range(nh + 2):
        stage = []
        if t < nh:
            stage.append(scores(t))
        if 1 <= t <= nh:
            stage.append(exps(t - 1))
        if t >= 3 and t % 2 == 1:
            stage.append(values((t - 3) // 2))
        _interleave(stage)


def _attn(q, z, lam_p, subln_w, ks, vs, *, lam_init, tile):
    bsz, n, w = q.shape
    seg_lens = tuple(a.shape[1] for a in ks)
    tok = pl.BlockSpec((1, tile, w), lambda b, i: (b, i, 0))
    const = lambda a: pl.BlockSpec(a.shape, lambda b, i: (0,) * a.ndim)
    full = lambda a: pl.BlockSpec((1,) + a.shape[1:], lambda b, i: (b, 0, 0))
    return pl.pallas_call(
        functools.partial(_attn_kernel, seg_lens=seg_lens, lam_init=lam_init),
        grid=(bsz, n // tile),
        in_specs=[tok, tok, const(lam_p), const(subln_w)] + [full(a) for a in ks] + [full(a) for a in vs],
        out_specs=tok,
        out_shape=jax.ShapeDtypeStruct((bsz, n, w), MXU),
        scratch_shapes=[pltpu.VMEM((2, 2, sum(seg_lens), ATT_QTILE), F32),
                        pltpu.VMEM((2, 2, sum(seg_lens), ATT_QTILE), MXU)],
        compiler_params=pltpu.CompilerParams(dimension_semantics=("arbitrary", "arbitrary"),
                                             vmem_limit_bytes=VMEM_LIMIT),
        name="diff_attention_%dseg" % len(ks),
    )(q, z, lam_p, subln_w, *ks, *vs)


def _head_of_lane(shape, dim):
    return _iota(shape, dim) // C_HEAD_DIM


def _expand_heads(src, first):
    r, w = src.shape
    lane = _iota((r, w), 1)
    halves = [jnp.take_along_axis(src, first + (lane + off) // C_HEAD_DIM, axis=1) for off in range(0, C_WIDTH, w)]
    return jnp.concatenate(halves, axis=1)


def _split3(a):
    a1 = a.astype(MXU)
    r1 = a - a1.astype(F32)
    a2 = r1.astype(MXU)
    return a1, a2, (r1 - a2.astype(F32)).astype(MXU)


def _mm_sel(a, sel):
    hi, lo = _split(a)
    return jnp.dot(hi, sel, preferred_element_type=F32) + jnp.dot(lo, sel, preferred_element_type=F32)


def _same_head():
    return (_head_of_lane((C_WIDTH, C_WIDTH), 0) == _head_of_lane((C_WIDTH, C_WIDTH), 1)).astype(MXU)


def _gdn_prep(qkv_ref, gate_ref, n, off, alog_ref, dtb_ref, q_s, k_s, v_s, bf_s, bb_s, gf_s, gb_s):
    rows = GDN_BLOCK
    nblk = n // rows
    ri, rj = _iota((rows, rows), 0), _iota((rows, rows), 1)
    same_chunk = (ri // C_CHUNK) == (rj // C_CHUNK)
    csum = jnp.concatenate([(same_chunk & (rj <= ri)).astype(MXU), (same_chunk & (rj >= ri)).astype(MXU)], axis=0)

    def block(i, carry):
        base = pl.multiple_of(i * rows, rows)
        dst = pl.ds(pl.multiple_of(off + base, rows), rows)
        q_s[dst, :] = qkv_ref[0, pl.ds(base, rows), 0:C_WIDTH]
        k_s[dst, :] = qkv_ref[0, pl.ds(base, rows), C_WIDTH:2 * C_WIDTH]
        v_s[dst, :] = qkv_ref[0, pl.ds(base, rows), 2 * C_WIDTH:]
        gt = gate_ref[0, pl.ds(base, rows), :]
        h = C_HEADS
        beta = _sigmoid(gt)
        bf_s[dst, :] = _expand_heads(beta, 0)
        bb_s[dst, :] = _expand_heads(beta, h)
        sp = jnp.maximum(gt + dtb_ref[...], 0.0) + jnp.log1p(jnp.exp(-jnp.abs(gt + dtb_ref[...])))
        g = -jnp.exp(alog_ref[...]) * sp
        w = GATE_PAD
        r = jnp.dot(csum, jnp.concatenate(_split3(g), axis=1), preferred_element_type=F32)
        gc = r[:, :w] + (r[:, w:2 * w] + r[:, 2 * w:])
        gf_s[dst, :] = _expand_heads(gc[:rows], 2 * h)
        gb_s[dst, :] = _expand_heads(gc[rows:], 3 * h)
        return carry

    lax.fori_loop(0, nblk, block, 0)


def _block_diag(a, bmask):
    return jnp.where(bmask, jnp.concatenate([a] * C_HEADS, axis=0), 0.0)


def _gdn_local(chains, q_s, k_s, v_s, bufs, buf):
    c, wd = C_CHUNK, C_WIDTH
    u_buf, w_buf, a_buf, qd_buf, kd_buf, eg_buf = bufs
    ii = _iota((c, wd), 0)
    jj = _iota((c, wd), 1) % c
    eye = jj == ii
    bmask = _head_of_lane((C_HEADS * c, wd), 1) == (_iota((C_HEADS * c, wd), 0) // c)
    bd = lambda a: _block_diag(a, bmask)
    ident = jnp.where(eye, 1.0, 0.0)

    st = []
    for rows, backward, beta_s, gc_s, _, _ in chains:
        q, k, v, beta, gc = q_s[rows, :], k_s[rows, :], v_s[rows, :], beta_s[rows, :], gc_s[rows, :]
        incl = (jj >= ii) if backward else (jj <= ii)
        strict = (jj > ii) if backward else (jj < ii)
        grow = jnp.sum(jnp.where(eye, gc, 0.0), axis=0, keepdims=True)
        decay = jnp.exp(jnp.where(incl, gc - grow, NEG_BIG))
        glast = gc[0:1] if backward else gc[c - 1:c]
        st.append(dict(q=q, k=k, v=v, beta=beta, gc=gc, decay=decay, strict=strict, glast=glast, kb=k * beta,
                       egc=jnp.exp(gc)))
    yield
    for s in st:
        qk = _mm_nt(jnp.concatenate([s["kb"], s["q"]], axis=0), bd(s["k"]))
        s["lmat"] = jnp.where(s["strict"], qk[:c] * s["decay"], 0.0)
        s["attn"] = qk[c:] * s["decay"]
    yield
    in_blk = (ii // INV_BLOCK) == (jj // INV_BLOCK)
    for s in st:
        dmat = jnp.where(in_blk, s["lmat"], 0.0)
        s["t"] = ident - dmat
        s["pw"] = _mm(dmat, bd(dmat))
    yield
    nsq = int(math.log2(INV_BLOCK)) - 1
    for step in range(nsq):
        for s in st:
            b = bd(s["pw"])
            if step < nsq - 1:
                r = _mm(jnp.concatenate([s["t"], s["pw"]], axis=0), b)
                s["t"], s["pw"] = s["t"] + r[:c], r[c:]
            else:
                s["t"] = s["t"] + _mm(s["t"], b)
        yield
    size = INV_BLOCK
    while size < c:
        emask = ((ii // (2 * size)) == (jj // (2 * size))) & ((ii // size) != (jj // size))
        for s in st:
            s["y"] = _mm(jnp.where(emask, s["lmat"], 0.0), bd(s["t"]))
        yield
        for s in st:
            s["t"] = s["t"] - _mm(s["t"], bd(s["y"]))
        yield
        size *= 2
    for s, (_, _, _, _, d, slot) in zip(st, chains):
        dst = slice(slot * c, (slot + 1) * c)
        u_buf[buf, d, dst, :] = _mm(s["t"], bd(s["v"] * s["beta"]))
        a_buf[buf, d, dst, :] = s["attn"].astype(MXU)
        qd_buf[buf, d, dst, :] = (s["q"] * s["egc"]).astype(MXU)
    yield
    for s, (_, _, _, _, d, slot) in zip(st, chains):
        dst = slice(slot * c, (slot + 1) * c)
        w_buf[buf, d, dst, :] = _mm(s["t"], bd(s["kb"] * s["egc"])).astype(MXU)
        kd_buf[buf, d, dst, :] = (s["k"] * jnp.exp(s["glast"] - s["gc"])).astype(MXU)
        eg_buf[buf, d, slot * HALO:(slot + 1) * HALO, :] = jnp.broadcast_to(jnp.exp(s["glast"]), (HALO, wd))


def _gdn_scan(steps, rows, st_s, o_refs, bufs, buf):
    c, wd = C_CHUNK, C_WIDTH
    u_buf, w_buf, a_buf, qd_buf, kd_buf, eg_buf = bufs
    bmask = _head_of_lane((C_HEADS * c, wd), 1) == (_iota((C_HEADS * c, wd), 0) // c)
    smask = _head_of_lane((wd, wd), 0) == _head_of_lane((wd, wd), 1)
    states = [st_s[0], st_s[1]]
    for slots in steps:
        src = [slice(slot * c, (slot + 1) * c) for slot in slots]
        ws = [_mm(jnp.concatenate([w_buf[buf, d, r, :], qd_buf[buf, d, r, :]], axis=0), states[d])
              for d, r in enumerate(src)]
        yield
        v_new = [u_buf[buf, d, r, :] - ws[d][:c] for d, r in enumerate(src)]
        upd = [_mm_tn(kd_buf[buf, d, r, :], v_new[d]) for d, r in enumerate(src)]
        for d, (r, slot) in enumerate(zip(src, slots)):
            o_refs[d][rows[d][slot], :] = ws[d][c:] + _mm(a_buf[buf, d, r, :], _block_diag(v_new[d], bmask))
        yield
        states = [states[d] * eg_buf[buf, d, slot * HALO:slot * HALO + 1, :] + jnp.where(smask, upd[d], 0.0)
                  for d, slot in enumerate(slots)]
    st_s[0], st_s[1] = states


def _gdn_kernel(*refs, n_ctx, n_lat, ctx_out):
    (rawc_ref, gtc_ref, zc_ref, rawl_ref, gtl_ref, zl_ref, alog_ref, dtb_ref, nw_ref) = refs[:9]
    nout = 2 if ctx_out else 1
    out_refs = refs[9:9 + nout]
    (q_s, k_s, v_s, bf_s, bb_s, gf_s, gb_s, of_s, ob_s, st_s) = refs[9 + nout:19 + nout]
    scr = (q_s, k_s, v_s, bf_s, bb_s, gf_s, gb_s)
    bufs = refs[19 + nout:]
    _gdn_prep(rawc_ref, gtc_ref, n_ctx, 0, alog_ref, dtb_ref, *scr)
    _gdn_prep(rawl_ref, gtl_ref, n_lat, n_ctx, alog_ref, dtb_ref, *scr)
    st_s[...] = jnp.zeros_like(st_s)
    ncb, nlb = n_ctx // GDN_BLOCK, n_lat // GDN_BLOCK
    per = GDN_BLOCK // C_CHUNK

    nblk = ncb + nlb
    steps = [(t, per - 1 - t) for t in range(per)]

    def block_rows(i):
        i = jnp.asarray(i, jnp.int32)
        bf = pl.multiple_of(i * GDN_BLOCK, GDN_BLOCK)
        bb = pl.multiple_of(jnp.where(i < ncb, ncb - 1 - i, 2 * ncb + nlb - 1 - i) * GDN_BLOCK, GDN_BLOCK)
        return ([pl.ds(bf + r * C_CHUNK, C_CHUNK) for r in range(per)],
                [pl.ds(bb + r * C_CHUNK, C_CHUNK) for r in range(per)])

    def local(i):
        rows_f, rows_b = block_rows(i)
        chains = [c for r in range(per) for c in ((rows_f[r], False, bf_s, gf_s, 0, r), (rows_b[r], True, bb_s, gb_s, 1, r))]
        for lo in range(0, len(chains), GDN_LOCKSTEP):
            yield from _gdn_local(chains[lo:lo + GDN_LOCKSTEP], q_s, k_s, v_s, bufs, i % 2)

    def scan(i):
        return _gdn_scan(steps, block_rows(i), st_s, (of_s, ob_s), bufs, i % 2)

    _interleave([local(0)])

    def block(i, carry):
        _interleave([local(i + 1), scan(i)])
        return carry

    lax.fori_loop(0, nblk - 1, block, 0)
    _interleave([scan(nblk - 1)])

    same_head = _same_head()

    def finish(z_ref, out_ref, off, n):
        rows = GDN_BLOCK

        def blk(i, carry):
            base = pl.multiple_of(i * rows, rows)
            src = pl.ds(pl.multiple_of(off + base, rows), rows)
            o = of_s[src, :] + ob_s[src, :]
            ms = _mm_sel(o * o, same_head) * (1.0 / C_HEAD_DIM)
            y = o * lax.rsqrt(ms + NORM_EPS) * nw_ref[...]
            out_ref[0, pl.ds(base, rows), :] = (y * _silu(z_ref[0, pl.ds(base, rows), :])).astype(out_ref.dtype)
            return carry

        lax.fori_loop(0, n // rows, blk, 0)

    finish(zl_ref, out_refs[0], n_ctx, n_lat)
    if ctx_out:
        finish(zc_ref, out_refs[1], 0, n_ctx)


def _gdn(rawc, gtc, zc, rawl, gtl, zl, alog, dtb, nw, *, ctx_out):
    bsz, n_ctx, _ = rawc.shape
    n_lat = rawl.shape[1]
    nt = n_ctx + n_lat
    seq = lambda a: pl.BlockSpec((1,) + a.shape[1:], lambda b: (b, 0, 0))
    const = lambda a: pl.BlockSpec(a.shape, lambda b: (0,) * a.ndim)
    out_shape = [jax.ShapeDtypeStruct((bsz, n_lat, C_WIDTH), MXU)]
    if ctx_out:
        out_shape.append(jax.ShapeDtypeStruct((bsz, n_ctx, C_WIDTH), MXU))
    big = lambda: pltpu.VMEM((nt, C_WIDTH), F32)
    loc = lambda dt: pltpu.VMEM((2, 2, GDN_BLOCK, C_WIDTH), dt)
    per = GDN_BLOCK // C_CHUNK
    res = pl.pallas_call(
        functools.partial(_gdn_kernel, n_ctx=n_ctx, n_lat=n_lat, ctx_out=ctx_out),
        grid=(bsz,),
        in_specs=[seq(rawc), seq(gtc), seq(zc), seq(rawl), seq(gtl), seq(zl),
                  const(alog), const(dtb), const(nw)],
        out_specs=[seq(s) for s in out_shape],
        out_shape=out_shape,
        scratch_shapes=[big() for _ in range(9)] + [pltpu.VMEM((2, C_WIDTH, C_WIDTH), F32)]
        + [loc(F32), loc(MXU), loc(MXU), loc(MXU), loc(MXU), pltpu.VMEM((2, 2, per * HALO, C_WIDTH), F32)],
        compiler_params=pltpu.CompilerParams(dimension_semantics=("arbitrary",), vmem_limit_bytes=VMEM_LIMIT),
        name="gated_deltanet_bidir",
    )(rawc, gtc, zc, rawl, gtl, zl, alog, dtb, nw)
    return res


def _out_kernel(x_ref, mod_ref, ya_ref, yb_ref, yc_ref, w_ref, fw_ref, o_ref, *, final):
    d = D_MODEL
    y = (jnp.dot(ya_ref[0], w_ref[0:A_WIDTH], preferred_element_type=F32)
         + jnp.dot(yb_ref[0], w_ref[A_WIDTH:A_WIDTH + B_WIDTH], preferred_element_type=F32)
         + jnp.dot(yc_ref[0], w_ref[A_WIDTH + B_WIDTH:], preferred_element_type=F32))
    x = x_ref[0] + mod_ref[0][:, 2 * d:] * y
    if final:
        x = x * lax.rsqrt(jnp.mean(x * x, axis=-1, keepdims=True) + NORM_EPS) * fw_ref[...]
    o_ref[0] = x


def _out(x, mod, ya, yb, yc, w_out, fw, *, final, tile):
    bsz, n, d = x.shape
    mod_map = (lambda b, i: (b, 0, 0)) if mod.shape[0] == bsz else (lambda b, i: (0, 0, 0))
    tok = lambda w: pl.BlockSpec((1, tile, w), lambda b, i: (b, i, 0))
    const = lambda a: pl.BlockSpec(a.shape, lambda b, i: (0,) * a.ndim)
    return pl.pallas_call(
        functools.partial(_out_kernel, final=final),
        grid=(bsz, n // tile),
        in_specs=[tok(d), pl.BlockSpec((1, 1, 3 * d), mod_map), tok(A_WIDTH), tok(B_WIDTH), tok(C_WIDTH),
                  const(w_out), const(fw)],
        out_specs=tok(d),
        out_shape=jax.ShapeDtypeStruct((bsz, n, d), F32),
        compiler_params=pltpu.CompilerParams(dimension_semantics=("arbitrary", "arbitrary"),
                                             vmem_limit_bytes=VMEM_LIMIT),
        name="outproj_residual_final" if final else "outproj_residual",
    )(x, mod, ya, yb, yc, w_out, fw)


def _rope_tables(n):
    pos = jnp.arange(n)
    half = B_HEAD_DIM // 2
    inv_freq = ROPE_THETA ** (-jnp.arange(0, half, 2, dtype=F32) / half)
    lane = np.arange(B_WIDTH)
    dd = lane % B_HEAD_DIM
    use_col = (dd // half) == 1
    second = ((dd % half) // (half // 2)) == 1
    fidx = dd % (half // 2)
    p = jnp.where(use_col[None, :], (pos % GRID_W)[:, None], (pos // GRID_W)[:, None]).astype(F32)
    ang = p * inv_freq[fidx][None, :]
    cos, sin = jnp.cos(ang), jnp.sin(ang)
    return cos, jnp.where(second[None, :], 0.0, -sin), jnp.where(second[None, :], sin, 0.0)


def kernel(x, c, ctx, c_ctx, w_ada, b_ada, norm_w, w_in, w_out, a_ln_w, a_ws, a_bs, b_lam, b_subln_w,
           c_conv_w, c_a_log, c_dt_bias, c_norm_w, final_norm_w):
    bsz, n, d = x.shape
    n_ctx = ctx.shape[1]
    depth = w_ada.shape[0]
    assert d == D_MODEL and n % GDN_BLOCK == 0 and n_ctx % GDN_BLOCK == 0 and n % GRID_W == 0
    tile = ATT_QTILE * ATT_TILES
    tile_ctx = min(ATT_QTILE, n_ctx)
    ptile = min(PROJ_TILE, n)
    otile = next(t for t in (OUT_TILE, PROJ_TILE, GDN_BLOCK) if n % t == 0)
    assert n % ptile == 0 and n % tile == 0

    rows = bsz + 1
    rows_pad = -(-rows // 8) * 8
    c_rows = jnp.concatenate([c, c_ctx[None, :], jnp.zeros((rows_pad - rows, d), F32)], axis=0)
    mods = _ada(c_rows, w_ada, b_ada)

    tabs = _rope_tables(n)
    tabs_ctx = tuple(t[:tile_ctx] for t in tabs)
    gate_pad = jnp.zeros((depth, d, GATE_PAD - N_GATE), w_in.dtype)
    w_in_p = jnp.concatenate([w_in, gate_pad], axis=-1).astype(MXU)
    w_out_b = w_out.astype(MXU)
    aws = a_ws.reshape(depth, A_GROUPS * A_CHUNK, A_CHUNK).astype(MXU)
    abias = jnp.repeat(jnp.swapaxes(a_bs, 1, 2), A_WIDTH // A_GROUPS, axis=2)
    row = lambda a: a.reshape(1, -1)

    xc = ctx
    for i in range(depth):
        lam_init = 0.8 - 0.6 * math.exp(-0.3 * i)
        ctx_out = i < depth - 1
        mod_l = mods[i, :bsz][:, None, :]
        mod_c = mods[i, bsz:bsz + 1][:, None, :]
        common = (row(norm_w[i]), w_in_p[i])
        a_par = (row(a_ln_w[i]), aws[i], abias[i], c_conv_w[i])
        ya, qb, kb, vb, zb, craw, zc, gt = _proj(x, mod_l, *common, tabs, *a_par, rope=True, tile=ptile)
        ya_c, qb_c, kb_c, vb_c, zb_c, craw_c, zc_c, gt_c = _proj(xc, mod_c, *common, tabs_ctx, *a_par,
                                                                 rope=False, tile=tile_ctx)
        lam_p, sw = b_lam[i], row(b_subln_w[i])
        yb = _attn(qb, zb, lam_p, sw, [kb_c, kb], [vb_c, vb], lam_init=lam_init, tile=tile)
        gate_row = lambda p: jnp.concatenate([jnp.zeros((2 * C_HEADS,), F32), p.reshape(-1),
                                              jnp.zeros((GATE_PAD - N_GATE,), F32)])[None, :]
        gdn_par = (gate_row(c_a_log[i]), gate_row(c_dt_bias[i]), jnp.tile(c_norm_w[i], C_HEADS)[None, :])
        yc_all = _gdn(craw_c, gt_c, zc_c, craw, gt, zc, *gdn_par, ctx_out=ctx_out)
        final = i == depth - 1
        x = _out(x, mod_l, ya, yb, yc_all[0], w_out_b[i], row(final_norm_w), final=final, tile=otile)
        if ctx_out:
            yb_c = _attn(qb_c, zb_c, lam_p, sw, [kb_c], [vb_c], lam_init=lam_init, tile=tile_ctx)
            xc = _out(xc, mod_c, ya_c, yb_c, yc_all[1], w_out_b[i], row(final_norm_w), final=False, tile=tile_ctx)
    return x
```

```python
import functools
import math

import numpy as np
import jax
import jax.numpy as jnp
from jax import lax
from jax.experimental import pallas as pl
from jax.experimental.pallas import tpu as pltpu

F32 = jnp.float32
MXU = jnp.bfloat16

D_MODEL = 1024
NORM_EPS = 1e-6
GRID_W = 64
A_WIDTH, A_GROUPS, A_CHUNK, A_LN_EPS = 256, 4, 128, 1e-5
B_HEADS, B_HEAD_DIM, B_WIDTH = 4, 64, 512
ROPE_THETA = 10000.0
C_HEADS, C_HEAD_DIM, C_WIDTH, C_CONV, C_CHUNK = 4, 64, 256, 5, 64
N_GATE = 4 * C_HEADS
GATE_PAD = 128
OFF_A, OFF_B, OFF_C, OFF_G = 0, 3 * A_WIDTH, 3 * A_WIDTH + 4 * B_WIDTH, 3 * A_WIDTH + 4 * B_WIDTH + 4 * C_WIDTH
D_IN_PAD = OFF_G + GATE_PAD
NEG_BIG = -1e30
LOG2E = math.log2(math.e)
ATT_KCHUNK = 256
ATT_SCHUNK = 256
ATT_QTILE = 256
ATT_TILES = 1
PROJ_TILE = 512
OUT_TILE = 2048
CONV_ROWS = 256
VMEM_LIMIT = 56 * 1024 * 1024
GDN_BLOCK = 256
GDN_LOCKSTEP = 8
SUBLANES = 8
HALO = SUBLANES
ADA_TILE = 512
INV_BLOCK = 16


def _mm(a, b):
    return jnp.dot(a.astype(MXU), b.astype(MXU), preferred_element_type=F32)


def _mm_nt(a, b):
    return lax.dot_general(a.astype(MXU), b.astype(MXU), (((1,), (1,)), ((), ())), preferred_element_type=F32)


def _mm_tn(a, b):
    return lax.dot_general(a.astype(MXU), b.astype(MXU), (((0,), (0,)), ((), ())), preferred_element_type=F32)


def _split(a):
    hi = a.astype(MXU)
    return hi, (a - hi.astype(F32)).astype(MXU)


def _mm3(a, b):
    ah, al = _split(a)
    bh, bl = _split(b)
    d = lambda x, y: jnp.dot(x, y, preferred_element_type=F32)
    return d(ah, bh) + (d(ah, bl) + d(al, bh))


def _sigmoid(x):
    return 1.0 / (1.0 + jnp.exp(-x))


def _silu(x):
    return x * _sigmoid(x)


def _iota(shape, dim):
    return lax.broadcasted_iota(jnp.int32, shape, dim)


def _ada_kernel(c_ref, w_ref, b_ref, o_ref):
    o_ref[0] = _mm3(_silu(c_ref[...]), w_ref[0]) + b_ref[0]


def _ada(c_rows, w_ada, b_ada):
    depth, d, d3 = w_ada.shape
    r = c_rows.shape[0]
    tn = ADA_TILE
    return pl.pallas_call(
        _ada_kernel,
        grid=(depth, d3 // tn),
        in_specs=[pl.BlockSpec((r, d), lambda l, j: (0, 0)),
                  pl.BlockSpec((1, d, tn), lambda l, j: (l, 0, j)),
                  pl.BlockSpec((1, 1, tn), lambda l, j: (l, 0, j))],
        out_specs=pl.BlockSpec((1, r, tn), lambda l, j: (l, 0, j)),
        out_shape=jax.ShapeDtypeStruct((depth, r, d3), F32),
        name="ada_modulation",
    )(c_rows, w_ada, b_ada.reshape(depth, 1, d3))


def _proj_kernel(x_ref, xp_ref, xn_ref, mod_ref, nw_ref, w_ref, cos_ref, sinn_ref, sinp_ref, lnw_ref, aws_ref,
                 abias_ref, convw_ref, ya_ref, q_ref, k_ref, v_ref, zb_ref, cqkv_ref, zc_ref, g_ref, *, rope):
    d = D_MODEL
    mod = mod_ref[0]

    def normed(x):
        y = x * lax.rsqrt(jnp.mean(x * x, axis=-1, keepdims=True) + NORM_EPS) * nw_ref[...]
        return (y * (1.0 + mod[:, d:2 * d]) + mod[:, :d]).astype(MXU)

    t = x_ref.shape[1]
    half_t = t // 2
    proj = lambda lhs, lo, hi: jnp.dot(lhs, w_ref[:, lo:hi], preferred_element_type=F32)
    vals = {}
    h0 = normed(x_ref[0, :half_t])
    pc0 = proj(jnp.concatenate([h0, normed(xp_ref[0])], axis=0), OFF_C, OFF_G)
    h1 = normed(x_ref[0, half_t:])
    pc1 = proj(jnp.concatenate([h1, normed(xn_ref[0])], axis=0), OFF_C, OFF_G)
    h = jnp.concatenate([h0, h1], axis=0)

    def finish_a():
        pa = vals.pop("a")
        u, v, z = pa[:, :A_WIDTH], pa[:, A_WIDTH:2 * A_WIDTH], pa[:, 2 * A_WIDTH:]
        vc = v - jnp.mean(v, axis=-1, keepdims=True)
        vn = (vc * lax.rsqrt(jnp.mean(vc * vc, axis=-1, keepdims=True) + A_LN_EPS) * lnw_ref[...]).astype(MXU)
        grp = _iota((A_CHUNK, A_WIDTH), 1) // (A_WIDTH // A_GROUPS)
        for c in range(t // A_CHUNK):
            rows = slice(c * A_CHUNK, (c + 1) * A_CHUNK)
            r = jnp.dot(aws_ref[...], vn[rows], preferred_element_type=F32)
            s = abias_ref[...]
            for g in range(A_GROUPS):
                s = s + jnp.where(grp == g, r[g * A_CHUNK:(g + 1) * A_CHUNK], 0.0)
            ya_ref[0, rows, :] = (u[rows] * s * _silu(z[rows])).astype(ya_ref.dtype)

    def rot(a):
        if not rope:
            return a
        half = B_HEAD_DIM // 4
        return (a * cos_ref[...] + pltpu.roll(a, B_WIDTH - half, 1) * sinn_ref[...]
                + pltpu.roll(a, half, 1) * sinp_ref[...])

    def finish_q():
        q_ref[0] = (rot(vals.pop("q")) * (B_HEAD_DIM ** -0.5 * LOG2E)).astype(q_ref.dtype)

    def finish_k():
        k_ref[0] = rot(vals.pop("k")).astype(k_ref.dtype)

    def finish_v():
        v_ref[0] = vals.pop("v").T.astype(v_ref.dtype)

    def finish_z():
        zb_ref[0] = vals.pop("z").astype(zb_ref.dtype)

    sub = min(t, CONV_ROWS)

    def start_c():
        i, last = pl.program_id(0), pl.num_programs(0) - 1
        zc_ref[0, :half_t, :] = pc0[:half_t, 3 * C_WIDTH:]
        zc_ref[0, half_t:, :] = pc1[:half_t, 3 * C_WIDTH:]
        prev = jnp.where(i == 0, 0.0, pc0[half_t:, :3 * C_WIDTH])
        nxt = jnp.where(i == last, 0.0, pc1[half_t:, :3 * C_WIDTH])
        vals["c"] = jnp.concatenate([prev, pc0[:half_t, :3 * C_WIDTH], pc1[:half_t, :3 * C_WIDTH], nxt], axis=0)

    def finish_c(r0):
        w = convw_ref[...]
        pad = C_CONV // 2
        win = vals["c"][r0:r0 + sub + 2 * HALO]
        acc = win[HALO:HALO + sub] * w[pad:pad + 1]
        for j in range(C_CONV):
            if j != pad:
                acc = acc + pltpu.roll(win, (pad - j) % (sub + 2 * HALO), 0)[HALO:HALO + sub] * w[j:j + 1]
        act = _silu(acc)
        qh, kh = act[:, :C_WIDTH], act[:, C_WIDTH:2 * C_WIDTH]
        l2 = lambda a: a * lax.rsqrt(_mm_sel(a * a, _same_head()) + 1e-6)
        cqkv_ref[0, r0:r0 + sub, 0:C_WIDTH] = l2(qh) * (C_HEAD_DIM ** -0.5)
        cqkv_ref[0, r0:r0 + sub, C_WIDTH:2 * C_WIDTH] = l2(kh)
        cqkv_ref[0, r0:r0 + sub, 2 * C_WIDTH:] = act[:, 2 * C_WIDTH:]

    def start(name, lo, hi):
        def go():
            vals[name] = proj(h, lo, hi)
        return go

    def finish_g():
        g_ref[0] = vals.pop("g")

    bq = OFF_B
    order = [start_c, start("a", OFF_A, OFF_B)]
    convs = [functools.partial(finish_c, r0) for r0 in range(0, t, sub)]
    later = [start("q", bq, bq + B_WIDTH), start("k", bq + B_WIDTH, bq + 2 * B_WIDTH),
             start("v", bq + 2 * B_WIDTH, bq + 3 * B_WIDTH), start("z", bq + 3 * B_WIDTH, OFF_C),
             start("g", OFF_G, D_IN_PAD)]
    fins = convs + [finish_a, finish_q, finish_k, finish_v, finish_z, finish_g]
    while later or fins:
        if later:
            order.append(later.pop(0))
        if fins:
            order.append(fins.pop(0))
    for step in order:
        step()


def _proj(x, mod, nw, w_in, tabs, lnw, aws, abias, convw, *, rope, tile):
    bsz, n, d = x.shape
    cos, sinn, sinp = tabs
    mod_map = (lambda i, b: (b, 0, 0)) if mod.shape[0] == bsz else (lambda i, b: (0, 0, 0))
    tok = lambda w: pl.BlockSpec((1, tile, w), lambda i, b: (b, i, 0))
    per = tile // HALO
    before = pl.BlockSpec((1, HALO, d), lambda i, b: (b, jnp.maximum(i * per - 1, 0), 0))
    after = pl.BlockSpec((1, HALO, d), lambda i, b: (b, jnp.minimum((i + 1) * per, n // HALO - 1), 0))
    const = lambda a: pl.BlockSpec(a.shape, lambda i, b: (0,) * a.ndim)
    tab = pl.BlockSpec((tile, B_WIDTH), lambda i, b: (i, 0))
    outs = [(A_WIDTH, MXU), (B_WIDTH, MXU), (B_WIDTH, MXU), (B_WIDTH, MXU), (B_WIDTH, F32),
            (3 * C_WIDTH, F32), (C_WIDTH, F32), (GATE_PAD, F32)]
    out_specs = [tok(w) for w, _ in outs]
    out_shape = [jax.ShapeDtypeStruct((bsz, n, w), dt) for w, dt in outs]
    out_specs[3] = pl.BlockSpec((1, B_WIDTH, tile), lambda i, b: (b, 0, i))
    out_shape[3] = jax.ShapeDtypeStruct((bsz, B_WIDTH, n), MXU)
    return pl.pallas_call(
        functools.partial(_proj_kernel, rope=rope),
        grid=(n // tile, bsz),
        in_specs=[tok(d), before, after, pl.BlockSpec((1, 1, 3 * d), mod_map), const(nw), const(w_in), tab, tab, tab,
                  const(lnw), const(aws), const(abias), const(convw)],
        out_specs=out_specs,
        out_shape=out_shape,
        compiler_params=pltpu.CompilerParams(dimension_semantics=("arbitrary", "arbitrary"),
                                             vmem_limit_bytes=VMEM_LIMIT),
        name="norm_inproj_rope" if rope else "norm_inproj_ctx",
    )(x, x, x, mod, nw, w_in, cos, sinn, sinp, lnw, aws, abias, convw)


def _interleave(gens):
    gens = list(gens)
    while gens:
        for g in list(gens):
            if next(g, StopIteration) is StopIteration:
                gens.remove(g)


def _attn_kernel(*refs, seg_lens, lam_init):
    nseg = len(seg_lens)
    q_ref, z_ref, lam_ref, sw_ref = refs[:4]
    k_refs, v_refs = refs[4:4 + nseg], refs[4 + nseg:4 + 2 * nseg]
    o_ref, s_buf, e_buf = refs[4 + 2 * nseg:]
    lf = lam_ref[...]
    lam = (jnp.exp(jnp.sum(lf[0:1] * lf[1:2], axis=-1, keepdims=True))
           - jnp.exp(jnp.sum(lf[2:3] * lf[3:4], axis=-1, keepdims=True)) + lam_init)
    tq = ATT_QTILE
    ntile = q_ref.shape[1] // tq
    pair = 2 * B_HEAD_DIM
    first = _iota((1, pair), 1) < B_HEAD_DIM
    zero = jnp.zeros((), q_ref.dtype)
    chunks, row = [], 0
    for seg, n in enumerate(seg_lens):
        for c0 in range(0, n, ATT_KCHUNK):
            chunks.append((seg, c0, row))
            row += ATT_KCHUNK
    kc = ATT_KCHUNK
    maxes, heads = {}, {}

    dens = {}

    def scores(i):
        hh, j = divmod(i, 2)
        qt, hp = divmod(hh, B_HEADS)
        lanes = slice(hp * pair, (hp + 1) * pair)
        qm = jnp.where(first if j == 0 else jnp.logical_not(first), q_ref[0, qt * tq:(qt + 1) * tq, lanes], zero)
        m, row = None, 0
        for seg, n in enumerate(seg_lens):
            for c0 in range(0, n, ATT_SCHUNK):
                size = min(ATT_SCHUNK, n - c0)
                s = _mm_nt(k_refs[seg][0, c0:c0 + size, lanes], qm)
                for r in range(0, size, kc):
                    s_buf[hh % 2, j, row:row + kc, :] = s[r:r + kc]
                    cm = jnp.max(s[r:r + kc].reshape(kc // SUBLANES, SUBLANES, tq), axis=0)
                    m = cm if m is None else jnp.maximum(m, cm)
                    row += kc
                    yield
        maxes[i] = jnp.max(m, axis=0, keepdims=True)

    def exps(i):
        hh, j = divmod(i, 2)
        den = None
        for _, _, r0 in chunks:
            e = jnp.exp2(s_buf[hh % 2, j, r0:r0 + kc, :] - maxes[i])
            e_buf[hh % 2, j, r0:r0 + kc, :] = e.astype(MXU)
            cs = jnp.sum(e.reshape(kc // SUBLANES, SUBLANES, tq), axis=0)
            den = cs if den is None else den + cs
            yield
        dens[i] = jnp.sum(den, axis=0, keepdims=True)

    def values(hh):
        qt, hp = divmod(hh, B_HEADS)
        lanes = slice(hp * pair, (hp + 1) * pair)
        d0, d1 = dens[2 * hh], dens[2 * hh + 1]
        coef = (lam * d0 * (1.0 / d1)).astype(MXU)
        pv = None
        for seg, c0, r0 in chunks:
            p = e_buf[hh % 2, 0, r0:r0 + kc, :] - coef * e_buf[hh % 2, 1, r0:r0 + kc, :]
            t = _mm(v_refs[seg][0, lanes, c0:c0 + kc], p)
            pv = t if pv is None else pv + t
            yield
        acc = (pv * (1.0 / d0)).T
        o = acc * lax.rsqrt(jnp.mean(acc * acc, axis=-1, keepdims=True) + NORM_EPS) * sw_ref[...]
        heads[hh] = o * (1.0 - lam_init)
        if hp == B_HEADS - 1:
            rows = slice(qt * tq, (qt + 1) * tq)
            o_all = jnp.concatenate([heads.pop(qt * B_HEADS + p) for p in range(B_HEADS)], axis=-1)
            o_ref[0, rows, :] = (o_all * _silu(z_ref[0, rows, :])).astype(o_ref.dtype)

    nh = 2 * B_HEADS * ntile
    for t in range(nh + 2):
        stage = []
        if t < nh:
            stage.append(scores(t))
        if 1 <= t <= nh:
            stage.append(exps(t - 1))
        if t >= 3 and t % 2 == 1:
            stage.append(values((t - 3) // 2))
        _interleave(stage)


def _attn(q, z, lam_p, subln_w, ks, vs, *, lam_init, tile):
    bsz, n, w = q.shape
    seg_lens = tuple(a.shape[1] for a in ks)
    tok = pl.BlockSpec((1, tile, w), lambda b, i: (b, i, 0))
    const = lambda a: pl.BlockSpec(a.shape, lambda b, i: (0,) * a.ndim)
    full = lambda a: pl.BlockSpec((1,) + a.shape[1:], lambda b, i: (b, 0, 0))
    return pl.pallas_call(
        functools.partial(_attn_kernel, seg_lens=seg_lens, lam_init=lam_init),
        grid=(bsz, n // tile),
        in_specs=[tok, tok, const(lam_p), const(subln_w)] + [full(a) for a in ks] + [full(a) for a in vs],
        out_specs=tok,
        out_shape=jax.ShapeDtypeStruct((bsz, n, w), MXU),
        scratch_shapes=[pltpu.VMEM((2, 2, sum(seg_lens), ATT_QTILE), F32),
                        pltpu.VMEM((2, 2, sum(seg_lens), ATT_QTILE), MXU)],
        compiler_params=pltpu.CompilerParams(dimension_semantics=("arbitrary", "arbitrary"),
                                             vmem_limit_bytes=VMEM_LIMIT),
        name="diff_attention_%dseg" % len(ks),
    )(q, z, lam_p, subln_w, *ks, *vs)


def _head_of_lane(shape, dim):
    return _iota(shape, dim) // C_HEAD_DIM


def _expand_heads(src, first):
    r, w = src.shape
    lane = _iota((r, w), 1)
    halves = [jnp.take_along_axis(src, first + (lane + off) // C_HEAD_DIM, axis=1) for off in range(0, C_WIDTH, w)]
    return jnp.concatenate(halves, axis=1)


def _split3(a):
    a1 = a.astype(MXU)
    r1 = a - a1.astype(F32)
    a2 = r1.astype(MXU)
    return a1, a2, (r1 - a2.astype(F32)).astype(MXU)


def _mm_sel(a, sel):
    hi, lo = _split(a)
    return jnp.dot(hi, sel, preferred_element_type=F32) + jnp.dot(lo, sel, preferred_element_type=F32)


def _same_head():
    return (_head_of_lane((C_WIDTH, C_WIDTH), 0) == _head_of_lane((C_WIDTH, C_WIDTH), 1)).astype(MXU)


def _gdn_prep(qkv_ref, gate_ref, n, off, alog_ref, dtb_ref, q_s, k_s, v_s, bf_s, bb_s, gf_s, gb_s):
    rows = GDN_BLOCK
    nblk = n // rows
    ri, rj = _iota((rows, rows), 0), _iota((rows, rows), 1)
    same_chunk = (ri // C_CHUNK) == (rj // C_CHUNK)
    csum = jnp.concatenate([(same_chunk & (rj <= ri)).astype(MXU), (same_chunk & (rj >= ri)).astype(MXU)], axis=0)

    def block(i, carry):
        base = pl.multiple_of(i * rows, rows)
        dst = pl.ds(pl.multiple_of(off + base, rows), rows)
        q_s[dst, :] = qkv_ref[0, pl.ds(base, rows), 0:C_WIDTH]
        k_s[dst, :] = qkv_ref[0, pl.ds(base, rows), C_WIDTH:2 * C_WIDTH]
        v_s[dst, :] = qkv_ref[0, pl.ds(base, rows), 2 * C_WIDTH:]
        gt = gate_ref[0, pl.ds(base, rows), :]
        h = C_HEADS
        beta = _sigmoid(gt)
        bf_s[dst, :] = _expand_heads(beta, 0)
        bb_s[dst, :] = _expand_heads(beta, h)
        sp = jnp.maximum(gt + dtb_ref[...], 0.0) + jnp.log1p(jnp.exp(-jnp.abs(gt + dtb_ref[...])))
        g = -jnp.exp(alog_ref[...]) * sp
        w = GATE_PAD
        r = jnp.dot(csum, jnp.concatenate(_split3(g), axis=1), preferred_element_type=F32)
        gc = r[:, :w] + (r[:, w:2 * w] + r[:, 2 * w:])
        gf_s[dst, :] = _expand_heads(gc[:rows], 2 * h)
        gb_s[dst, :] = _expand_heads(gc[rows:], 3 * h)
        return carry

    lax.fori_loop(0, nblk, block, 0)


def _block_diag(a, bmask):
    return jnp.where(bmask, jnp.concatenate([a] * C_HEADS, axis=0), 0.0)


def _gdn_local(chains, q_s, k_s, v_s, bufs, buf):
    c, wd = C_CHUNK, C_WIDTH
    u_buf, w_buf, a_buf, qd_buf, kd_buf, eg_buf = bufs
    ii = _iota((c, wd), 0)
    jj = _iota((c, wd), 1) % c
    eye = jj == ii
    bmask = _head_of_lane((C_HEADS * c, wd), 1) == (_iota((C_HEADS * c, wd), 0) // c)
    bd = lambda a: _block_diag(a, bmask)
    ident = jnp.where(eye, 1.0, 0.0)

    st = []
    for rows, backward, beta_s, gc_s, _, _ in chains:
        q, k, v, beta, gc = q_s[rows, :], k_s[rows, :], v_s[rows, :], beta_s[rows, :], gc_s[rows, :]
        incl = (jj >= ii) if backward else (jj <= ii)
        strict = (jj > ii) if backward else (jj < ii)
        grow = jnp.sum(jnp.where(eye, gc, 0.0), axis=0, keepdims=True)
        decay = jnp.exp(jnp.where(incl, gc - grow, NEG_BIG))
        glast = gc[0:1] if backward else gc[c - 1:c]
        st.append(dict(q=q, k=k, v=v, beta=beta, gc=gc, decay=decay, strict=strict, glast=glast, kb=k * beta,
                       egc=jnp.exp(gc)))
    yield
    for s in st:
        qk = _mm_nt(jnp.concatenate([s["kb"], s["q"]], axis=0), bd(s["k"]))
        s["lmat"] = jnp.where(s["strict"], qk[:c] * s["decay"], 0.0)
        s["attn"] = qk[c:] * s["decay"]
    yield
    in_blk = (ii // INV_BLOCK) == (jj // INV_BLOCK)
    for s in st:
        dmat = jnp.where(in_blk, s["lmat"], 0.0)
        s["t"] = ident - dmat
        s["pw"] = _mm(dmat, bd(dmat))
    yield
    nsq = int(math.log2(INV_BLOCK)) - 1
    for step in range(nsq):
        for s in st:
            b = bd(s["pw"])
            if step < nsq - 1:
                r = _mm(jnp.concatenate([s["t"], s["pw"]], axis=0), b)
                s["t"], s["pw"] = s["t"] + r[:c], r[c:]
            else:
                s["t"] = s["t"] + _mm(s["t"], b)
        yield
    size = INV_BLOCK
    while size < c:
        emask = ((ii // (2 * size)) == (jj // (2 * size))) & ((ii // size) != (jj // size))
        for s in st:
            s["y"] = _mm(jnp.where(emask, s["lmat"], 0.0), bd(s["t"]))
        yield
        for s in st:
            s["t"] = s["t"] - _mm(s["t"], bd(s["y"]))
        yield
        size *= 2
    for s, (_, _, _, _, d, slot) in zip(st, chains):
        dst = slice(slot * c, (slot + 1) * c)
        u_buf[buf, d, dst, :] = _mm(s["t"], bd(s["v"] * s["beta"]))
        a_buf[buf, d, dst, :] = s["attn"].astype(MXU)
        qd_buf[buf, d, dst, :] = (s["q"] * s["egc"]).astype(MXU)
    yield
    for s, (_, _, _, _, d, slot) in zip(st, chains):
        dst = slice(slot * c, (slot + 1) * c)
        w_buf[buf, d, dst, :] = _mm(s["t"], bd(s["kb"] * s["egc"])).astype(MXU)
        kd_buf[buf, d, dst, :] = (s["k"] * jnp.exp(s["glast"] - s["gc"])).astype(MXU)
        eg_buf[buf, d, slot * HALO:(slot + 1) * HALO, :] = jnp.broadcast_to(jnp.exp(s["glast"]), (HALO, wd))


def _gdn_scan(steps, rows, st_s, o_refs, bufs, buf):
    c, wd = C_CHUNK, C_WIDTH
    u_buf, w_buf, a_buf, qd_buf, kd_buf, eg_buf = bufs
    bmask = _head_of_lane((C_HEADS * c, wd), 1) == (_iota((C_HEADS * c, wd), 0) // c)
    smask = _head_of_lane((wd, wd), 0) == _head_of_lane((wd, wd), 1)
    states = [st_s[0], st_s[1]]
    for slots in steps:
        src = [slice(slot * c, (slot + 1) * c) for slot in slots]
        ws = [_mm(jnp.concatenate([w_buf[buf, d, r, :], qd_buf[buf, d, r, :]], axis=0), states[d])
              for d, r in enumerate(src)]
        yield
        v_new = [u_buf[buf, d, r, :] - ws[d][:c] for d, r in enumerate(src)]
        upd = [_mm_tn(kd_buf[buf, d, r, :], v_new[d]) for d, r in enumerate(src)]
        for d, (r, slot) in enumerate(zip(src, slots)):
            o_refs[d][rows[d][slot], :] = ws[d][c:] + _mm(a_buf[buf, d, r, :], _block_diag(v_new[d], bmask))
        yield
        states = [states[d] * eg_buf[buf, d, slot * HALO:slot * HALO + 1, :] + jnp.where(smask, upd[d], 0.0)
                  for d, slot in enumerate(slots)]
    st_s[0], st_s[1] = states


def _gdn_kernel(*refs, n_ctx, n_lat, ctx_out):
    (rawc_ref, gtc_ref, zc_ref, rawl_ref, gtl_ref, zl_ref, alog_ref, dtb_ref, nw_ref) = refs[:9]
    nout = 2 if ctx_out else 1
    out_refs = refs[9:9 + nout]
    (q_s, k_s, v_s, bf_s, bb_s, gf_s, gb_s, of_s, ob_s, st_s) = refs[9 + nout:19 + nout]
    scr = (q_s, k_s, v_s, bf_s, bb_s, gf_s, gb_s)
    bufs = refs[19 + nout:]
    _gdn_prep(rawc_ref, gtc_ref, n_ctx, 0, alog_ref, dtb_ref, *scr)
    _gdn_prep(rawl_ref, gtl_ref, n_lat, n_ctx, alog_ref, dtb_ref, *scr)
    st_s[...] = jnp.zeros_like(st_s)
    ncb, nlb = n_ctx // GDN_BLOCK, n_lat // GDN_BLOCK
    per = GDN_BLOCK // C_CHUNK

    nblk = ncb + nlb
    steps = [(t, per - 1 - t) for t in range(per)]

    def block_rows(i):
        i = jnp.asarray(i, jnp.int32)
        bf = pl.multiple_of(i * GDN_BLOCK, GDN_BLOCK)
        bb = pl.multiple_of(jnp.where(i < ncb, ncb - 1 - i, 2 * ncb + nlb - 1 - i) * GDN_BLOCK, GDN_BLOCK)
        return ([pl.ds(bf + r * C_CHUNK, C_CHUNK) for r in range(per)],
                [pl.ds(bb + r * C_CHUNK, C_CHUNK) for r in range(per)])

    def local(i):
        rows_f, rows_b = block_rows(i)
        chains = [c for r in range(per) for c in ((rows_f[r], False, bf_s, gf_s, 0, r), (rows_b[r], True, bb_s, gb_s, 1, r))]
        for lo in range(0, len(chains), GDN_LOCKSTEP):
            yield from _gdn_local(chains[lo:lo + GDN_LOCKSTEP], q_s, k_s, v_s, bufs, i % 2)

    def scan(i):
        return _gdn_scan(steps, block_rows(i), st_s, (of_s, ob_s), bufs, i % 2)

    _interleave([local(0)])

    def block(i, carry):
        _interleave([local(i + 1), scan(i)])
        return carry

    lax.fori_loop(0, nblk - 1, block, 0)
    _interleave([scan(nblk - 1)])

    same_head = _same_head()

    def finish(z_ref, out_ref, off, n):
        rows = GDN_BLOCK

        def blk(i, carry):
            base = pl.multiple_of(i * rows, rows)
            src = pl.ds(pl.multiple_of(off + base, rows), rows)
            o = of_s[src, :] + ob_s[src, :]
            ms = _mm_sel(o * o, same_head) * (1.0 / C_HEAD_DIM)
            y = o * lax.rsqrt(ms + NORM_EPS) * nw_ref[...]
            out_ref[0, pl.ds(base, rows), :] = (y * _silu(z_ref[0, pl.ds(base, rows), :])).astype(out_ref.dtype)
            return carry

        lax.fori_loop(0, n // rows, blk, 0)

    finish(zl_ref, out_refs[0], n_ctx, n_lat)
    if ctx_out:
        finish(zc_ref, out_refs[1], 0, n_ctx)


def _gdn(rawc, gtc, zc, rawl, gtl, zl, alog, dtb, nw, *, ctx_out):
    bsz, n_ctx, _ = rawc.shape
    n_lat = rawl.shape[1]
    nt = n_ctx + n_lat
    seq = lambda a: pl.BlockSpec((1,) + a.shape[1:], lambda b: (b, 0, 0))
    const = lambda a: pl.BlockSpec(a.shape, lambda b: (0,) * a.ndim)
    out_shape = [jax.ShapeDtypeStruct((bsz, n_lat, C_WIDTH), MXU)]
    if ctx_out:
        out_shape.append(jax.ShapeDtypeStruct((bsz, n_ctx, C_WIDTH), MXU))
    big = lambda: pltpu.VMEM((nt, C_WIDTH), F32)
    loc = lambda dt: pltpu.VMEM((2, 2, GDN_BLOCK, C_WIDTH), dt)
    per = GDN_BLOCK // C_CHUNK
    res = pl.pallas_call(
        functools.partial(_gdn_kernel, n_ctx=n_ctx, n_lat=n_lat, ctx_out=ctx_out),
        grid=(bsz,),
        in_specs=[seq(rawc), seq(gtc), seq(zc), seq(rawl), seq(gtl), seq(zl),
                  const(alog), const(dtb), const(nw)],
        out_specs=[seq(s) for s in out_shape],
        out_shape=out_shape,
        scratch_shapes=[big() for _ in range(9)] + [pltpu.VMEM((2, C_WIDTH, C_WIDTH), F32)]
        + [loc(F32), loc(MXU), loc(MXU), loc(MXU), loc(MXU), pltpu.VMEM((2, 2, per * HALO, C_WIDTH), F32)],
        compiler_params=pltpu.CompilerParams(dimension_semantics=("arbitrary",), vmem_limit_bytes=VMEM_LIMIT),
        name="gated_deltanet_bidir",
    )(rawc, gtc, zc, rawl, gtl, zl, alog, dtb, nw)
    return res


def _out_kernel(x_ref, mod_ref, ya_ref, yb_ref, yc_ref, w_ref, fw_ref, o_ref, *, final):
    d = D_MODEL
    y = (jnp.dot(ya_ref[0], w_ref[0:A_WIDTH], preferred_element_type=F32)
         + jnp.dot(yb_ref[0], w_ref[A_WIDTH:A_WIDTH + B_WIDTH], preferred_element_type=F32)
         + jnp.dot(yc_ref[0], w_ref[A_WIDTH + B_WIDTH:], preferred_element_type=F32))
    x = x_ref[0] + mod_ref[0][:, 2 * d:] * y
    if final:
        x = x * lax.rsqrt(jnp.mean(x * x, axis=-1, keepdims=True) + NORM_EPS) * fw_ref[...]
    o_ref[0] = x


def _out(x, mod, ya, yb, yc, w_out, fw, *, final, tile):
    bsz, n, d = x.shape
    mod_map = (lambda b, i: (b, 0, 0)) if mod.shape[0] == bsz else (lambda b, i: (0, 0, 0))
    tok = lambda w: pl.BlockSpec((1, tile, w), lambda b, i: (b, i, 0))
    const = lambda a: pl.BlockSpec(a.shape, lambda b, i: (0,) * a.ndim)
    return pl.pallas_call(
        functools.partial(_out_kernel, final=final),
        grid=(bsz, n // tile),
        in_specs=[tok(d), pl.BlockSpec((1, 1, 3 * d), mod_map), tok(A_WIDTH), tok(B_WIDTH), tok(C_WIDTH),
                  const(w_out), const(fw)],
        out_specs=tok(d),
        out_shape=jax.ShapeDtypeStruct((bsz, n, d), F32),
        compiler_params=pltpu.CompilerParams(dimension_semantics=("arbitrary", "arbitrary"),
                                             vmem_limit_bytes=VMEM_LIMIT),
        name="outproj_residual_final" if final else "outproj_residual",
    )(x, mod, ya, yb, yc, w_out, fw)


def _rope_tables(n):
    pos = jnp.arange(n)
    half = B_HEAD_DIM // 2
    inv_freq = ROPE_THETA ** (-jnp.arange(0, half, 2, dtype=F32) / half)
    lane = np.arange(B_WIDTH)
    dd = lane % B_HEAD_DIM
    use_col = (dd // half) == 1
    second = ((dd % half) // (half // 2)) == 1
    fidx = dd % (half // 2)
    p = jnp.where(use_col[None, :], (pos % GRID_W)[:, None], (pos // GRID_W)[:, None]).astype(F32)
    ang = p * inv_freq[fidx][None, :]
    cos, sin = jnp.cos(ang), jnp.sin(ang)
    return cos, jnp.where(second[None, :], 0.0, -sin), jnp.where(second[None, :], sin, 0.0)


def kernel(x, c, ctx, c_ctx, w_ada, b_ada, norm_w, w_in, w_out, a_ln_w, a_ws, a_bs, b_lam, b_subln_w,
           c_conv_w, c_a_log, c_dt_bias, c_norm_w, final_norm_w):
    bsz, n, d = x.shape
    n_ctx = ctx.shape[1]
    depth = w_ada.shape[0]
    assert d == D_MODEL and n % GDN_BLOCK == 0 and n_ctx % GDN_BLOCK == 0 and n % GRID_W == 0
    tile = ATT_QTILE * ATT_TILES
    tile_ctx = min(ATT_QTILE, n_ctx)
    ptile = min(PROJ_TILE, n)
    otile = next(t for t in (OUT_TILE, PROJ_TILE, GDN_BLOCK) if n % t == 0)
    assert n % ptile == 0 and n % tile == 0

    rows = bsz + 1
    rows_pad = -(-rows // 8) * 8
    c_rows = jnp.concatenate([c, c_ctx[None, :], jnp.zeros((rows_pad - rows, d), F32)], axis=0)
    mods = _ada(c_rows, w_ada, b_ada)

    tabs = _rope_tables(n)
    tabs_ctx = tuple(t[:tile_ctx] for t in tabs)
    gate_pad = jnp.zeros((depth, d, GATE_PAD - N_GATE), w_in.dtype)
    w_in_p = jnp.concatenate([w_in, gate_pad], axis=-1).astype(MXU)
    w_out_b = w_out.astype(MXU)
    aws = a_ws.reshape(depth, A_GROUPS * A_CHUNK, A_CHUNK).astype(MXU)
    abias = jnp.repeat(jnp.swapaxes(a_bs, 1, 2), A_WIDTH // A_GROUPS, axis=2)
    row = lambda a: a.reshape(1, -1)

    xc = ctx
    for i in range(depth):
        lam_init = 0.8 - 0.6 * math.exp(-0.3 * i)
        ctx_out = i < depth - 1
        mod_l = mods[i, :bsz][:, None, :]
        mod_c = mods[i, bsz:bsz + 1][:, None, :]
        common = (row(norm_w[i]), w_in_p[i])
        a_par = (row(a_ln_w[i]), aws[i], abias[i], c_conv_w[i])
        ya, qb, kb, vb, zb, craw, zc, gt = _proj(x, mod_l, *common, tabs, *a_par, rope=True, tile=ptile)
        ya_c, qb_c, kb_c, vb_c, zb_c, craw_c, zc_c, gt_c = _proj(xc, mod_c, *common, tabs_ctx, *a_par,
                                                                 rope=False, tile=tile_ctx)
        lam_p, sw = b_lam[i], row(b_subln_w[i])
        yb = _attn(qb, zb, lam_p, sw, [kb_c, kb], [vb_c, vb], lam_init=lam_init, tile=tile)
        gate_row = lambda p: jnp.concatenate([jnp.zeros((2 * C_HEADS,), F32), p.reshape(-1),
                                              jnp.zeros((GATE_PAD - N_GATE,), F32)])[None, :]
        gdn_par = (gate_row(c_a_log[i]), gate_row(c_dt_bias[i]), jnp.tile(c_norm_w[i], C_HEADS)[None, :])
        yc_all = _gdn(craw_c, gt_c, zc_c, craw, gt, zc, *gdn_par, ctx_out=ctx_out)
        final = i == depth - 1
        x = _out(x, mod_l, ya, yb, yc_all[0], w_out_b[i], row(final_norm_w), final=final, tile=otile)
        if ctx_out:
            yb_c = _attn(qb_c, zb_c, lam_p, sw, [kb_c], [vb_c], lam_init=lam_init, tile=tile_ctx)
            xc = _out(xc, mod_c, ya_c, yb_c, yc_all[1], w_out_b[i], row(final_norm_w), final=False, tile=tile_ctx)
    return x
```

```python
import functools
import math

import numpy as np
import jax
import jax.numpy as jnp
from jax import lax
from jax.experimental import pallas as pl
from jax.experimental.pallas import tpu as pltpu

F32 = jnp.float32
MXU = jnp.bfloat16

D_MODEL = 1024
NORM_EPS = 1e-6
GRID_W = 64
A_WIDTH, A_GROUPS, A_CHUNK, A_LN_EPS = 256, 4, 128, 1e-5
B_HEADS, B_HEAD_DIM, B_WIDTH = 4, 64, 512
ROPE_THETA = 10000.0
C_HEADS, C_HEAD_DIM, C_WIDTH, C_CONV, C_CHUNK = 4, 64, 256, 5, 64
N_GATE = 4 * C_HEADS
GATE_PAD = 128
OFF_A, OFF_B, OFF_C, OFF_G = 0, 3 * A_WIDTH, 3 * A_WIDTH + 4 * B_WIDTH, 3 * A_WIDTH + 4 * B_WIDTH + 4 * C_WIDTH
D_IN_PAD = OFF_G + GATE_PAD
NEG_BIG = -1e30
LOG2E = math.log2(math.e)
ATT_KCHUNK = 256
ATT_SCHUNK = 256
ATT_QTILE = 256
ATT_TILES = 1
PROJ_TILE = 512
OUT_TILE = 2048
CONV_ROWS = 256
VMEM_LIMIT = 56 * 1024 * 1024
GDN_BLOCK = 256
GDN_LOCKSTEP = 8
SUBLANES = 8
HALO = SUBLANES
ADA_TILE = 512
INV_BLOCK = 16


def _mm(a, b):
    return jnp.dot(a.astype(MXU), b.astype(MXU), preferred_element_type=F32)


def _mm_nt(a, b):
    return lax.dot_general(a.astype(MXU), b.astype(MXU), (((1,), (1,)), ((), ())), preferred_element_type=F32)


def _mm_tn(a, b):
    return lax.dot_general(a.astype(MXU), b.astype(MXU), (((0,), (0,)), ((), ())), preferred_element_type=F32)


def _split(a):
    hi = a.astype(MXU)
    return hi, (a - hi.astype(F32)).astype(MXU)


def _mm3(a, b):
    ah, al = _split(a)
    bh, bl = _split(b)
    d = lambda x, y: jnp.dot(x, y, preferred_element_type=F32)
    return d(ah, bh) + (d(ah, bl) + d(al, bh))


def _sigmoid(x):
    return 1.0 / (1.0 + jnp.exp(-x))


def _silu(x):
    return x * _sigmoid(x)


def _iota(shape, dim):
    return lax.broadcasted_iota(jnp.int32, shape, dim)


def _ada_kernel(c_ref, w_ref, b_ref, o_ref):
    o_ref[0] = _mm3(_silu(c_ref[...]), w_ref[0]) + b_ref[0]


def _ada(c_rows, w_ada, b_ada):
    depth, d, d3 = w_ada.shape
    r = c_rows.shape[0]
    tn = ADA_TILE
    return pl.pallas_call(
        _ada_kernel,
        grid=(depth, d3 // tn),
        in_specs=[pl.BlockSpec((r, d), lambda l, j: (0, 0)),
                  pl.BlockSpec((1, d, tn), lambda l, j: (l, 0, j)),
                  pl.BlockSpec((1, 1, tn), lambda l, j: (l, 0, j))],
        out_specs=pl.BlockSpec((1, r, tn), lambda l, j: (l, 0, j)),
        out_shape=jax.ShapeDtypeStruct((depth, r, d3), F32),
        name="ada_modulation",
    )(c_rows, w_ada, b_ada.reshape(depth, 1, d3))


def _proj_kernel(x_ref, xp_ref, xn_ref, mod_ref, nw_ref, w_ref, cos_ref, sinn_ref, sinp_ref, lnw_ref, aws_ref,
                 abias_ref, convw_ref, ya_ref, q_ref, k_ref, v_ref, zb_ref, cqkv_ref, zc_ref, g_ref, *, rope):
    d = D_MODEL
    mod = mod_ref[0]

    def normed(x):
        y = x * lax.rsqrt(jnp.mean(x * x, axis=-1, keepdims=True) + NORM_EPS) * nw_ref[...]
        return (y * (1.0 + mod[:, d:2 * d]) + mod[:, :d]).astype(MXU)

    t = x_ref.shape[1]
    half_t = t // 2
    proj = lambda lhs, lo, hi: jnp.dot(lhs, w_ref[:, lo:hi], preferred_element_type=F32)
    vals = {}
    h0 = normed(x_ref[0, :half_t])
    pc0 = proj(jnp.concatenate([h0, normed(xp_ref[0])], axis=0), OFF_C, OFF_G)
    h1 = normed(x_ref[0, half_t:])
    pc1 = proj(jnp.concatenate([h1, normed(xn_ref[0])], axis=0), OFF_C, OFF_G)
    h = jnp.concatenate([h0, h1], axis=0)

    def finish_a():
        pa = vals.pop("a")
        u, v, z = pa[:, :A_WIDTH], pa[:, A_WIDTH:2 * A_WIDTH], pa[:, 2 * A_WIDTH:]
        vc = v - jnp.mean(v, axis=-1, keepdims=True)
        vn = (vc * lax.rsqrt(jnp.mean(vc * vc, axis=-1, keepdims=True) + A_LN_EPS) * lnw_ref[...]).astype(MXU)
        grp = _iota((A_CHUNK, A_WIDTH), 1) // (A_WIDTH // A_GROUPS)
        for c in range(t // A_CHUNK):
            rows = slice(c * A_CHUNK, (c + 1) * A_CHUNK)
            r = jnp.dot(aws_ref[...], vn[rows], preferred_element_type=F32)
            s = abias_ref[...]
            for g in range(A_GROUPS):
                s = s + jnp.where(grp == g, r[g * A_CHUNK:(g + 1) * A_CHUNK], 0.0)
            ya_ref[0, rows, :] = (u[rows] * s * _silu(z[rows])).astype(ya_ref.dtype)

    def rot(a):
        if not rope:
            return a
        half = B_HEAD_DIM // 4
        return (a * cos_ref[...] + pltpu.roll(a, B_WIDTH - half, 1) * sinn_ref[...]
                + pltpu.roll(a, half, 1) * sinp_ref[...])

    def finish_q():
        q_ref[0] = (rot(vals.pop("q")) * (B_HEAD_DIM ** -0.5 * LOG2E)).astype(q_ref.dtype)

    def finish_k():
        k_ref[0] = rot(vals.pop("k")).astype(k_ref.dtype)

    def finish_v():
        v_ref[0] = vals.pop("v").T.astype(v_ref.dtype)

    def finish_z():
        zb_ref[0] = vals.pop("z").astype(zb_ref.dtype)

    sub = min(t, CONV_ROWS)

    def start_c():
        i, last = pl.program_id(0), pl.num_programs(0) - 1
        zc_ref[0, :half_t, :] = pc0[:half_t, 3 * C_WIDTH:]
        zc_ref[0, half_t:, :] = pc1[:half_t, 3 * C_WIDTH:]
        prev = jnp.where(i == 0, 0.0, pc0[half_t:, :3 * C_WIDTH])
        nxt = jnp.where(i == last, 0.0, pc1[half_t:, :3 * C_WIDTH])
        vals["c"] = jnp.concatenate([prev, pc0[:half_t, :3 * C_WIDTH], pc1[:half_t, :3 * C_WIDTH], nxt], axis=0)

    def finish_c(r0):
        w = convw_ref[...]
        pad = C_CONV // 2
        win = vals["c"][r0:r0 + sub + 2 * HALO]
        acc = win[HALO:HALO + sub] * w[pad:pad + 1]
        for j in range(C_CONV):
            if j != pad:
                acc = acc + pltpu.roll(win, (pad - j) % (sub + 2 * HALO), 0)[HALO:HALO + sub] * w[j:j + 1]
        act = _silu(acc)
        qh, kh = act[:, :C_WIDTH], act[:, C_WIDTH:2 * C_WIDTH]
        l2 = lambda a: a * lax.rsqrt(_mm_sel(a * a, _same_head()) + 1e-6)
        cqkv_ref[0, r0:r0 + sub, 0:C_WIDTH] = l2(qh) * (C_HEAD_DIM ** -0.5)
        cqkv_ref[0, r0:r0 + sub, C_WIDTH:2 * C_WIDTH] = l2(kh)
        cqkv_ref[0, r0:r0 + sub, 2 * C_WIDTH:] = act[:, 2 * C_WIDTH:]

    def start(name, lo, hi):
        def go():
            vals[name] = proj(h, lo, hi)
        return go

    def finish_g():
        g_ref[0] = vals.pop("g")

    bq = OFF_B
    order = [start_c, start("a", OFF_A, OFF_B)]
    convs = [functools.partial(finish_c, r0) for r0 in range(0, t, sub)]
    later = [start("q", bq, bq + B_WIDTH), start("k", bq + B_WIDTH, bq + 2 * B_WIDTH),
             start("v", bq + 2 * B_WIDTH, bq + 3 * B_WIDTH), start("z", bq + 3 * B_WIDTH, OFF_C),
             start("g", OFF_G, D_IN_PAD)]
    fins = convs + [finish_a, finish_q, finish_k, finish_v, finish_z, finish_g]
    while later or fins:
        if later:
            order.append(later.pop(0))
        if fins:
            order.append(fins.pop(0))
    for step in order:
        step()


def _proj(x, mod, nw, w_in, tabs, lnw, aws, abias, convw, *, rope, tile):
    bsz, n, d = x.shape
    cos, sinn, sinp = tabs
    mod_map = (lambda i, b: (b, 0, 0)) if mod.shape[0] == bsz else (lambda i, b: (0, 0, 0))
    tok = lambda w: pl.BlockSpec((1, tile, w), lambda i, b: (b, i, 0))
    per = tile // HALO
    before = pl.BlockSpec((1, HALO, d), lambda i, b: (b, jnp.maximum(i * per - 1, 0), 0))
    after = pl.BlockSpec((1, HALO, d), lambda i, b: (b, jnp.minimum((i + 1) * per, n // HALO - 1), 0))
    const = lambda a: pl.BlockSpec(a.shape, lambda i, b: (0,) * a.ndim)
    tab = pl.BlockSpec((tile, B_WIDTH), lambda i, b: (i, 0))
    outs = [(A_WIDTH, MXU), (B_WIDTH, MXU), (B_WIDTH, MXU), (B_WIDTH, MXU), (B_WIDTH, F32),
            (3 * C_WIDTH, F32), (C_WIDTH, F32), (GATE_PAD, F32)]
    out_specs = [tok(w) for w, _ in outs]
    out_shape = [jax.ShapeDtypeStruct((bsz, n, w), dt) for w, dt in outs]
    out_specs[3] = pl.BlockSpec((1, B_WIDTH, tile), lambda i, b: (b, 0, i))
    out_shape[3] = jax.ShapeDtypeStruct((bsz, B_WIDTH, n), MXU)
    return pl.pallas_call(
        functools.partial(_proj_kernel, rope=rope),
        grid=(n // tile, bsz),
        in_specs=[tok(d), before, after, pl.BlockSpec((1, 1, 3 * d), mod_map), const(nw), const(w_in), tab, tab, tab,
                  const(lnw), const(aws), const(abias), const(convw)],
        out_specs=out_specs,
        out_shape=out_shape,
        compiler_params=pltpu.CompilerParams(dimension_semantics=("arbitrary", "arbitrary"),
                                             vmem_limit_bytes=VMEM_LIMIT),
        name="norm_inproj_rope" if rope else "norm_inproj_ctx",
    )(x, x, x, mod, nw, w_in, cos, sinn, sinp, lnw, aws, abias, convw)


def _interleave(gens):
    gens = list(gens)
    while gens:
        for g in list(gens):
            if next(g, StopIteration) is StopIteration:
                gens.remove(g)


def _attn_kernel(*refs, seg_lens, lam_init):
    nseg = len(seg_lens)
    q_ref, z_ref, lam_ref, sw_ref = refs[:4]
    k_refs, v_refs = refs[4:4 + nseg], refs[4 + nseg:4 + 2 * nseg]
    o_ref, s_buf, e_buf = refs[4 + 2 * nseg:]
    lf = lam_ref[...]
    lam = (jnp.exp(jnp.sum(lf[0:1] * lf[1:2], axis=-1, keepdims=True))
           - jnp.exp(jnp.sum(lf[2:3] * lf[3:4], axis=-1, keepdims=True)) + lam_init)
    tq = ATT_QTILE
    ntile = q_ref.shape[1] // tq
    pair = 2 * B_HEAD_DIM
    first = _iota((1, pair), 1) < B_HEAD_DIM
    zero = jnp.zeros((), q_ref.dtype)
    chunks, row = [], 0
    for seg, n in enumerate(seg_lens):
        for c0 in range(0, n, ATT_KCHUNK):
            chunks.append((seg, c0, row))
            row += ATT_KCHUNK
    kc = ATT_KCHUNK
    maxes, heads = {}, {}

    dens = {}

    def scores(i):
        hh, j = divmod(i, 2)
        qt, hp = divmod(hh, B_HEADS)
        lanes = slice(hp * pair, (hp + 1) * pair)
        qm = jnp.where(first if j == 0 else jnp.logical_not(first), q_ref[0, qt * tq:(qt + 1) * tq, lanes], zero)
        m, row = None, 0
        for seg, n in enumerate(seg_lens):
            for c0 in range(0, n, ATT_SCHUNK):
                size = min(ATT_SCHUNK, n - c0)
                s = _mm_nt(k_refs[seg][0, c0:c0 + size, lanes], qm)
                for r in range(0, size, kc):
                    s_buf[hh % 2, j, row:row + kc, :] = s[r:r + kc]
                    cm = jnp.max(s[r:r + kc].reshape(kc // SUBLANES, SUBLANES, tq), axis=0)
                    m = cm if m is None else jnp.maximum(m, cm)
                    row += kc
                    yield
        maxes[i] = jnp.max(m, axis=0, keepdims=True)

    def exps(i):
        hh, j = divmod(i, 2)
        den = None
        for _, _, r0 in chunks:
            e = jnp.exp2(s_buf[hh % 2, j, r0:r0 + kc, :] - maxes[i])
            e_buf[hh % 2, j, r0:r0 + kc, :] = e.astype(MXU)
            cs = jnp.sum(e.reshape(kc // SUBLANES, SUBLANES, tq), axis=0)
            den = cs if den is None else den + cs
            yield
        dens[i] = jnp.sum(den, axis=0, keepdims=True)

    def values(hh):
        qt, hp = divmod(hh, B_HEADS)
        lanes = slice(hp * pair, (hp + 1) * pair)
        d0, d1 = dens[2 * hh], dens[2 * hh + 1]
        coef = (lam * d0 * (1.0 / d1)).astype(MXU)
        pv = None
        for seg, c0, r0 in chunks:
            p = e_buf[hh % 2, 0, r0:r0 + kc, :] - coef * e_buf[hh % 2, 1, r0:r0 + kc, :]
            t = _mm(v_refs[seg][0, lanes, c0:c0 + kc], p)
            pv = t if pv is None else pv + t
            yield
        acc = (pv * (1.0 / d0)).T
        o = acc * lax.rsqrt(jnp.mean(acc * acc, axis=-1, keepdims=True) + NORM_EPS) * sw_ref[...]
        heads[hh] = o * (1.0 - lam_init)
        if hp == B_HEADS - 1:
            rows = slice(qt * tq, (qt + 1) * tq)
            o_all = jnp.concatenate([heads.pop(qt * B_HEADS + p) for p in range(B_HEADS)], axis=-1)
            o_ref[0, rows, :] = (o_all * _silu(z_ref[0, rows, :])).astype(o_ref.dtype)

    nh = 2 * B_HEADS * ntile
    for t in range(nh + 2):
        stage = []
        if t < nh:
            stage.append(scores(t))
        if 1 <= t <= nh:
            stage.append(exps(t - 1))
        if t >= 3 and t % 2 == 1:
            stage.append(values((t - 3) // 2))
        _interleave(stage)


def _attn(q, z, lam_p, subln_w, ks, vs, *, lam_init, tile):
    bsz, n, w = q.shape
    seg_lens = tuple(a.shape[1] for a in ks)
    tok = pl.BlockSpec((1, tile, w), lambda b, i: (b, i, 0))
    const = lambda a: pl.BlockSpec(a.shape, lambda b, i: (0,) * a.ndim)
    full = lambda a: pl.BlockSpec((1,) + a.shape[1:], lambda b, i: (b, 0, 0))
    return pl.pallas_call(
        functools.partial(_attn_kernel, seg_lens=seg_lens, lam_init=lam_init),
        grid=(bsz, n // tile),
        in_specs=[tok, tok, const(lam_p), const(subln_w)] + [full(a) for a in ks] + [full(a) for a in vs],
        out_specs=tok,
        out_shape=jax.ShapeDtypeStruct((bsz, n, w), MXU),
        scratch_shapes=[pltpu.VMEM((2, 2, sum(seg_lens), ATT_QTILE), F32),
                        pltpu.VMEM((2, 2, sum(seg_lens), ATT_QTILE), MXU)],
        compiler_params=pltpu.CompilerParams(dimension_semantics=("arbitrary", "arbitrary"),
                                             vmem_limit_bytes=VMEM_LIMIT),
        name="diff_attention_%dseg" % len(ks),
    )(q, z, lam_p, subln_w, *ks, *vs)


def _head_of_lane(shape, dim):
    return _iota(shape, dim) // C_HEAD_DIM


def _expand_heads(src, first):
    r, w = src.shape
    lane = _iota((r, w), 1)
    halves = [jnp.take_along_axis(src, first + (lane + off) // C_HEAD_DIM, axis=1) for off in range(0, C_WIDTH, w)]
    return jnp.concatenate(halves, axis=1)


def _split3(a):
    a1 = a.astype(MXU)
    r1 = a - a1.astype(F32)
    a2 = r1.astype(MXU)
    return a1, a2, (r1 - a2.astype(F32)).astype(MXU)


def _mm_sel(a, sel):
    hi, lo = _split(a)
    return jnp.dot(hi, sel, preferred_element_type=F32) + jnp.dot(lo, sel, preferred_element_type=F32)


def _same_head():
    return (_head_of_lane((C_WIDTH, C_WIDTH), 0) == _head_of_lane((C_WIDTH, C_WIDTH), 1)).astype(MXU)


def _gdn_prep(qkv_ref, gate_ref, n, off, alog_ref, dtb_ref, q_s, k_s, v_s, bf_s, bb_s, gf_s, gb_s):
    rows = GDN_BLOCK
    nblk = n // rows
    ri, rj = _iota((rows, rows), 0), _iota((rows, rows), 1)
    same_chunk = (ri // C_CHUNK) == (rj // C_CHUNK)
    csum = jnp.concatenate([(same_chunk & (rj <= ri)).astype(MXU), (same_chunk & (rj >= ri)).astype(MXU)], axis=0)

    for i in range(nblk):
        src = slice(i * rows, (i + 1) * rows)
        dst = slice(off + i * rows, off + (i + 1) * rows)
        q_s[dst, :] = qkv_ref[0, src, 0:C_WIDTH]
        k_s[dst, :] = qkv_ref[0, src, C_WIDTH:2 * C_WIDTH]
        v_s[dst, :] = qkv_ref[0, src, 2 * C_WIDTH:]
        gt = gate_ref[0, src, :]
        h = C_HEADS
        beta = _sigmoid(gt)
        bf_s[dst, :] = _expand_heads(beta, 0)
        bb_s[dst, :] = _expand_heads(beta, h)
        sp = jnp.maximum(gt + dtb_ref[...], 0.0) + jnp.log1p(jnp.exp(-jnp.abs(gt + dtb_ref[...])))
        g = -jnp.exp(alog_ref[...]) * sp
        w = GATE_PAD
        r = jnp.dot(csum, jnp.concatenate(_split3(g), axis=1), preferred_element_type=F32)
        gc = r[:, :w] + (r[:, w:2 * w] + r[:, 2 * w:])
        gf_s[dst, :] = _expand_heads(gc[:rows], 2 * h)
        gb_s[dst, :] = _expand_heads(gc[rows:], 3 * h)
        yield


def _block_diag(a, bmask):
    return jnp.where(bmask, jnp.concatenate([a] * C_HEADS, axis=0), 0.0)


def _gdn_local(chains, q_s, k_s, v_s, bufs, buf):
    c, wd = C_CHUNK, C_WIDTH
    u_buf, w_buf, a_buf, qd_buf, kd_buf, eg_buf = bufs
    ii = _iota((c, wd), 0)
    jj = _iota((c, wd), 1) % c
    eye = jj == ii
    bmask = _head_of_lane((C_HEADS * c, wd), 1) == (_iota((C_HEADS * c, wd), 0) // c)
    bd = lambda a: _block_diag(a, bmask)
    ident = jnp.where(eye, 1.0, 0.0)

    st = []
    for rows, backward, beta_s, gc_s, _, _ in chains:
        q, k, v, beta, gc = q_s[rows, :], k_s[rows, :], v_s[rows, :], beta_s[rows, :], gc_s[rows, :]
        incl = (jj >= ii) if backward else (jj <= ii)
        strict = (jj > ii) if backward else (jj < ii)
        grow = jnp.sum(jnp.where(eye, gc, 0.0), axis=0, keepdims=True)
        decay = jnp.exp(jnp.where(incl, gc - grow, NEG_BIG))
        glast = gc[0:1] if backward else gc[c - 1:c]
        st.append(dict(q=q, k=k, v=v, beta=beta, gc=gc, decay=decay, strict=strict, glast=glast, kb=k * beta,
                       egc=jnp.exp(gc)))
    yield
    for s in st:
        qk = _mm_nt(jnp.concatenate([s["kb"], s["q"]], axis=0), bd(s["k"]))
        s["lmat"] = jnp.where(s["strict"], qk[:c] * s["decay"], 0.0)
        s["attn"] = qk[c:] * s["decay"]
    yield
    in_blk = (ii // INV_BLOCK) == (jj // INV_BLOCK)
    for s in st:
        dmat = jnp.where(in_blk, s["lmat"], 0.0)
        s["t"] = ident - dmat
        s["pw"] = _mm(dmat, bd(dmat))
    yield
    nsq = int(math.log2(INV_BLOCK)) - 1
    for step in range(nsq):
        for s in st:
            b = bd(s["pw"])
            if step < nsq - 1:
                r = _mm(jnp.concatenate([s["t"], s["pw"]], axis=0), b)
                s["t"], s["pw"] = s["t"] + r[:c], r[c:]
            else:
                s["t"] = s["t"] + _mm(s["t"], b)
        yield
    size = INV_BLOCK
    while size < c:
        emask = ((ii // (2 * size)) == (jj // (2 * size))) & ((ii // size) != (jj // size))
        for s in st:
            s["y"] = _mm(jnp.where(emask, s["lmat"], 0.0), bd(s["t"]))
        yield
        for s in st:
            s["t"] = s["t"] - _mm(s["t"], bd(s["y"]))
        yield
        size *= 2
    for s, (_, _, _, _, d, slot) in zip(st, chains):
        dst = slice(slot * c, (slot + 1) * c)
        u_buf[buf, d, dst, :] = _mm(s["t"], bd(s["v"] * s["beta"]))
        a_buf[buf, d, dst, :] = s["attn"].astype(MXU)
        qd_buf[buf, d, dst, :] = (s["q"] * s["egc"]).astype(MXU)
    yield
    for s, (_, _, _, _, d, slot) in zip(st, chains):
        dst = slice(slot * c, (slot + 1) * c)
        w_buf[buf, d, dst, :] = _mm(s["t"], bd(s["kb"] * s["egc"])).astype(MXU)
        kd_buf[buf, d, dst, :] = (s["k"] * jnp.exp(s["glast"] - s["gc"])).astype(MXU)
        eg_buf[buf, d, slot * HALO:(slot + 1) * HALO, :] = jnp.broadcast_to(jnp.exp(s["glast"]), (HALO, wd))


def _gdn_scan(steps, rows, st_s, o_refs, bufs, buf):
    c, wd = C_CHUNK, C_WIDTH
    u_buf, w_buf, a_buf, qd_buf, kd_buf, eg_buf = bufs
    bmask = _head_of_lane((C_HEADS * c, wd), 1) == (_iota((C_HEADS * c, wd), 0) // c)
    smask = _head_of_lane((wd, wd), 0) == _head_of_lane((wd, wd), 1)
    states = [st_s[0], st_s[1]]
    for slots in steps:
        src = [slice(slot * c, (slot + 1) * c) for slot in slots]
        ws = [_mm(jnp.concatenate([w_buf[buf, d, r, :], qd_buf[buf, d, r, :]], axis=0), states[d])
              for d, r in enumerate(src)]
        yield
        v_new = [u_buf[buf, d, r, :] - ws[d][:c] for d, r in enumerate(src)]
        upd = [_mm_tn(kd_buf[buf, d, r, :], v_new[d]) for d, r in enumerate(src)]
        for d, (r, slot) in enumerate(zip(src, slots)):
            o_refs[d][rows[d][slot], :] = ws[d][c:] + _mm(a_buf[buf, d, r, :], _block_diag(v_new[d], bmask))
        yield
        states = [states[d] * eg_buf[buf, d, slot * HALO:slot * HALO + 1, :] + jnp.where(smask, upd[d], 0.0)
                  for d, slot in enumerate(slots)]
    st_s[0], st_s[1] = states


def _gdn_kernel(*refs, n_ctx, n_lat, ctx_out):
    (rawc_ref, gtc_ref, zc_ref, rawl_ref, gtl_ref, zl_ref, alog_ref, dtb_ref, nw_ref) = refs[:9]
    nout = 2 if ctx_out else 1
    out_refs = refs[9:9 + nout]
    (q_s, k_s, v_s, bf_s, bb_s, gf_s, gb_s, of_s, ob_s, st_s) = refs[9 + nout:19 + nout]
    scr = (q_s, k_s, v_s, bf_s, bb_s, gf_s, gb_s)
    bufs = refs[19 + nout:]
    _interleave([_gdn_prep(rawc_ref, gtc_ref, n_ctx, 0, alog_ref, dtb_ref, *scr)])
    st_s[...] = jnp.zeros_like(st_s)
    ncb, nlb = n_ctx // GDN_BLOCK, n_lat // GDN_BLOCK
    per = GDN_BLOCK // C_CHUNK

    nblk = ncb + nlb
    steps = [(t, per - 1 - t) for t in range(per)]

    def block_rows(i):
        if isinstance(i, int):
            bf = i * GDN_BLOCK
            bb = (ncb - 1 - i if i < ncb else 2 * ncb + nlb - 1 - i) * GDN_BLOCK
        else:
            bf = pl.multiple_of(i * GDN_BLOCK, GDN_BLOCK)
            bb = pl.multiple_of(jnp.where(i < ncb, ncb - 1 - i, 2 * ncb + nlb - 1 - i) * GDN_BLOCK, GDN_BLOCK)
        return ([pl.ds(bf + r * C_CHUNK, C_CHUNK) for r in range(per)],
                [pl.ds(bb + r * C_CHUNK, C_CHUNK) for r in range(per)])

    def local(i):
        rows_f, rows_b = block_rows(i)
        chains = [c for r in range(per) for c in ((rows_f[r], False, bf_s, gf_s, 0, r), (rows_b[r], True, bb_s, gb_s, 1, r))]
        for lo in range(0, len(chains), GDN_LOCKSTEP):
            yield from _gdn_local(chains[lo:lo + GDN_LOCKSTEP], q_s, k_s, v_s, bufs, i % 2)

    def scan(i):
        return _gdn_scan(steps, block_rows(i), st_s, (of_s, ob_s), bufs, i % 2)

    _interleave([local(0), _gdn_prep(rawl_ref, gtl_ref, n_lat, n_ctx, alog_ref, dtb_ref, *scr)])

    def block(i, carry):
        _interleave([local(i + 1), scan(i)])
        return carry

    lax.fori_loop(0, nblk - 1, block, 0)

    same_head = _same_head()

    def finish(blocks):
        for b in blocks:
            src = slice(b * GDN_BLOCK, (b + 1) * GDN_BLOCK)
            if b >= ncb:
                z_ref, out_ref, dst = zl_ref, out_refs[0], slice((b - ncb) * GDN_BLOCK, (b - ncb + 1) * GDN_BLOCK)
            elif ctx_out:
                z_ref, out_ref, dst = zc_ref, out_refs[1], src
            else:
                continue
            o = of_s[src, :] + ob_s[src, :]
            ms = _mm_sel(o * o, same_head) * (1.0 / C_HEAD_DIM)
            y = o * lax.rsqrt(ms + NORM_EPS) * nw_ref[...]
            out_ref[0, dst, :] = (y * _silu(z_ref[0, dst, :])).astype(out_ref.dtype)
            yield

    late = sorted({nblk - 1, ncb})
    _interleave([scan(nblk - 1), finish([b for b in range(nblk) if b not in late])])
    _interleave([finish(late)])


def _gdn(rawc, gtc, zc, rawl, gtl, zl, alog, dtb, nw, *, ctx_out):
    bsz, n_ctx, _ = rawc.shape
    n_lat = rawl.shape[1]
    nt = n_ctx + n_lat
    seq = lambda a: pl.BlockSpec((1,) + a.shape[1:], lambda b: (b, 0, 0))
    const = lambda a: pl.BlockSpec(a.shape, lambda b: (0,) * a.ndim)
    out_shape = [jax.ShapeDtypeStruct((bsz, n_lat, C_WIDTH), MXU)]
    if ctx_out:
        out_shape.append(jax.ShapeDtypeStruct((bsz, n_ctx, C_WIDTH), MXU))
    big = lambda: pltpu.VMEM((nt, C_WIDTH), F32)
    loc = lambda dt: pltpu.VMEM((2, 2, GDN_BLOCK, C_WIDTH), dt)
    per = GDN_BLOCK // C_CHUNK
    res = pl.pallas_call(
        functools.partial(_gdn_kernel, n_ctx=n_ctx, n_lat=n_lat, ctx_out=ctx_out),
        grid=(bsz,),
        in_specs=[seq(rawc), seq(gtc), seq(zc), seq(rawl), seq(gtl), seq(zl),
                  const(alog), const(dtb), const(nw)],
        out_specs=[seq(s) for s in out_shape],
        out_shape=out_shape,
        scratch_shapes=[big() for _ in range(9)] + [pltpu.VMEM((2, C_WIDTH, C_WIDTH), F32)]
        + [loc(F32), loc(MXU), loc(MXU), loc(MXU), loc(MXU), pltpu.VMEM((2, 2, per * HALO, C_WIDTH), F32)],
        compiler_params=pltpu.CompilerParams(dimension_semantics=("arbitrary",), vmem_limit_bytes=VMEM_LIMIT),
        name="gated_deltanet_bidir",
    )(rawc, gtc, zc, rawl, gtl, zl, alog, dtb, nw)
    return res


def _out_kernel(x_ref, mod_ref, ya_ref, yb_ref, yc_ref, w_ref, fw_ref, o_ref, *, final):
    d = D_MODEL
    y = (jnp.dot(ya_ref[0], w_ref[0:A_WIDTH], preferred_element_type=F32)
         + jnp.dot(yb_ref[0], w_ref[A_WIDTH:A_WIDTH + B_WIDTH], preferred_element_type=F32)
         + jnp.dot(yc_ref[0], w_ref[A_WIDTH + B_WIDTH:], preferred_element_type=F32))
    x = x_ref[0] + mod_ref[0][:, 2 * d:] * y
    if final:
        x = x * lax.rsqrt(jnp.mean(x * x, axis=-1, keepdims=True) + NORM_EPS) * fw_ref[...]
    o_ref[0] = x


def _out(x, mod, ya, yb, yc, w_out, fw, *, final, tile):
    bsz, n, d = x.shape
    mod_map = (lambda b, i: (b, 0, 0)) if mod.shape[0] == bsz else (lambda b, i: (0, 0, 0))
    tok = lambda w: pl.BlockSpec((1, tile, w), lambda b, i: (b, i, 0))
    const = lambda a: pl.BlockSpec(a.shape, lambda b, i: (0,) * a.ndim)
    return pl.pallas_call(
        functools.partial(_out_kernel, final=final),
        grid=(bsz, n // tile),
        in_specs=[tok(d), pl.BlockSpec((1, 1, 3 * d), mod_map), tok(A_WIDTH), tok(B_WIDTH), tok(C_WIDTH),
                  const(w_out), const(fw)],
        out_specs=tok(d),
        out_shape=jax.ShapeDtypeStruct((bsz, n, d), F32),
        compiler_params=pltpu.CompilerParams(dimension_semantics=("arbitrary", "arbitrary"),
                                             vmem_limit_bytes=VMEM_LIMIT),
        name="outproj_residual_final" if final else "outproj_residual",
    )(x, mod, ya, yb, yc, w_out, fw)


def _rope_tables(n):
    pos = jnp.arange(n)
    half = B_HEAD_DIM // 2
    inv_freq = ROPE_THETA ** (-jnp.arange(0, half, 2, dtype=F32) / half)
    lane = np.arange(B_WIDTH)
    dd = lane % B_HEAD_DIM
    use_col = (dd // half) == 1
    second = ((dd % half) // (half // 2)) == 1
    fidx = dd % (half // 2)
    p = jnp.where(use_col[None, :], (pos % GRID_W)[:, None], (pos // GRID_W)[:, None]).astype(F32)
    ang = p * inv_freq[fidx][None, :]
    cos, sin = jnp.cos(ang), jnp.sin(ang)
    return cos, jnp.where(second[None, :], 0.0, -sin), jnp.where(second[None, :], sin, 0.0)


def kernel(x, c, ctx, c_ctx, w_ada, b_ada, norm_w, w_in, w_out, a_ln_w, a_ws, a_bs, b_lam, b_subln_w,
           c_conv_w, c_a_log, c_dt_bias, c_norm_w, final_norm_w):
    bsz, n, d = x.shape
    n_ctx = ctx.shape[1]
    depth = w_ada.shape[0]
    assert d == D_MODEL and n % GDN_BLOCK == 0 and n_ctx % GDN_BLOCK == 0 and n % GRID_W == 0
    tile = ATT_QTILE * ATT_TILES
    tile_ctx = min(ATT_QTILE, n_ctx)
    ptile = min(PROJ_TILE, n)
    otile = next(t for t in (OUT_TILE, PROJ_TILE, GDN_BLOCK) if n % t == 0)
    assert n % ptile == 0 and n % tile == 0

    rows = bsz + 1
    rows_pad = -(-rows // 8) * 8
    c_rows = jnp.concatenate([c, c_ctx[None, :], jnp.zeros((rows_pad - rows, d), F32)], axis=0)
    mods = _ada(c_rows, w_ada, b_ada)

    tabs = _rope_tables(n)
    tabs_ctx = tuple(t[:tile_ctx] for t in tabs)
    gate_pad = jnp.zeros((depth, d, GATE_PAD - N_GATE), w_in.dtype)
    w_in_p = jnp.concatenate([w_in, gate_pad], axis=-1).astype(MXU)
    w_out_b = w_out.astype(MXU)
    aws = a_ws.reshape(depth, A_GROUPS * A_CHUNK, A_CHUNK).astype(MXU)
    abias = jnp.repeat(jnp.swapaxes(a_bs, 1, 2), A_WIDTH // A_GROUPS, axis=2)
    row = lambda a: a.reshape(1, -1)

    xc = ctx
    for i in range(depth):
        lam_init = 0.8 - 0.6 * math.exp(-0.3 * i)
        ctx_out = i < depth - 1
        mod_l = mods[i, :bsz][:, None, :]
        mod_c = mods[i, bsz:bsz + 1][:, None, :]
        common = (row(norm_w[i]), w_in_p[i])
        a_par = (row(a_ln_w[i]), aws[i], abias[i], c_conv_w[i])
        ya, qb, kb, vb, zb, craw, zc, gt = _proj(x, mod_l, *common, tabs, *a_par, rope=True, tile=ptile)
        ya_c, qb_c, kb_c, vb_c, zb_c, craw_c, zc_c, gt_c = _proj(xc, mod_c, *common, tabs_ctx, *a_par,
                                                                 rope=False, tile=tile_ctx)
        lam_p, sw = b_lam[i], row(b_subln_w[i])
        yb = _attn(qb, zb, lam_p, sw, [kb_c, kb], [vb_c, vb], lam_init=lam_init, tile=tile)
        gate_row = lambda p: jnp.concatenate([jnp.zeros((2 * C_HEADS,), F32), p.reshape(-1),
                                              jnp.zeros((GATE_PAD - N_GATE,), F32)])[None, :]
        gdn_par = (gate_row(c_a_log[i]), gate_row(c_dt_bias[i]), jnp.tile(c_norm_w[i], C_HEADS)[None, :])
        yc_all = _gdn(craw_c, gt_c, zc_c, craw, gt, zc, *gdn_par, ctx_out=ctx_out)
        final = i == depth - 1
        x = _out(x, mod_l, ya, yb, yc_all[0], w_out_b[i], row(final_norm_w), final=final, tile=otile)
        if ctx_out:
            yb_c = _attn(qb_c, zb_c, lam_p, sw, [kb_c], [vb_c], lam_init=lam_init, tile=tile_ctx)
            xc = _out(xc, mod_c, ya_c, yb_c, yc_all[1], w_out_b[i], row(final_norm_w), final=False, tile=tile_ctx)
    return x
```

```python
import functools
import math

import numpy as np
import jax
import jax.numpy as jnp
from jax import lax
from jax.experimental import pallas as pl
from jax.experimental.pallas import tpu as pltpu

F32 = jnp.float32
MXU = jnp.bfloat16

D_MODEL = 1024
NORM_EPS = 1e-6
GRID_W = 64
A_WIDTH, A_GROUPS, A_CHUNK, A_LN_EPS = 256, 4, 128, 1e-5
B_HEADS, B_HEAD_DIM, B_WIDTH = 4, 64, 512
ROPE_THETA = 10000.0
C_HEADS, C_HEAD_DIM, C_WIDTH, C_CONV, C_CHUNK = 4, 64, 256, 5, 64
N_GATE = 4 * C_HEADS
GATE_PAD = 128
OFF_A, OFF_B, OFF_C, OFF_G = 0, 3 * A_WIDTH, 3 * A_WIDTH + 4 * B_WIDTH, 3 * A_WIDTH + 4 * B_WIDTH + 4 * C_WIDTH
D_IN_PAD = OFF_G + GATE_PAD
NEG_BIG = -1e30
LOG2E = math.log2(math.e)
ATT_KCHUNK = 256
ATT_SCHUNK = 256
ATT_QTILE = 256
ATT_TILES = 1
PROJ_TILE = 512
OUT_TILE = 2048
CONV_ROWS = 256
VMEM_LIMIT = 56 * 1024 * 1024
GDN_BLOCK = 256
GDN_LOCKSTEP = 8
SUBLANES = 8
HALO = SUBLANES
ADA_TILE = 512
INV_BLOCK = 16


def _mm(a, b):
    return jnp.dot(a.astype(MXU), b.astype(MXU), preferred_element_type=F32)


def _mm_nt(a, b):
    return lax.dot_general(a.astype(MXU), b.astype(MXU), (((1,), (1,)), ((), ())), preferred_element_type=F32)


def _mm_tn(a, b):
    return lax.dot_general(a.astype(MXU), b.astype(MXU), (((0,), (0,)), ((), ())), preferred_element_type=F32)


def _split(a):
    hi = a.astype(MXU)
    return hi, (a - hi.astype(F32)).astype(MXU)


def _mm3(a, b):
    ah, al = _split(a)
    bh, bl = _split(b)
    d = lambda x, y: jnp.dot(x, y, preferred_element_type=F32)
    return d(ah, bh) + (d(ah, bl) + d(al, bh))


def _sigmoid(x):
    return 1.0 / (1.0 + jnp.exp(-x))


def _silu(x):
    return x * _sigmoid(x)


def _iota(shape, dim):
    return lax.broadcasted_iota(jnp.int32, shape, dim)


def _ada_kernel(c_ref, w_ref, b_ref, o_ref):
    o_ref[0] = _mm3(_silu(c_ref[...]), w_ref[0]) + b_ref[0]


def _ada(c_rows, w_ada, b_ada):
    depth, d, d3 = w_ada.shape
    r = c_rows.shape[0]
    tn = ADA_TILE
    return pl.pallas_call(
        _ada_kernel,
        grid=(depth, d3 // tn),
        in_specs=[pl.BlockSpec((r, d), lambda l, j: (0, 0)),
                  pl.BlockSpec((1, d, tn), lambda l, j: (l, 0, j)),
                  pl.BlockSpec((1, 1, tn), lambda l, j: (l, 0, j))],
        out_specs=pl.BlockSpec((1, r, tn), lambda l, j: (l, 0, j)),
        out_shape=jax.ShapeDtypeStruct((depth, r, d3), F32),
        name="ada_modulation",
    )(c_rows, w_ada, b_ada.reshape(depth, 1, d3))


def _proj_kernel(x_ref, xp_ref, xn_ref, mod_ref, nw_ref, w_ref, cos_ref, sinn_ref, sinp_ref, lnw_ref, aws_ref,
                 abias_ref, convw_ref, ya_ref, q_ref, k_ref, v_ref, zb_ref, cqkv_ref, zc_ref, g_ref, *, rope):
    d = D_MODEL
    mod = mod_ref[0]

    def normed(x):
        y = x * lax.rsqrt(jnp.mean(x * x, axis=-1, keepdims=True) + NORM_EPS) * nw_ref[...]
        return (y * (1.0 + mod[:, d:2 * d]) + mod[:, :d]).astype(MXU)

    t = x_ref.shape[1]
    half_t = t // 2
    proj = lambda lhs, lo, hi: jnp.dot(lhs, w_ref[:, lo:hi], preferred_element_type=F32)
    vals = {}
    h0 = normed(x_ref[0, :half_t])
    pc0 = proj(jnp.concatenate([h0, normed(xp_ref[0])], axis=0), OFF_C, OFF_G)
    h1 = normed(x_ref[0, half_t:])
    pc1 = proj(jnp.concatenate([h1, normed(xn_ref[0])], axis=0), OFF_C, OFF_G)
    h = jnp.concatenate([h0, h1], axis=0)

    def finish_a():
        pa = vals.pop("a")
        u, v, z = pa[:, :A_WIDTH], pa[:, A_WIDTH:2 * A_WIDTH], pa[:, 2 * A_WIDTH:]
        vc = v - jnp.mean(v, axis=-1, keepdims=True)
        vn = (vc * lax.rsqrt(jnp.mean(vc * vc, axis=-1, keepdims=True) + A_LN_EPS) * lnw_ref[...]).astype(MXU)
        grp = _iota((A_CHUNK, A_WIDTH), 1) // (A_WIDTH // A_GROUPS)
        for c in range(t // A_CHUNK):
            rows = slice(c * A_CHUNK, (c + 1) * A_CHUNK)
            r = jnp.dot(aws_ref[...], vn[rows], preferred_element_type=F32)
            s = abias_ref[...]
            for g in range(A_GROUPS):
                s = s + jnp.where(grp == g, r[g * A_CHUNK:(g + 1) * A_CHUNK], 0.0)
            ya_ref[0, rows, :] = (u[rows] * s * _silu(z[rows])).astype(ya_ref.dtype)

    def rot(a):
        if not rope:
            return a
        half = B_HEAD_DIM // 4
        return (a * cos_ref[...] + pltpu.roll(a, B_WIDTH - half, 1) * sinn_ref[...]
                + pltpu.roll(a, half, 1) * sinp_ref[...])

    def finish_q():
        q_ref[0] = (rot(vals.pop("q")) * (B_HEAD_DIM ** -0.5 * LOG2E)).astype(q_ref.dtype)

    def finish_k():
        k_ref[0] = rot(vals.pop("k")).astype(k_ref.dtype)

    def finish_v():
        v_ref[0] = vals.pop("v").T.astype(v_ref.dtype)

    def finish_z():
        zb_ref[0] = vals.pop("z").astype(zb_ref.dtype)

    sub = min(t, CONV_ROWS)

    def start_c():
        i, last = pl.program_id(0), pl.num_programs(0) - 1
        zc_ref[0, :half_t, :] = pc0[:half_t, 3 * C_WIDTH:]
        zc_ref[0, half_t:, :] = pc1[:half_t, 3 * C_WIDTH:]
        prev = jnp.where(i == 0, 0.0, pc0[half_t:, :3 * C_WIDTH])
        nxt = jnp.where(i == last, 0.0, pc1[half_t:, :3 * C_WIDTH])
        vals["c"] = jnp.concatenate([prev, pc0[:half_t, :3 * C_WIDTH], pc1[:half_t, :3 * C_WIDTH], nxt], axis=0)

    def finish_c(r0):
        w = convw_ref[...]
        pad = C_CONV // 2
        win = vals["c"][r0:r0 + sub + 2 * HALO]
        acc = win[HALO:HALO + sub] * w[pad:pad + 1]
        for j in range(C_CONV):
            if j != pad:
                acc = acc + pltpu.roll(win, (pad - j) % (sub + 2 * HALO), 0)[HALO:HALO + sub] * w[j:j + 1]
        act = _silu(acc)
        qh, kh = act[:, :C_WIDTH], act[:, C_WIDTH:2 * C_WIDTH]
        l2 = lambda a: a * lax.rsqrt(_mm_sel(a * a, _same_head()) + 1e-6)
        cqkv_ref[0, r0:r0 + sub, 0:C_WIDTH] = l2(qh) * (C_HEAD_DIM ** -0.5)
        cqkv_ref[0, r0:r0 + sub, C_WIDTH:2 * C_WIDTH] = l2(kh)
        cqkv_ref[0, r0:r0 + sub, 2 * C_WIDTH:] = act[:, 2 * C_WIDTH:]

    def start(name, lo, hi):
        def go():
            vals[name] = proj(h, lo, hi)
        return go

    def finish_g():
        g_ref[0] = vals.pop("g")

    bq = OFF_B
    order = [start_c, start("a", OFF_A, OFF_B)]
    convs = [functools.partial(finish_c, r0) for r0 in range(0, t, sub)]
    later = [start("q", bq, bq + B_WIDTH), start("k", bq + B_WIDTH, bq + 2 * B_WIDTH),
             start("v", bq + 2 * B_WIDTH, bq + 3 * B_WIDTH), start("z", bq + 3 * B_WIDTH, OFF_C),
             start("g", OFF_G, D_IN_PAD)]
    fins = convs + [finish_a, finish_q, finish_k, finish_v, finish_z, finish_g]
    while later or fins:
        if later:
            order.append(later.pop(0))
        if fins:
            order.append(fins.pop(0))
    for step in order:
        step()


def _proj(x, mod, nw, w_in, tabs, lnw, aws, abias, convw, *, rope, tile):
    bsz, n, d = x.shape
    cos, sinn, sinp = tabs
    mod_map = (lambda i, b: (b, 0, 0)) if mod.shape[0] == bsz else (lambda i, b: (0, 0, 0))
    tok = lambda w: pl.BlockSpec((1, tile, w), lambda i, b: (b, i, 0))
    per = tile // HALO
    before = pl.BlockSpec((1, HALO, d), lambda i, b: (b, jnp.maximum(i * per - 1, 0), 0))
    after = pl.BlockSpec((1, HALO, d), lambda i, b: (b, jnp.minimum((i + 1) * per, n // HALO - 1), 0))
    const = lambda a: pl.BlockSpec(a.shape, lambda i, b: (0,) * a.ndim)
    tab = pl.BlockSpec((tile, B_WIDTH), lambda i, b: (i, 0))
    outs = [(A_WIDTH, MXU), (B_WIDTH, MXU), (B_WIDTH, MXU), (B_WIDTH, MXU), (B_WIDTH, F32),
            (3 * C_WIDTH, F32), (C_WIDTH, F32), (GATE_PAD, F32)]
    out_specs = [tok(w) for w, _ in outs]
    out_shape = [jax.ShapeDtypeStruct((bsz, n, w), dt) for w, dt in outs]
    out_specs[3] = pl.BlockSpec((1, B_WIDTH, tile), lambda i, b: (b, 0, i))
    out_shape[3] = jax.ShapeDtypeStruct((bsz, B_WIDTH, n), MXU)
    return pl.pallas_call(
        functools.partial(_proj_kernel, rope=rope),
        grid=(n // tile, bsz),
        in_specs=[tok(d), before, after, pl.BlockSpec((1, 1, 3 * d), mod_map), const(nw), const(w_in), tab, tab, tab,
                  const(lnw), const(aws), const(abias), const(convw)],
        out_specs=out_specs,
        out_shape=out_shape,
        compiler_params=pltpu.CompilerParams(dimension_semantics=("arbitrary", "arbitrary"),
                                             vmem_limit_bytes=VMEM_LIMIT),
        name="norm_inproj_rope" if rope else "norm_inproj_ctx",
    )(x, x, x, mod, nw, w_in, cos, sinn, sinp, lnw, aws, abias, convw)


def _interleave(gens):
    gens = list(gens)
    while gens:
        for g in list(gens):
            if next(g, StopIteration) is StopIteration:
                gens.remove(g)


def _attn_kernel(*refs, seg_lens, lam_init):
    nseg = len(seg_lens)
    q_ref, z_ref, lam_ref, sw_ref = refs[:4]
    k_refs, v_refs = refs[4:4 + nseg], refs[4 + nseg:4 + 2 * nseg]
    o_ref, s_buf, e_buf = refs[4 + 2 * nseg:]
    lf = lam_ref[...]
    lam = (jnp.exp(jnp.sum(lf[0:1] * lf[1:2], axis=-1, keepdims=True))
           - jnp.exp(jnp.sum(lf[2:3] * lf[3:4], axis=-1, keepdims=True)) + lam_init)
    tq = ATT_QTILE
    ntile = q_ref.shape[1] // tq
    pair = 2 * B_HEAD_DIM
    first = _iota((1, pair), 1) < B_HEAD_DIM
    zero = jnp.zeros((), q_ref.dtype)
    chunks, row = [], 0
    for seg, n in enumerate(seg_lens):
        for c0 in range(0, n, ATT_KCHUNK):
            chunks.append((seg, c0, row))
            row += ATT_KCHUNK
    kc = ATT_KCHUNK
    maxes, heads = {}, {}

    dens = {}

    def scores(i):
        hh, j = divmod(i, 2)
        qt, hp = divmod(hh, B_HEADS)
        lanes = slice(hp * pair, (hp + 1) * pair)
        qm = jnp.where(first if j == 0 else jnp.logical_not(first), q_ref[0, qt * tq:(qt + 1) * tq, lanes], zero)
        m, row = None, 0
        for seg, n in enumerate(seg_lens):
            for c0 in range(0, n, ATT_SCHUNK):
                size = min(ATT_SCHUNK, n - c0)
                s = _mm_nt(k_refs[seg][0, c0:c0 + size, lanes], qm)
                for r in range(0, size, kc):
                    s_buf[hh % 2, j, row:row + kc, :] = s[r:r + kc]
                    cm = jnp.max(s[r:r + kc].reshape(kc // SUBLANES, SUBLANES, tq), axis=0)
                    m = cm if m is None else jnp.maximum(m, cm)
                    row += kc
                    yield
        maxes[i] = jnp.max(m, axis=0, keepdims=True)

    def exps(i):
        hh, j = divmod(i, 2)
        den = None
        for _, _, r0 in chunks:
            e = jnp.exp2(s_buf[hh % 2, j, r0:r0 + kc, :] - maxes[i])
            e_buf[hh % 2, j, r0:r0 + kc, :] = e.astype(MXU)
            cs = jnp.sum(e.reshape(kc // SUBLANES, SUBLANES, tq), axis=0)
            den = cs if den is None else den + cs
            yield
        dens[i] = jnp.sum(den, axis=0, keepdims=True)

    def values(hh):
        qt, hp = divmod(hh, B_HEADS)
        lanes = slice(hp * pair, (hp + 1) * pair)
        d0, d1 = dens[2 * hh], dens[2 * hh + 1]
        coef = (lam * d0 * (1.0 / d1)).astype(MXU)
        pv = None
        for seg, c0, r0 in chunks:
            p = e_buf[hh % 2, 0, r0:r0 + kc, :] - coef * e_buf[hh % 2, 1, r0:r0 + kc, :]
            t = _mm(v_refs[seg][0, lanes, c0:c0 + kc], p)
            pv = t if pv is None else pv + t
            yield
        acc = (pv * (1.0 / d0)).T
        o = acc * lax.rsqrt(jnp.mean(acc * acc, axis=-1, keepdims=True) + NORM_EPS) * sw_ref[...]
        heads[hh] = o * (1.0 - lam_init)
        if hp == B_HEADS - 1:
            rows = slice(qt * tq, (qt + 1) * tq)
            o_all = jnp.concatenate([heads.pop(qt * B_HEADS + p) for p in range(B_HEADS)], axis=-1)
            o_ref[0, rows, :] = (o_all * _silu(z_ref[0, rows, :])).astype(o_ref.dtype)

    nh = 2 * B_HEADS * ntile
    for t in range(nh + 2):
        stage = []
        if t < nh:
            stage.append(scores(t))
        if 1 <= t <= nh:
            stage.append(exps(t - 1))
        if t >= 3 and t % 2 == 1:
            stage.append(values((t - 3) // 2))
        _interleave(stage)


def _attn(q, z, lam_p, subln_w, ks, vs, *, lam_init, tile):
    bsz, n, w = q.shape
    seg_lens = tuple(a.shape[1] for a in ks)
    tok = pl.BlockSpec((1, tile, w), lambda b, i: (b, i, 0))
    const = lambda a: pl.BlockSpec(a.shape, lambda b, i: (0,) * a.ndim)
    full = lambda a: pl.BlockSpec((1,) + a.shape[1:], lambda b, i: (b, 0, 0))
    return pl.pallas_call(
        functools.partial(_attn_kernel, seg_lens=seg_lens, lam_init=lam_init),
        grid=(bsz, n // tile),
        in_specs=[tok, tok, const(lam_p), const(subln_w)] + [full(a) for a in ks] + [full(a) for a in vs],
        out_specs=tok,
        out_shape=jax.ShapeDtypeStruct((bsz, n, w), MXU),
        scratch_shapes=[pltpu.VMEM((2, 2, sum(seg_lens), ATT_QTILE), F32),
                        pltpu.VMEM((2, 2, sum(seg_lens), ATT_QTILE), MXU)],
        compiler_params=pltpu.CompilerParams(dimension_semantics=("arbitrary", "arbitrary"),
                                             vmem_limit_bytes=VMEM_LIMIT),
        name="diff_attention_%dseg" % len(ks),
    )(q, z, lam_p, subln_w, *ks, *vs)


def _head_of_lane(shape, dim):
    return _iota(shape, dim) // C_HEAD_DIM


def _expand_heads(src, first):
    r, w = src.shape
    lane = _iota((r, w), 1)
    halves = [jnp.take_along_axis(src, first + (lane + off) // C_HEAD_DIM, axis=1) for off in range(0, C_WIDTH, w)]
    return jnp.concatenate(halves, axis=1)


def _split3(a):
    a1 = a.astype(MXU)
    r1 = a - a1.astype(F32)
    a2 = r1.astype(MXU)
    return a1, a2, (r1 - a2.astype(F32)).astype(MXU)


def _mm_sel(a, sel):
    hi, lo = _split(a)
    return jnp.dot(hi, sel, preferred_element_type=F32) + jnp.dot(lo, sel, preferred_element_type=F32)


def _same_head():
    return (_head_of_lane((C_WIDTH, C_WIDTH), 0) == _head_of_lane((C_WIDTH, C_WIDTH), 1)).astype(MXU)


def _gdn_prep(qkv_ref, gate_ref, blocks, off, alog_ref, dtb_ref, q_s, k_s, v_s, bf_s, bb_s, gf_s, gb_s):
    rows = GDN_BLOCK
    ri, rj = _iota((rows, rows), 0), _iota((rows, rows), 1)
    same_chunk = (ri // C_CHUNK) == (rj // C_CHUNK)
    csum = jnp.concatenate([(same_chunk & (rj <= ri)).astype(MXU), (same_chunk & (rj >= ri)).astype(MXU)], axis=0)

    for i in blocks:
        src = slice(i * rows, (i + 1) * rows)
        dst = slice(off + i * rows, off + (i + 1) * rows)
        q_s[dst, :] = qkv_ref[0, src, 0:C_WIDTH]
        k_s[dst, :] = qkv_ref[0, src, C_WIDTH:2 * C_WIDTH]
        v_s[dst, :] = qkv_ref[0, src, 2 * C_WIDTH:]
        gt = gate_ref[0, src, :]
        h = C_HEADS
        beta = _sigmoid(gt)
        bf_s[dst, :] = _expand_heads(beta, 0)
        bb_s[dst, :] = _expand_heads(beta, h)
        sp = jnp.maximum(gt + dtb_ref[...], 0.0) + jnp.log1p(jnp.exp(-jnp.abs(gt + dtb_ref[...])))
        g = -jnp.exp(alog_ref[...]) * sp
        w = GATE_PAD
        r = jnp.dot(csum, jnp.concatenate(_split3(g), axis=1), preferred_element_type=F32)
        gc = r[:, :w] + (r[:, w:2 * w] + r[:, 2 * w:])
        gf_s[dst, :] = _expand_heads(gc[:rows], 2 * h)
        gb_s[dst, :] = _expand_heads(gc[rows:], 3 * h)
        yield


def _block_diag(a, bmask):
    return jnp.where(bmask, jnp.concatenate([a] * C_HEADS, axis=0), 0.0)


def _gdn_local(chains, q_s, k_s, v_s, bufs, buf):
    c, wd = C_CHUNK, C_WIDTH
    u_buf, w_buf, a_buf, qd_buf, kd_buf, eg_buf = bufs
    ii = _iota((c, wd), 0)
    jj = _iota((c, wd), 1) % c
    eye = jj == ii
    bmask = _head_of_lane((C_HEADS * c, wd), 1) == (_iota((C_HEADS * c, wd), 0) // c)
    bd = lambda a: _block_diag(a, bmask)
    ident = jnp.where(eye, 1.0, 0.0)

    st = []
    for rows, backward, beta_s, gc_s, _, _ in chains:
        q, k, v, beta, gc = q_s[rows, :], k_s[rows, :], v_s[rows, :], beta_s[rows, :], gc_s[rows, :]
        incl = (jj >= ii) if backward else (jj <= ii)
        strict = (jj > ii) if backward else (jj < ii)
        grow = jnp.sum(jnp.where(eye, gc, 0.0), axis=0, keepdims=True)
        decay = jnp.exp(jnp.where(incl, gc - grow, NEG_BIG))
        glast = gc[0:1] if backward else gc[c - 1:c]
        st.append(dict(q=q, k=k, v=v, beta=beta, gc=gc, decay=decay, strict=strict, glast=glast, kb=k * beta,
                       egc=jnp.exp(gc)))
    yield
    for s in st:
        qk = _mm_nt(jnp.concatenate([s["kb"], s["q"]], axis=0), bd(s["k"]))
        s["lmat"] = jnp.where(s["strict"], qk[:c] * s["decay"], 0.0)
        s["attn"] = qk[c:] * s["decay"]
    yield
    in_blk = (ii // INV_BLOCK) == (jj // INV_BLOCK)
    for s in st:
        dmat = jnp.where(in_blk, s["lmat"], 0.0)
        s["t"] = ident - dmat
        s["pw"] = _mm(dmat, bd(dmat))
    yield
    nsq = int(math.log2(INV_BLOCK)) - 1
    for step in range(nsq):
        for s in st:
            b = bd(s["pw"])
            if step < nsq - 1:
                r = _mm(jnp.concatenate([s["t"], s["pw"]], axis=0), b)
                s["t"], s["pw"] = s["t"] + r[:c], r[c:]
            else:
                s["t"] = s["t"] + _mm(s["t"], b)
        yield
    size = INV_BLOCK
    while size < c:
        emask = ((ii // (2 * size)) == (jj // (2 * size))) & ((ii // size) != (jj // size))
        for s in st:
            s["y"] = _mm(jnp.where(emask, s["lmat"], 0.0), bd(s["t"]))
        yield
        for s in st:
            s["t"] = s["t"] - _mm(s["t"], bd(s["y"]))
        yield
        size *= 2
    for s, (_, _, _, _, d, slot) in zip(st, chains):
        dst = slice(slot * c, (slot + 1) * c)
        u_buf[buf, d, dst, :] = _mm(s["t"], bd(s["v"] * s["beta"]))
        a_buf[buf, d, dst, :] = s["attn"].astype(MXU)
        qd_buf[buf, d, dst, :] = (s["q"] * s["egc"]).astype(MXU)
    yield
    for s, (_, _, _, _, d, slot) in zip(st, chains):
        dst = slice(slot * c, (slot + 1) * c)
        w_buf[buf, d, dst, :] = _mm(s["t"], bd(s["kb"] * s["egc"])).astype(MXU)
        kd_buf[buf, d, dst, :] = (s["k"] * jnp.exp(s["glast"] - s["gc"])).astype(MXU)
        eg_buf[buf, d, slot * HALO:(slot + 1) * HALO, :] = jnp.broadcast_to(jnp.exp(s["glast"]), (HALO, wd))


def _gdn_scan(steps, rows, st_s, o_refs, bufs, buf):
    c, wd = C_CHUNK, C_WIDTH
    u_buf, w_buf, a_buf, qd_buf, kd_buf, eg_buf = bufs
    bmask = _head_of_lane((C_HEADS * c, wd), 1) == (_iota((C_HEADS * c, wd), 0) // c)
    smask = _head_of_lane((wd, wd), 0) == _head_of_lane((wd, wd), 1)
    states = [st_s[0], st_s[1]]
    for slots in steps:
        src = [slice(slot * c, (slot + 1) * c) for slot in slots]
        ws = [_mm(jnp.concatenate([w_buf[buf, d, r, :], qd_buf[buf, d, r, :]], axis=0), states[d])
              for d, r in enumerate(src)]
        yield
        v_new = [u_buf[buf, d, r, :] - ws[d][:c] for d, r in enumerate(src)]
        upd = [_mm_tn(kd_buf[buf, d, r, :], v_new[d]) for d, r in enumerate(src)]
        for d, (r, slot) in enumerate(zip(src, slots)):
            o_refs[d][rows[d][slot], :] = ws[d][c:] + _mm(a_buf[buf, d, r, :], _block_diag(v_new[d], bmask))
        yield
        states = [states[d] * eg_buf[buf, d, slot * HALO:slot * HALO + 1, :] + jnp.where(smask, upd[d], 0.0)
                  for d, slot in enumerate(slots)]
    st_s[0], st_s[1] = states


def _gdn_kernel(*refs, n_ctx, n_lat, ctx_out):
    (rawc_ref, gtc_ref, zc_ref, rawl_ref, gtl_ref, zl_ref, alog_ref, dtb_ref, nw_ref) = refs[:9]
    nout = 2 if ctx_out else 1
    out_refs = refs[9:9 + nout]
    (q_s, k_s, v_s, bf_s, bb_s, gf_s, gb_s, of_s, ob_s, st_s) = refs[9 + nout:19 + nout]
    scr = (q_s, k_s, v_s, bf_s, bb_s, gf_s, gb_s)
    bufs = refs[19 + nout:]
    ncb, nlb = n_ctx // GDN_BLOCK, n_lat // GDN_BLOCK
    per = GDN_BLOCK // C_CHUNK
    _interleave([_gdn_prep(rawc_ref, gtc_ref, range(ncb), 0, alog_ref, dtb_ref, *scr)])
    st_s[...] = jnp.zeros_like(st_s)

    nblk = ncb + nlb
    steps = [(t, per - 1 - t) for t in range(per)]

    def block_rows(i):
        if isinstance(i, int):
            bf = i * GDN_BLOCK
            bb = (ncb - 1 - i if i < ncb else 2 * ncb + nlb - 1 - i) * GDN_BLOCK
        else:
            bf = pl.multiple_of(i * GDN_BLOCK, GDN_BLOCK)
            bb = pl.multiple_of(jnp.where(i < ncb, ncb - 1 - i, 2 * ncb + nlb - 1 - i) * GDN_BLOCK, GDN_BLOCK)
        return ([pl.ds(bf + r * C_CHUNK, C_CHUNK) for r in range(per)],
                [pl.ds(bb + r * C_CHUNK, C_CHUNK) for r in range(per)])

    def local(i):
        rows_f, rows_b = block_rows(i)
        chains = [c for r in range(per) for c in ((rows_f[r], False, bf_s, gf_s, 0, r), (rows_b[r], True, bb_s, gb_s, 1, r))]
        for lo in range(0, len(chains), GDN_LOCKSTEP):
            yield from _gdn_local(chains[lo:lo + GDN_LOCKSTEP], q_s, k_s, v_s, bufs, i % 2)

    def scan(i):
        return _gdn_scan(steps, block_rows(i), st_s, (of_s, ob_s), bufs, i % 2)

    todo = list(range(nlb))

    def prep_for(steps_ahead):
        blocks = []
        for s in steps_ahead:
            for b in (s, 2 * ncb + nlb - 1 - s):
                if ncb <= b < nblk and b - ncb in todo:
                    todo.remove(b - ncb)
                    blocks.append(b - ncb)
        return _gdn_prep(rawl_ref, gtl_ref, blocks, n_ctx, alog_ref, dtb_ref, *scr)

    _interleave([local(0), prep_for((1, 2))])
    peeled = 0
    while todo:
        _interleave([local(peeled + 1), scan(peeled), prep_for((peeled + 3,))])
        peeled += 1

    def block(i, carry):
        _interleave([local(i + 1), scan(i)])
        return carry

    lax.fori_loop(peeled, nblk - 1, block, 0)

    same_head = _same_head()

    def finish(blocks):
        for b in blocks:
            src = slice(b * GDN_BLOCK, (b + 1) * GDN_BLOCK)
            if b >= ncb:
                z_ref, out_ref, dst = zl_ref, out_refs[0], slice((b - ncb) * GDN_BLOCK, (b - ncb + 1) * GDN_BLOCK)
            elif ctx_out:
                z_ref, out_ref, dst = zc_ref, out_refs[1], src
            else:
                continue
            o = of_s[src, :] + ob_s[src, :]
            ms = _mm_sel(o * o, same_head) * (1.0 / C_HEAD_DIM)
            y = o * lax.rsqrt(ms + NORM_EPS) * nw_ref[...]
            out_ref[0, dst, :] = (y * _silu(z_ref[0, dst, :])).astype(out_ref.dtype)
            yield

    late = sorted({nblk - 1, ncb})
    _interleave([scan(nblk - 1), finish([b for b in range(nblk) if b not in late])])
    _interleave([finish(late)])


def _gdn(rawc, gtc, zc, rawl, gtl, zl, alog, dtb, nw, *, ctx_out):
    bsz, n_ctx, _ = rawc.shape
    n_lat = rawl.shape[1]
    nt = n_ctx + n_lat
    seq = lambda a: pl.BlockSpec((1,) + a.shape[1:], lambda b: (b, 0, 0))
    const = lambda a: pl.BlockSpec(a.shape, lambda b: (0,) * a.ndim)
    out_shape = [jax.ShapeDtypeStruct((bsz, n_lat, C_WIDTH), MXU)]
    if ctx_out:
        out_shape.append(jax.ShapeDtypeStruct((bsz, n_ctx, C_WIDTH), MXU))
    big = lambda: pltpu.VMEM((nt, C_WIDTH), F32)
    loc = lambda dt: pltpu.VMEM((2, 2, GDN_BLOCK, C_WIDTH), dt)
    per = GDN_BLOCK // C_CHUNK
    res = pl.pallas_call(
        functools.partial(_gdn_kernel, n_ctx=n_ctx, n_lat=n_lat, ctx_out=ctx_out),
        grid=(bsz,),
        in_specs=[seq(rawc), seq(gtc), seq(zc), seq(rawl), seq(gtl), seq(zl),
                  const(alog), const(dtb), const(nw)],
        out_specs=[seq(s) for s in out_shape],
        out_shape=out_shape,
        scratch_shapes=[big() for _ in range(9)] + [pltpu.VMEM((2, C_WIDTH, C_WIDTH), F32)]
        + [loc(F32), loc(MXU), loc(MXU), loc(MXU), loc(MXU), pltpu.VMEM((2, 2, per * HALO, C_WIDTH), F32)],
        compiler_params=pltpu.CompilerParams(dimension_semantics=("arbitrary",), vmem_limit_bytes=VMEM_LIMIT),
        name="gated_deltanet_bidir",
    )(rawc, gtc, zc, rawl, gtl, zl, alog, dtb, nw)
    return res


def _out_kernel(x_ref, mod_ref, ya_ref, yb_ref, yc_ref, w_ref, fw_ref, o_ref, *, final):
    d = D_MODEL
    y = (jnp.dot(ya_ref[0], w_ref[0:A_WIDTH], preferred_element_type=F32)
         + jnp.dot(yb_ref[0], w_ref[A_WIDTH:A_WIDTH + B_WIDTH], preferred_element_type=F32)
         + jnp.dot(yc_ref[0], w_ref[A_WIDTH + B_WIDTH:], preferred_element_type=F32))
    x = x_ref[0] + mod_ref[0][:, 2 * d:] * y
    if final:
        x = x * lax.rsqrt(jnp.mean(x * x, axis=-1, keepdims=True) + NORM_EPS) * fw_ref[...]
    o_ref[0] = x


def _out(x, mod, ya, yb, yc, w_out, fw, *, final, tile):
    bsz, n, d = x.shape
    mod_map = (lambda b, i: (b, 0, 0)) if mod.shape[0] == bsz else (lambda b, i: (0, 0, 0))
    tok = lambda w: pl.BlockSpec((1, tile, w), lambda b, i: (b, i, 0))
    const = lambda a: pl.BlockSpec(a.shape, lambda b, i: (0,) * a.ndim)
    return pl.pallas_call(
        functools.partial(_out_kernel, final=final),
        grid=(bsz, n // tile),
        in_specs=[tok(d), pl.BlockSpec((1, 1, 3 * d), mod_map), tok(A_WIDTH), tok(B_WIDTH), tok(C_WIDTH),
                  const(w_out), const(fw)],
        out_specs=tok(d),
        out_shape=jax.ShapeDtypeStruct((bsz, n, d), F32),
        compiler_params=pltpu.CompilerParams(dimension_semantics=("arbitrary", "arbitrary"),
                                             vmem_limit_bytes=VMEM_LIMIT),
        name="outproj_residual_final" if final else "outproj_residual",
    )(x, mod, ya, yb, yc, w_out, fw)


def _rope_tables(n):
    pos = jnp.arange(n)
    half = B_HEAD_DIM // 2
    inv_freq = ROPE_THETA ** (-jnp.arange(0, half, 2, dtype=F32) / half)
    lane = np.arange(B_WIDTH)
    dd = lane % B_HEAD_DIM
    use_col = (dd // half) == 1
    second = ((dd % half) // (half // 2)) == 1
    fidx = dd % (half // 2)
    p = jnp.where(use_col[None, :], (pos % GRID_W)[:, None], (pos // GRID_W)[:, None]).astype(F32)
    ang = p * inv_freq[fidx][None, :]
    cos, sin = jnp.cos(ang), jnp.sin(ang)
    return cos, jnp.where(second[None, :], 0.0, -sin), jnp.where(second[None, :], sin, 0.0)


def kernel(x, c, ctx, c_ctx, w_ada, b_ada, norm_w, w_in, w_out, a_ln_w, a_ws, a_bs, b_lam, b_subln_w,
           c_conv_w, c_a_log, c_dt_bias, c_norm_w, final_norm_w):
    bsz, n, d = x.shape
    n_ctx = ctx.shape[1]
    depth = w_ada.shape[0]
    assert d == D_MODEL and n % GDN_BLOCK == 0 and n_ctx % GDN_BLOCK == 0 and n % GRID_W == 0
    tile = ATT_QTILE * ATT_TILES
    tile_ctx = min(ATT_QTILE, n_ctx)
    ptile = min(PROJ_TILE, n)
    otile = next(t for t in (OUT_TILE, PROJ_TILE, GDN_BLOCK) if n % t == 0)
    assert n % ptile == 0 and n % tile == 0

    rows = bsz + 1
    rows_pad = -(-rows // 8) * 8
    c_rows = jnp.concatenate([c, c_ctx[None, :], jnp.zeros((rows_pad - rows, d), F32)], axis=0)
    mods = _ada(c_rows, w_ada, b_ada)

    tabs = _rope_tables(n)
    tabs_ctx = tuple(t[:tile_ctx] for t in tabs)
    gate_pad = jnp.zeros((depth, d, GATE_PAD - N_GATE), w_in.dtype)
    w_in_p = jnp.concatenate([w_in, gate_pad], axis=-1).astype(MXU)
    w_out_b = w_out.astype(MXU)
    aws = a_ws.reshape(depth, A_GROUPS * A_CHUNK, A_CHUNK).astype(MXU)
    abias = jnp.repeat(jnp.swapaxes(a_bs, 1, 2), A_WIDTH // A_GROUPS, axis=2)
    row = lambda a: a.reshape(1, -1)

    xc = ctx
    for i in range(depth):
        lam_init = 0.8 - 0.6 * math.exp(-0.3 * i)
        ctx_out = i < depth - 1
        mod_l = mods[i, :bsz][:, None, :]
        mod_c = mods[i, bsz:bsz + 1][:, None, :]
        common = (row(norm_w[i]), w_in_p[i])
        a_par = (row(a_ln_w[i]), aws[i], abias[i], c_conv_w[i])
        ya, qb, kb, vb, zb, craw, zc, gt = _proj(x, mod_l, *common, tabs, *a_par, rope=True, tile=ptile)
        ya_c, qb_c, kb_c, vb_c, zb_c, craw_c, zc_c, gt_c = _proj(xc, mod_c, *common, tabs_ctx, *a_par,
                                                                 rope=False, tile=tile_ctx)
        lam_p, sw = b_lam[i], row(b_subln_w[i])
        yb = _attn(qb, zb, lam_p, sw, [kb_c, kb], [vb_c, vb], lam_init=lam_init, tile=tile)
        gate_row = lambda p: jnp.concatenate([jnp.zeros((2 * C_HEADS,), F32), p.reshape(-1),
                                              jnp.zeros((GATE_PAD - N_GATE,), F32)])[None, :]
        gdn_par = (gate_row(c_a_log[i]), gate_row(c_dt_bias[i]), jnp.tile(c_norm_w[i], C_HEADS)[None, :])
        yc_all = _gdn(craw_c, gt_c, zc_c, craw, gt, zc, *gdn_par, ctx_out=ctx_out)
        final = i == depth - 1
        x = _out(x, mod_l, ya, yb, yc_all[0], w_out_b[i], row(final_norm_w), final=final, tile=otile)
        if ctx_out:
            yb_c = _attn(qb_c, zb_c, lam_p, sw, [kb_c], [vb_c], lam_init=lam_init, tile=tile_ctx)
            xc = _out(xc, mod_c, ya_c, yb_c, yc_all[1], w_out_b[i], row(final_norm_w), final=False, tile=tile_ctx)
    return x
```

```python
import functools
import math

import numpy as np
import jax
import jax.numpy as jnp
from jax import lax
from jax.experimental import pallas as pl
from jax.experimental.pallas import tpu as pltpu

F32 = jnp.float32
MXU = jnp.bfloat16

D_MODEL = 1024
NORM_EPS = 1e-6
GRID_W = 64
A_WIDTH, A_GROUPS, A_CHUNK, A_LN_EPS = 256, 4, 128, 1e-5
B_HEADS, B_HEAD_DIM, B_WIDTH = 4, 64, 512
ROPE_THETA = 10000.0
C_HEADS, C_HEAD_DIM, C_WIDTH, C_CONV, C_CHUNK = 4, 64, 256, 5, 64
N_GATE = 4 * C_HEADS
GATE_PAD = 128
OFF_A, OFF_B, OFF_C, OFF_G = 0, 3 * A_WIDTH, 3 * A_WIDTH + 4 * B_WIDTH, 3 * A_WIDTH + 4 * B_WIDTH + 4 * C_WIDTH
D_IN_PAD = OFF_G + GATE_PAD
NEG_BIG = -1e30
LOG2E = math.log2(math.e)
ATT_KCHUNK = 256
ATT_SCHUNK = 256
ATT_QTILE = 256
ATT_TILES = 1
PROJ_TILE = 512
OUT_TILE = 2048
CONV_ROWS = 256
VMEM_LIMIT = 56 * 1024 * 1024
GDN_BLOCK = 256
GDN_LOCKSTEP = 8
SUBLANES = 8
HALO = SUBLANES
ADA_TILE = 512
INV_BLOCK = 8


def _mm(a, b):
    return jnp.dot(a.astype(MXU), b.astype(MXU), preferred_element_type=F32)


def _mm_nt(a, b):
    return lax.dot_general(a.astype(MXU), b.astype(MXU), (((1,), (1,)), ((), ())), preferred_element_type=F32)


def _mm_tn(a, b):
    return lax.dot_general(a.astype(MXU), b.astype(MXU), (((0,), (0,)), ((), ())), preferred_element_type=F32)


def _split(a):
    hi = a.astype(MXU)
    return hi, (a - hi.astype(F32)).astype(MXU)


def _mm3(a, b):
    ah, al = _split(a)
    bh, bl = _split(b)
    d = lambda x, y: jnp.dot(x, y, preferred_element_type=F32)
    return d(ah, bh) + (d(ah, bl) + d(al, bh))


def _sigmoid(x):
    return 1.0 / (1.0 + jnp.exp(-x))


def _silu(x):
    return x * _sigmoid(x)


def _iota(shape, dim):
    return lax.broadcasted_iota(jnp.int32, shape, dim)


def _ada_kernel(c_ref, w_ref, b_ref, o_ref):
    o_ref[0] = _mm3(_silu(c_ref[...]), w_ref[0]) + b_ref[0]


def _ada(c_rows, w_ada, b_ada):
    depth, d, d3 = w_ada.shape
    r = c_rows.shape[0]
    tn = ADA_TILE
    return pl.pallas_call(
        _ada_kernel,
        grid=(depth, d3 // tn),
        in_specs=[pl.BlockSpec((r, d), lambda l, j: (0, 0)),
                  pl.BlockSpec((1, d, tn), lambda l, j: (l, 0, j)),
                  pl.BlockSpec((1, 1, tn), lambda l, j: (l, 0, j))],
        out_specs=pl.BlockSpec((1, r, tn), lambda l, j: (l, 0, j)),
        out_shape=jax.ShapeDtypeStruct((depth, r, d3), F32),
        name="ada_modulation",
    )(c_rows, w_ada, b_ada.reshape(depth, 1, d3))


def _proj_kernel(x_ref, xp_ref, xn_ref, mod_ref, nw_ref, w_ref, cos_ref, sinn_ref, sinp_ref, lnw_ref, aws_ref,
                 abias_ref, convw_ref, ya_ref, q_ref, k_ref, v_ref, zb_ref, cqkv_ref, zc_ref, g_ref, *, rope):
    d = D_MODEL
    mod = mod_ref[0]

    def normed(x):
        y = x * lax.rsqrt(jnp.mean(x * x, axis=-1, keepdims=True) + NORM_EPS) * nw_ref[...]
        return (y * (1.0 + mod[:, d:2 * d]) + mod[:, :d]).astype(MXU)

    t = x_ref.shape[1]
    half_t = t // 2
    proj = lambda lhs, lo, hi: jnp.dot(lhs, w_ref[:, lo:hi], preferred_element_type=F32)
    vals = {}
    h0 = normed(x_ref[0, :half_t])
    pc0 = proj(jnp.concatenate([h0, normed(xp_ref[0])], axis=0), OFF_C, OFF_G)
    h1 = normed(x_ref[0, half_t:])
    pc1 = proj(jnp.concatenate([h1, normed(xn_ref[0])], axis=0), OFF_C, OFF_G)
    h = jnp.concatenate([h0, h1], axis=0)

    def finish_a():
        pa = vals.pop("a")
        u, v, z = pa[:, :A_WIDTH], pa[:, A_WIDTH:2 * A_WIDTH], pa[:, 2 * A_WIDTH:]
        vc = v - jnp.mean(v, axis=-1, keepdims=True)
        vn = (vc * lax.rsqrt(jnp.mean(vc * vc, axis=-1, keepdims=True) + A_LN_EPS) * lnw_ref[...]).astype(MXU)
        grp = _iota((A_CHUNK, A_WIDTH), 1) // (A_WIDTH // A_GROUPS)
        for c in range(t // A_CHUNK):
            rows = slice(c * A_CHUNK, (c + 1) * A_CHUNK)
            r = jnp.dot(aws_ref[...], vn[rows], preferred_element_type=F32)
            s = abias_ref[...]
            for g in range(A_GROUPS):
                s = s + jnp.where(grp == g, r[g * A_CHUNK:(g + 1) * A_CHUNK], 0.0)
            ya_ref[0, rows, :] = (u[rows] * s * _silu(z[rows])).astype(ya_ref.dtype)

    def rot(a):
        if not rope:
            return a
        half = B_HEAD_DIM // 4
        return (a * cos_ref[...] + pltpu.roll(a, B_WIDTH - half, 1) * sinn_ref[...]
                + pltpu.roll(a, half, 1) * sinp_ref[...])

    def finish_q():
        q_ref[0] = (rot(vals.pop("q")) * (B_HEAD_DIM ** -0.5 * LOG2E)).astype(q_ref.dtype)

    def finish_k():
        k_ref[0] = rot(vals.pop("k")).astype(k_ref.dtype)

    def finish_v():
        v_ref[0] = vals.pop("v").T.astype(v_ref.dtype)

    def finish_z():
        zb_ref[0] = vals.pop("z").astype(zb_ref.dtype)

    sub = min(t, CONV_ROWS)

    def start_c():
        i, last = pl.program_id(0), pl.num_programs(0) - 1
        zc_ref[0, :half_t, :] = pc0[:half_t, 3 * C_WIDTH:]
        zc_ref[0, half_t:, :] = pc1[:half_t, 3 * C_WIDTH:]
        prev = jnp.where(i == 0, 0.0, pc0[half_t:, :3 * C_WIDTH])
        nxt = jnp.where(i == last, 0.0, pc1[half_t:, :3 * C_WIDTH])
        vals["c"] = jnp.concatenate([prev, pc0[:half_t, :3 * C_WIDTH], pc1[:half_t, :3 * C_WIDTH], nxt], axis=0)

    def finish_c(r0):
        w = convw_ref[...]
        pad = C_CONV // 2
        win = vals["c"][r0:r0 + sub + 2 * HALO]
        acc = win[HALO:HALO + sub] * w[pad:pad + 1]
        for j in range(C_CONV):
            if j != pad:
                acc = acc + pltpu.roll(win, (pad - j) % (sub + 2 * HALO), 0)[HALO:HALO + sub] * w[j:j + 1]
        act = _silu(acc)
        qh, kh = act[:, :C_WIDTH], act[:, C_WIDTH:2 * C_WIDTH]
        l2 = lambda a: a * lax.rsqrt(_mm_sel(a * a, _same_head()) + 1e-6)
        cqkv_ref[0, r0:r0 + sub, 0:C_WIDTH] = l2(qh) * (C_HEAD_DIM ** -0.5)
        cqkv_ref[0, r0:r0 + sub, C_WIDTH:2 * C_WIDTH] = l2(kh)
        cqkv_ref[0, r0:r0 + sub, 2 * C_WIDTH:] = act[:, 2 * C_WIDTH:]

    def start(name, lo, hi):
        def go():
            vals[name] = proj(h, lo, hi)
        return go

    def finish_g():
        g_ref[0] = vals.pop("g")

    bq = OFF_B
    order = [start_c, start("a", OFF_A, OFF_B)]
    convs = [functools.partial(finish_c, r0) for r0 in range(0, t, sub)]
    later = [start("q", bq, bq + B_WIDTH), start("k", bq + B_WIDTH, bq + 2 * B_WIDTH),
             start("v", bq + 2 * B_WIDTH, bq + 3 * B_WIDTH), start("z", bq + 3 * B_WIDTH, OFF_C),
             start("g", OFF_G, D_IN_PAD)]
    fins = convs + [finish_a, finish_q, finish_k, finish_v, finish_z, finish_g]
    while later or fins:
        if later:
            order.append(later.pop(0))
        if fins:
            order.append(fins.pop(0))
    for step in order:
        step()


def _proj(x, mod, nw, w_in, tabs, lnw, aws, abias, convw, *, rope, tile):
    bsz, n, d = x.shape
    cos, sinn, sinp = tabs
    mod_map = (lambda i, b: (b, 0, 0)) if mod.shape[0] == bsz else (lambda i, b: (0, 0, 0))
    tok = lambda w: pl.BlockSpec((1, tile, w), lambda i, b: (b, i, 0))
    per = tile // HALO
    before = pl.BlockSpec((1, HALO, d), lambda i, b: (b, jnp.maximum(i * per - 1, 0), 0))
    after = pl.BlockSpec((1, HALO, d), lambda i, b: (b, jnp.minimum((i + 1) * per, n // HALO - 1), 0))
    const = lambda a: pl.BlockSpec(a.shape, lambda i, b: (0,) * a.ndim)
    tab = pl.BlockSpec((tile, B_WIDTH), lambda i, b: (i, 0))
    outs = [(A_WIDTH, MXU), (B_WIDTH, MXU), (B_WIDTH, MXU), (B_WIDTH, MXU), (B_WIDTH, F32),
            (3 * C_WIDTH, F32), (C_WIDTH, F32), (GATE_PAD, F32)]
    out_specs = [tok(w) for w, _ in outs]
    out_shape = [jax.ShapeDtypeStruct((bsz, n, w), dt) for w, dt in outs]
    out_specs[3] = pl.BlockSpec((1, B_WIDTH, tile), lambda i, b: (b, 0, i))
    out_shape[3] = jax.ShapeDtypeStruct((bsz, B_WIDTH, n), MXU)
    return pl.pallas_call(
        functools.partial(_proj_kernel, rope=rope),
        grid=(n // tile, bsz),
        in_specs=[tok(d), before, after, pl.BlockSpec((1, 1, 3 * d), mod_map), const(nw), const(w_in), tab, tab, tab,
                  const(lnw), const(aws), const(abias), const(convw)],
        out_specs=out_specs,
        out_shape=out_shape,
        compiler_params=pltpu.CompilerParams(dimension_semantics=("arbitrary", "arbitrary"),
                                             vmem_limit_bytes=VMEM_LIMIT),
        name="norm_inproj_rope" if rope else "norm_inproj_ctx",
    )(x, x, x, mod, nw, w_in, cos, sinn, sinp, lnw, aws, abias, convw)


def _interleave(gens):
    gens = list(gens)
    while gens:
        for g in list(gens):
            if next(g, StopIteration) is StopIteration:
                gens.remove(g)


def _attn_kernel(*refs, seg_lens, lam_init):
    nseg = len(seg_lens)
    q_ref, z_ref, lam_ref, sw_ref = refs[:4]
    k_refs, v_refs = refs[4:4 + nseg], refs[4 + nseg:4 + 2 * nseg]
    o_ref, s_buf, e_buf = refs[4 + 2 * nseg:]
    lf = lam_ref[...]
    lam = (jnp.exp(jnp.sum(lf[0:1] * lf[1:2], axis=-1, keepdims=True))
           - jnp.exp(jnp.sum(lf[2:3] * lf[3:4], axis=-1, keepdims=True)) + lam_init)
    tq = ATT_QTILE
    ntile = q_ref.shape[1] // tq
    pair = 2 * B_HEAD_DIM
    first = _iota((1, pair), 1) < B_HEAD_DIM
    zero = jnp.zeros((), q_ref.dtype)
    chunks, row = [], 0
    for seg, n in enumerate(seg_lens):
        for c0 in range(0, n, ATT_KCHUNK):
            chunks.append((seg, c0, row))
            row += ATT_KCHUNK
    kc = ATT_KCHUNK
    maxes, heads = {}, {}

    dens = {}

    def scores(i):
        hh, j = divmod(i, 2)
        qt, hp = divmod(hh, B_HEADS)
        lanes = slice(hp * pair, (hp + 1) * pair)
        qm = jnp.where(first if j == 0 else jnp.logical_not(first), q_ref[0, qt * tq:(qt + 1) * tq, lanes], zero)
        m, row = None, 0
        for seg, n in enumerate(seg_lens):
            for c0 in range(0, n, ATT_SCHUNK):
                size = min(ATT_SCHUNK, n - c0)
                s = _mm_nt(k_refs[seg][0, c0:c0 + size, lanes], qm)
                for r in range(0, size, kc):
                    s_buf[hh % 2, j, row:row + kc, :] = s[r:r + kc]
                    cm = jnp.max(s[r:r + kc].reshape(kc // SUBLANES, SUBLANES, tq), axis=0)
                    m = cm if m is None else jnp.maximum(m, cm)
                    row += kc
                    yield
        maxes[i] = jnp.max(m, axis=0, keepdims=True)

    def exps(i):
        hh, j = divmod(i, 2)
        den = None
        for _, _, r0 in chunks:
            e = jnp.exp2(s_buf[hh % 2, j, r0:r0 + kc, :] - maxes[i])
            e_buf[hh % 2, j, r0:r0 + kc, :] = e.astype(MXU)
            cs = jnp.sum(e.reshape(kc // SUBLANES, SUBLANES, tq), axis=0)
            den = cs if den is None else den + cs
            yield
        dens[i] = jnp.sum(den, axis=0, keepdims=True)

    def values(hh):
        qt, hp = divmod(hh, B_HEADS)
        lanes = slice(hp * pair, (hp + 1) * pair)
        d0, d1 = dens[2 * hh], dens[2 * hh + 1]
        coef = (lam * d0 * (1.0 / d1)).astype(MXU)
        pv = None
        for seg, c0, r0 in chunks:
            p = e_buf[hh % 2, 0, r0:r0 + kc, :] - coef * e_buf[hh % 2, 1, r0:r0 + kc, :]
            t = _mm(v_refs[seg][0, lanes, c0:c0 + kc], p)
            pv = t if pv is None else pv + t
            yield
        acc = (pv * (1.0 / d0)).T
        o = acc * lax.rsqrt(jnp.mean(acc * acc, axis=-1, keepdims=True) + NORM_EPS) * sw_ref[...]
        heads[hh] = o * (1.0 - lam_init)
        if hp == B_HEADS - 1:
            rows = slice(qt * tq, (qt + 1) * tq)
            o_all = jnp.concatenate([heads.pop(qt * B_HEADS + p) for p in range(B_HEADS)], axis=-1)
            o_ref[0, rows, :] = (o_all * _silu(z_ref[0, rows, :])).astype(o_ref.dtype)

    nh = 2 * B_HEADS * ntile
    for t in range(nh + 2):
        stage = []
        if t < nh:
            stage.append(scores(t))
        if 1 <= t <= nh:
            stage.append(exps(t - 1))
        if t >= 3 and t % 2 == 1:
            stage.append(values((t - 3) // 2))
        _interleave(stage)


def _attn(q, z, lam_p, subln_w, ks, vs, *, lam_init, tile):
    bsz, n, w = q.shape
    seg_lens = tuple(a.shape[1] for a in ks)
    tok = pl.BlockSpec((1, tile, w), lambda b, i: (b, i, 0))
    const = lambda a: pl.BlockSpec(a.shape, lambda b, i: (0,) * a.ndim)
    full = lambda a: pl.BlockSpec((1,) + a.shape[1:], lambda b, i: (b, 0, 0))
    return pl.pallas_call(
        functools.partial(_attn_kernel, seg_lens=seg_lens, lam_init=lam_init),
        grid=(bsz, n // tile),
        in_specs=[tok, tok, const(lam_p), const(subln_w)] + [full(a) for a in ks] + [full(a) for a in vs],
        out_specs=tok,
        out_shape=jax.ShapeDtypeStruct((bsz, n, w), MXU),
        scratch_shapes=[pltpu.VMEM((2, 2, sum(seg_lens), ATT_QTILE), F32),
                        pltpu.VMEM((2, 2, sum(seg_lens), ATT_QTILE), MXU)],
        compiler_params=pltpu.CompilerParams(dimension_semantics=("arbitrary", "arbitrary"),
                                             vmem_limit_bytes=VMEM_LIMIT),
        name="diff_attention_%dseg" % len(ks),
    )(q, z, lam_p, subln_w, *ks, *vs)


def _head_of_lane(shape, dim):
    return _iota(shape, dim) // C_HEAD_DIM


def _expand_heads(src, first):
    r, w = src.shape
    lane = _iota((r, w), 1)
    halves = [jnp.take_along_axis(src, first + (lane + off) // C_HEAD_DIM, axis=1) for off in range(0, C_WIDTH, w)]
    return jnp.concatenate(halves, axis=1)


def _split3(a):
    a1 = a.astype(MXU)
    r1 = a - a1.astype(F32)
    a2 = r1.astype(MXU)
    return a1, a2, (r1 - a2.astype(F32)).astype(MXU)


def _mm_sel(a, sel):
    hi, lo = _split(a)
    return jnp.dot(hi, sel, preferred_element_type=F32) + jnp.dot(lo, sel, preferred_element_type=F32)


def _same_head():
    return (_head_of_lane((C_WIDTH, C_WIDTH), 0) == _head_of_lane((C_WIDTH, C_WIDTH), 1)).astype(MXU)


def _gdn_prep(qkv_ref, gate_ref, blocks, off, alog_ref, dtb_ref, q_s, k_s, v_s, bf_s, bb_s, gf_s, gb_s):
    rows = GDN_BLOCK
    ri, rj = _iota((rows, rows), 0), _iota((rows, rows), 1)
    same_chunk = (ri // C_CHUNK) == (rj // C_CHUNK)
    csum = jnp.concatenate([(same_chunk & (rj <= ri)).astype(MXU), (same_chunk & (rj >= ri)).astype(MXU)], axis=0)

    for i in blocks:
        src = slice(i * rows, (i + 1) * rows)
        dst = slice(off + i * rows, off + (i + 1) * rows)
        q_s[dst, :] = qkv_ref[0, src, 0:C_WIDTH]
        k_s[dst, :] = qkv_ref[0, src, C_WIDTH:2 * C_WIDTH]
        v_s[dst, :] = qkv_ref[0, src, 2 * C_WIDTH:]
        gt = gate_ref[0, src, :]
        h = C_HEADS
        beta = _sigmoid(gt)
        bf_s[dst, :] = _expand_heads(beta, 0)
        bb_s[dst, :] = _expand_heads(beta, h)
        sp = jnp.maximum(gt + dtb_ref[...], 0.0) + jnp.log1p(jnp.exp(-jnp.abs(gt + dtb_ref[...])))
        g = -jnp.exp(alog_ref[...]) * sp
        w = GATE_PAD
        r = jnp.dot(csum, jnp.concatenate(_split3(g), axis=1), preferred_element_type=F32)
        gc = r[:, :w] + (r[:, w:2 * w] + r[:, 2 * w:])
        gf_s[dst, :] = _expand_heads(gc[:rows], 2 * h)
        gb_s[dst, :] = _expand_heads(gc[rows:], 3 * h)
        yield


def _block_diag(a, bmask):
    return jnp.where(bmask, jnp.concatenate([a] * C_HEADS, axis=0), 0.0)


def _gdn_local(chains, q_s, k_s, v_s, bufs, buf):
    c, wd = C_CHUNK, C_WIDTH
    u_buf, w_buf, a_buf, qd_buf, kd_buf, eg_buf = bufs
    ii = _iota((c, wd), 0)
    jj = _iota((c, wd), 1) % c
    eye = jj == ii
    bmask = _head_of_lane((C_HEADS * c, wd), 1) == (_iota((C_HEADS * c, wd), 0) // c)
    bd = lambda a: _block_diag(a, bmask)
    ident = jnp.where(eye, 1.0, 0.0)

    st = []
    for rows, backward, beta_s, gc_s, _, _ in chains:
        q, k, v, beta, gc = q_s[rows, :], k_s[rows, :], v_s[rows, :], beta_s[rows, :], gc_s[rows, :]
        incl = (jj >= ii) if backward else (jj <= ii)
        strict = (jj > ii) if backward else (jj < ii)
        grow = jnp.sum(jnp.where(eye, gc, 0.0), axis=0, keepdims=True)
        decay = jnp.exp(jnp.where(incl, gc - grow, NEG_BIG))
        glast = gc[0:1] if backward else gc[c - 1:c]
        st.append(dict(q=q, k=k, v=v, beta=beta, gc=gc, decay=decay, strict=strict, glast=glast, kb=k * beta,
                       egc=jnp.exp(gc)))
    yield
    for s in st:
        qk = _mm_nt(jnp.concatenate([s["kb"], s["q"]], axis=0), bd(s["k"]))
        s["lmat"] = jnp.where(s["strict"], qk[:c] * s["decay"], 0.0)
        s["attn"] = qk[c:] * s["decay"]
    yield
    in_blk = (ii // INV_BLOCK) == (jj // INV_BLOCK)
    for s in st:
        dmat = jnp.where(in_blk, s["lmat"], 0.0)
        s["t"] = ident - dmat
        s["pw"] = _mm(dmat, bd(dmat))
    yield
    nsq = int(math.log2(INV_BLOCK)) - 1
    for step in range(nsq):
        for s in st:
            b = bd(s["pw"])
            if step < nsq - 1:
                r = _mm(jnp.concatenate([s["t"], s["pw"]], axis=0), b)
                s["t"], s["pw"] = s["t"] + r[:c], r[c:]
            else:
                s["t"] = s["t"] + _mm(s["t"], b)
        yield
    size = INV_BLOCK
    while size < c:
        emask = ((ii // (2 * size)) == (jj // (2 * size))) & ((ii // size) != (jj // size))
        for s in st:
            s["y"] = _mm(jnp.where(emask, s["lmat"], 0.0), bd(s["t"]))
        yield
        for s in st:
            s["t"] = s["t"] - _mm(s["t"], bd(s["y"]))
        yield
        size *= 2
    for s, (_, _, _, _, d, slot) in zip(st, chains):
        dst = slice(slot * c, (slot + 1) * c)
        u_buf[buf, d, dst, :] = _mm(s["t"], bd(s["v"] * s["beta"]))
        a_buf[buf, d, dst, :] = s["attn"].astype(MXU)
        qd_buf[buf, d, dst, :] = (s["q"] * s["egc"]).astype(MXU)
    yield
    for s, (_, _, _, _, d, slot) in zip(st, chains):
        dst = slice(slot * c, (slot + 1) * c)
        w_buf[buf, d, dst, :] = _mm(s["t"], bd(s["kb"] * s["egc"])).astype(MXU)
        kd_buf[buf, d, dst, :] = (s["k"] * jnp.exp(s["glast"] - s["gc"])).astype(MXU)
        eg_buf[buf, d, slot * HALO:(slot + 1) * HALO, :] = jnp.broadcast_to(jnp.exp(s["glast"]), (HALO, wd))


def _gdn_scan(steps, rows, st_s, o_refs, bufs, buf):
    c, wd = C_CHUNK, C_WIDTH
    u_buf, w_buf, a_buf, qd_buf, kd_buf, eg_buf = bufs
    bmask = _head_of_lane((C_HEADS * c, wd), 1) == (_iota((C_HEADS * c, wd), 0) // c)
    smask = _head_of_lane((wd, wd), 0) == _head_of_lane((wd, wd), 1)
    states = [st_s[0], st_s[1]]
    for slots in steps:
        src = [slice(slot * c, (slot + 1) * c) for slot in slots]
        ws = [_mm(jnp.concatenate([w_buf[buf, d, r, :], qd_buf[buf, d, r, :]], axis=0), states[d])
              for d, r in enumerate(src)]
        yield
        v_new = [u_buf[buf, d, r, :] - ws[d][:c] for d, r in enumerate(src)]
        upd = [_mm_tn(kd_buf[buf, d, r, :], v_new[d]) for d, r in enumerate(src)]
        for d, (r, slot) in enumerate(zip(src, slots)):
            o_refs[d][rows[d][slot], :] = ws[d][c:] + _mm(a_buf[buf, d, r, :], _block_diag(v_new[d], bmask))
        yield
        states = [states[d] * eg_buf[buf, d, slot * HALO:slot * HALO + 1, :] + jnp.where(smask, upd[d], 0.0)
                  for d, slot in enumerate(slots)]
    st_s[0], st_s[1] = states


def _gdn_kernel(*refs, n_ctx, n_lat, ctx_out):
    (rawc_ref, gtc_ref, zc_ref, rawl_ref, gtl_ref, zl_ref, alog_ref, dtb_ref, nw_ref) = refs[:9]
    nout = 2 if ctx_out else 1
    out_refs = refs[9:9 + nout]
    (q_s, k_s, v_s, bf_s, bb_s, gf_s, gb_s, of_s, ob_s, st_s) = refs[9 + nout:19 + nout]
    scr = (q_s, k_s, v_s, bf_s, bb_s, gf_s, gb_s)
    bufs = refs[19 + nout:]
    ncb, nlb = n_ctx // GDN_BLOCK, n_lat // GDN_BLOCK
    per = GDN_BLOCK // C_CHUNK
    _interleave([_gdn_prep(rawc_ref, gtc_ref, range(ncb), 0, alog_ref, dtb_ref, *scr)])
    st_s[...] = jnp.zeros_like(st_s)

    nblk = ncb + nlb
    steps = [(t, per - 1 - t) for t in range(per)]

    def block_rows(i):
        if isinstance(i, int):
            bf = i * GDN_BLOCK
            bb = (ncb - 1 - i if i < ncb else 2 * ncb + nlb - 1 - i) * GDN_BLOCK
        else:
            bf = pl.multiple_of(i * GDN_BLOCK, GDN_BLOCK)
            bb = pl.multiple_of(jnp.where(i < ncb, ncb - 1 - i, 2 * ncb + nlb - 1 - i) * GDN_BLOCK, GDN_BLOCK)
        return ([pl.ds(bf + r * C_CHUNK, C_CHUNK) for r in range(per)],
                [pl.ds(bb + r * C_CHUNK, C_CHUNK) for r in range(per)])

    def local(i):
        rows_f, rows_b = block_rows(i)
        chains = [c for r in range(per) for c in ((rows_f[r], False, bf_s, gf_s, 0, r), (rows_b[r], True, bb_s, gb_s, 1, r))]
        for lo in range(0, len(chains), GDN_LOCKSTEP):
            yield from _gdn_local(chains[lo:lo + GDN_LOCKSTEP], q_s, k_s, v_s, bufs, i % 2)

    def scan(i):
        return _gdn_scan(steps, block_rows(i), st_s, (of_s, ob_s), bufs, i % 2)

    todo = list(range(nlb))

    def prep_for(steps_ahead):
        blocks = []
        for s in steps_ahead:
            for b in (s, 2 * ncb + nlb - 1 - s):
                if ncb <= b < nblk and b - ncb in todo:
                    todo.remove(b - ncb)
                    blocks.append(b - ncb)
        return _gdn_prep(rawl_ref, gtl_ref, blocks, n_ctx, alog_ref, dtb_ref, *scr)

    _interleave([local(0), prep_for((1, 2))])
    peeled = 0
    while todo:
        _interleave([local(peeled + 1), scan(peeled), prep_for((peeled + 3,))])
        peeled += 1

    def block(i, carry):
        _interleave([local(i + 1), scan(i)])
        return carry

    lax.fori_loop(peeled, nblk - 1, block, 0)

    same_head = _same_head()

    def finish(blocks):
        for b in blocks:
            src = slice(b * GDN_BLOCK, (b + 1) * GDN_BLOCK)
            if b >= ncb:
                z_ref, out_ref, dst = zl_ref, out_refs[0], slice((b - ncb) * GDN_BLOCK, (b - ncb + 1) * GDN_BLOCK)
            elif ctx_out:
                z_ref, out_ref, dst = zc_ref, out_refs[1], src
            else:
                continue
            o = of_s[src, :] + ob_s[src, :]
            ms = _mm_sel(o * o, same_head) * (1.0 / C_HEAD_DIM)
            y = o * lax.rsqrt(ms + NORM_EPS) * nw_ref[...]
            out_ref[0, dst, :] = (y * _silu(z_ref[0, dst, :])).astype(out_ref.dtype)
            yield

    late = sorted({nblk - 1, ncb})
    _interleave([scan(nblk - 1), finish([b for b in range(nblk) if b not in late])])
    _interleave([finish(late)])


def _gdn(rawc, gtc, zc, rawl, gtl, zl, alog, dtb, nw, *, ctx_out):
    bsz, n_ctx, _ = rawc.shape
    n_lat = rawl.shape[1]
    nt = n_ctx + n_lat
    seq = lambda a: pl.BlockSpec((1,) + a.shape[1:], lambda b: (b, 0, 0))
    const = lambda a: pl.BlockSpec(a.shape, lambda b: (0,) * a.ndim)
    out_shape = [jax.ShapeDtypeStruct((bsz, n_lat, C_WIDTH), MXU)]
    if ctx_out:
        out_shape.append(jax.ShapeDtypeStruct((bsz, n_ctx, C_WIDTH), MXU))
    big = lambda: pltpu.VMEM((nt, C_WIDTH), F32)
    loc = lambda dt: pltpu.VMEM((2, 2, GDN_BLOCK, C_WIDTH), dt)
    per = GDN_BLOCK // C_CHUNK
    res = pl.pallas_call(
        functools.partial(_gdn_kernel, n_ctx=n_ctx, n_lat=n_lat, ctx_out=ctx_out),
        grid=(bsz,),
        in_specs=[seq(rawc), seq(gtc), seq(zc), seq(rawl), seq(gtl), seq(zl),
                  const(alog), const(dtb), const(nw)],
        out_specs=[seq(s) for s in out_shape],
        out_shape=out_shape,
        scratch_shapes=[big() for _ in range(9)] + [pltpu.VMEM((2, C_WIDTH, C_WIDTH), F32)]
        + [loc(F32), loc(MXU), loc(MXU), loc(MXU), loc(MXU), pltpu.VMEM((2, 2, per * HALO, C_WIDTH), F32)],
        compiler_params=pltpu.CompilerParams(dimension_semantics=("arbitrary",), vmem_limit_bytes=VMEM_LIMIT),
        name="gated_deltanet_bidir",
    )(rawc, gtc, zc, rawl, gtl, zl, alog, dtb, nw)
    return res


def _out_kernel(x_ref, mod_ref, ya_ref, yb_ref, yc_ref, w_ref, fw_ref, o_ref, *, final):
    d = D_MODEL
    y = (jnp.dot(ya_ref[0], w_ref[0:A_WIDTH], preferred_element_type=F32)
         + jnp.dot(yb_ref[0], w_ref[A_WIDTH:A_WIDTH + B_WIDTH], preferred_element_type=F32)
         + jnp.dot(yc_ref[0], w_ref[A_WIDTH + B_WIDTH:], preferred_element_type=F32))
    x = x_ref[0] + mod_ref[0][:, 2 * d:] * y
    if final:
        x = x * lax.rsqrt(jnp.mean(x * x, axis=-1, keepdims=True) + NORM_EPS) * fw_ref[...]
    o_ref[0] = x


def _out(x, mod, ya, yb, yc, w_out, fw, *, final, tile):
    bsz, n, d = x.shape
    mod_map = (lambda b, i: (b, 0, 0)) if mod.shape[0] == bsz else (lambda b, i: (0, 0, 0))
    tok = lambda w: pl.BlockSpec((1, tile, w), lambda b, i: (b, i, 0))
    const = lambda a: pl.BlockSpec(a.shape, lambda b, i: (0,) * a.ndim)
    return pl.pallas_call(
        functools.partial(_out_kernel, final=final),
        grid=(bsz, n // tile),
        in_specs=[tok(d), pl.BlockSpec((1, 1, 3 * d), mod_map), tok(A_WIDTH), tok(B_WIDTH), tok(C_WIDTH),
                  const(w_out), const(fw)],
        out_specs=tok(d),
        out_shape=jax.ShapeDtypeStruct((bsz, n, d), F32),
        compiler_params=pltpu.CompilerParams(dimension_semantics=("arbitrary", "arbitrary"),
                                             vmem_limit_bytes=VMEM_LIMIT),
        name="outproj_residual_final" if final else "outproj_residual",
    )(x, mod, ya, yb, yc, w_out, fw)


def _rope_tables(n):
    pos = jnp.arange(n)
    half = B_HEAD_DIM // 2
    inv_freq = ROPE_THETA ** (-jnp.arange(0, half, 2, dtype=F32) / half)
    lane = np.arange(B_WIDTH)
    dd = lane % B_HEAD_DIM
    use_col = (dd // half) == 1
    second = ((dd % half) // (half // 2)) == 1
    fidx = dd % (half // 2)
    p = jnp.where(use_col[None, :], (pos % GRID_W)[:, None], (pos // GRID_W)[:, None]).astype(F32)
    ang = p * inv_freq[fidx][None, :]
    cos, sin = jnp.cos(ang), jnp.sin(ang)
    return cos, jnp.where(second[None, :], 0.0, -sin), jnp.where(second[None, :], sin, 0.0)


def kernel(x, c, ctx, c_ctx, w_ada, b_ada, norm_w, w_in, w_out, a_ln_w, a_ws, a_bs, b_lam, b_subln_w,
           c_conv_w, c_a_log, c_dt_bias, c_norm_w, final_norm_w):
    bsz, n, d = x.shape
    n_ctx = ctx.shape[1]
    depth = w_ada.shape[0]
    assert d == D_MODEL and n % GDN_BLOCK == 0 and n_ctx % GDN_BLOCK == 0 and n % GRID_W == 0
    tile = ATT_QTILE * ATT_TILES
    tile_ctx = min(ATT_QTILE, n_ctx)
    ptile = min(PROJ_TILE, n)
    otile = next(t for t in (OUT_TILE, PROJ_TILE, GDN_BLOCK) if n % t == 0)
    assert n % ptile == 0 and n % tile == 0

    rows = bsz + 1
    rows_pad = -(-rows // 8) * 8
    c_rows = jnp.concatenate([c, c_ctx[None, :], jnp.zeros((rows_pad - rows, d), F32)], axis=0)
    mods = _ada(c_rows, w_ada, b_ada)

    tabs = _rope_tables(n)
    tabs_ctx = tuple(t[:tile_ctx] for t in tabs)
    gate_pad = jnp.zeros((depth, d, GATE_PAD - N_GATE), w_in.dtype)
    w_in_p = jnp.concatenate([w_in, gate_pad], axis=-1).astype(MXU)
    w_out_b = w_out.astype(MXU)
    aws = a_ws.reshape(depth, A_GROUPS * A_CHUNK, A_CHUNK).astype(MXU)
    abias = jnp.repeat(jnp.swapaxes(a_bs, 1, 2), A_WIDTH // A_GROUPS, axis=2)
    row = lambda a: a.reshape(1, -1)

    xc = ctx
    for i in range(depth):
        lam_init = 0.8 - 0.6 * math.exp(-0.3 * i)
        ctx_out = i < depth - 1
        mod_l = mods[i, :bsz][:, None, :]
        mod_c = mods[i, bsz:bsz + 1][:, None, :]
        common = (row(norm_w[i]), w_in_p[i])
        a_par = (row(a_ln_w[i]), aws[i], abias[i], c_conv_w[i])
        ya, qb, kb, vb, zb, craw, zc, gt = _proj(x, mod_l, *common, tabs, *a_par, rope=True, tile=ptile)
        ya_c, qb_c, kb_c, vb_c, zb_c, craw_c, zc_c, gt_c = _proj(xc, mod_c, *common, tabs_ctx, *a_par,
                                                                 rope=False, tile=tile_ctx)
        lam_p, sw = b_lam[i], row(b_subln_w[i])
        yb = _attn(qb, zb, lam_p, sw, [kb_c, kb], [vb_c, vb], lam_init=lam_init, tile=tile)
        gate_row = lambda p: jnp.concatenate([jnp.zeros((2 * C_HEADS,), F32), p.reshape(-1),
                                              jnp.zeros((GATE_PAD - N_GATE,), F32)])[None, :]
        gdn_par = (gate_row(c_a_log[i]), gate_row(c_dt_bias[i]), jnp.tile(c_norm_w[i], C_HEADS)[None, :])
        yc_all = _gdn(craw_c, gt_c, zc_c, craw, gt, zc, *gdn_par, ctx_out=ctx_out)
        final = i == depth - 1
        x = _out(x, mod_l, ya, yb, yc_all[0], w_out_b[i], row(final_norm_w), final=final, tile=otile)
        if ctx_out:
            yb_c = _attn(qb_c, zb_c, lam_p, sw, [kb_c], [vb_c], lam_init=lam_init, tile=tile_ctx)
            xc = _out(xc, mod_c, ya_c, yb_c, yc_all[1], w_out_b[i], row(final_norm_w), final=False, tile=tile_ctx)
    return x
```

```python
import functools
import math

import numpy as np
import jax
import jax.numpy as jnp
from jax import lax
from jax.experimental import pallas as pl
from jax.experimental.pallas import tpu as pltpu

F32 = jnp.float32
MXU = jnp.bfloat16

D_MODEL = 1024
NORM_EPS = 1e-6
GRID_W = 64
A_WIDTH, A_GROUPS, A_CHUNK, A_LN_EPS = 256, 4, 128, 1e-5
B_HEADS, B_HEAD_DIM, B_WIDTH = 4, 64, 512
ROPE_THETA = 10000.0
C_HEADS, C_HEAD_DIM, C_WIDTH, C_CONV, C_CHUNK = 4, 64, 256, 5, 64
N_GATE = 4 * C_HEADS
GATE_PAD = 128
OFF_A, OFF_B, OFF_C, OFF_G = 0, 3 * A_WIDTH, 3 * A_WIDTH + 4 * B_WIDTH, 3 * A_WIDTH + 4 * B_WIDTH + 4 * C_WIDTH
D_IN_PAD = OFF_G + GATE_PAD
NEG_BIG = -1e30
LOG2E = math.log2(math.e)
ATT_KCHUNK = 256
ATT_SCHUNK = 256
ATT_QTILE = 256
ATT_TILES = 2
PROJ_TILE = 512
OUT_TILE = 2048
CONV_ROWS = 256
VMEM_LIMIT = 56 * 1024 * 1024
GDN_BLOCK = 256
GDN_LOCKSTEP = 8
SUBLANES = 8
HALO = SUBLANES
ADA_TILE = 512
INV_BLOCK = 8


def _mm(a, b):
    return jnp.dot(a.astype(MXU), b.astype(MXU), preferred_element_type=F32)


def _mm_nt(a, b):
    return lax.dot_general(a.astype(MXU), b.astype(MXU), (((1,), (1,)), ((), ())), preferred_element_type=F32)


def _mm_tn(a, b):
    return lax.dot_general(a.astype(MXU), b.astype(MXU), (((0,), (0,)), ((), ())), preferred_element_type=F32)


def _split(a):
    hi = a.astype(MXU)
    return hi, (a - hi.astype(F32)).astype(MXU)


def _mm3(a, b):
    ah, al = _split(a)
    bh, bl = _split(b)
    d = lambda x, y: jnp.dot(x, y, preferred_element_type=F32)
    return d(ah, bh) + (d(ah, bl) + d(al, bh))


def _sigmoid(x):
    return 1.0 / (1.0 + jnp.exp(-x))


def _silu(x):
    return x * _sigmoid(x)


def _iota(shape, dim):
    return lax.broadcasted_iota(jnp.int32, shape, dim)


def _ada_kernel(c_ref, w_ref, b_ref, o_ref):
    o_ref[0] = _mm3(_silu(c_ref[...]), w_ref[0]) + b_ref[0]


def _ada(c_rows, w_ada, b_ada):
    depth, d, d3 = w_ada.shape
    r = c_rows.shape[0]
    tn = ADA_TILE
    return pl.pallas_call(
        _ada_kernel,
        grid=(depth, d3 // tn),
        in_specs=[pl.BlockSpec((r, d), lambda l, j: (0, 0)),
                  pl.BlockSpec((1, d, tn), lambda l, j: (l, 0, j)),
                  pl.BlockSpec((1, 1, tn), lambda l, j: (l, 0, j))],
        out_specs=pl.BlockSpec((1, r, tn), lambda l, j: (l, 0, j)),
        out_shape=jax.ShapeDtypeStruct((depth, r, d3), F32),
        name="ada_modulation",
    )(c_rows, w_ada, b_ada.reshape(depth, 1, d3))


def _proj_kernel(x_ref, xp_ref, xn_ref, mod_ref, nw_ref, w_ref, cos_ref, sinn_ref, sinp_ref, lnw_ref, aws_ref,
                 abias_ref, convw_ref, ya_ref, q_ref, k_ref, v_ref, zb_ref, cqkv_ref, zc_ref, g_ref, *, rope):
    d = D_MODEL
    mod = mod_ref[0]

    def normed(x):
        y = x * lax.rsqrt(jnp.mean(x * x, axis=-1, keepdims=True) + NORM_EPS) * nw_ref[...]
        return (y * (1.0 + mod[:, d:2 * d]) + mod[:, :d]).astype(MXU)

    t = x_ref.shape[1]
    half_t = t // 2
    proj = lambda lhs, lo, hi: jnp.dot(lhs, w_ref[:, lo:hi], preferred_element_type=F32)
    vals = {}
    h0 = normed(x_ref[0, :half_t])
    pc0 = proj(jnp.concatenate([h0, normed(xp_ref[0])], axis=0), OFF_C, OFF_G)
    h1 = normed(x_ref[0, half_t:])
    pc1 = proj(jnp.concatenate([h1, normed(xn_ref[0])], axis=0), OFF_C, OFF_G)
    h = jnp.concatenate([h0, h1], axis=0)

    def finish_a():
        pa = vals.pop("a")
        u, v, z = pa[:, :A_WIDTH], pa[:, A_WIDTH:2 * A_WIDTH], pa[:, 2 * A_WIDTH:]
        vc = v - jnp.mean(v, axis=-1, keepdims=True)
        vn = (vc * lax.rsqrt(jnp.mean(vc * vc, axis=-1, keepdims=True) + A_LN_EPS) * lnw_ref[...]).astype(MXU)
        grp = _iota((A_CHUNK, A_WIDTH), 1) // (A_WIDTH // A_GROUPS)
        for c in range(t // A_CHUNK):
            rows = slice(c * A_CHUNK, (c + 1) * A_CHUNK)
            r = jnp.dot(aws_ref[...], vn[rows], preferred_element_type=F32)
            s = abias_ref[...]
            for g in range(A_GROUPS):
                s = s + jnp.where(grp == g, r[g * A_CHUNK:(g + 1) * A_CHUNK], 0.0)
            ya_ref[0, rows, :] = (u[rows] * s * _silu(z[rows])).astype(ya_ref.dtype)

    def rot(a):
        if not rope:
            return a
        half = B_HEAD_DIM // 4
        return (a * cos_ref[...] + pltpu.roll(a, B_WIDTH - half, 1) * sinn_ref[...]
                + pltpu.roll(a, half, 1) * sinp_ref[...])

    def finish_q():
        q_ref[0] = (rot(vals.pop("q")) * (B_HEAD_DIM ** -0.5 * LOG2E)).astype(q_ref.dtype)

    def finish_k():
        k_ref[0] = rot(vals.pop("k")).astype(k_ref.dtype)

    def finish_v():
        v_ref[0] = vals.pop("v").T.astype(v_ref.dtype)

    def finish_z():
        zb_ref[0] = vals.pop("z").astype(zb_ref.dtype)

    sub = min(t, CONV_ROWS)

    def start_c():
        i, last = pl.program_id(0), pl.num_programs(0) - 1
        zc_ref[0, :half_t, :] = pc0[:half_t, 3 * C_WIDTH:]
        zc_ref[0, half_t:, :] = pc1[:half_t, 3 * C_WIDTH:]
        prev = jnp.where(i == 0, 0.0, pc0[half_t:, :3 * C_WIDTH])
        nxt = jnp.where(i == last, 0.0, pc1[half_t:, :3 * C_WIDTH])
        vals["c"] = jnp.concatenate([prev, pc0[:half_t, :3 * C_WIDTH], pc1[:half_t, :3 * C_WIDTH], nxt], axis=0)

    def finish_c(r0):
        w = convw_ref[...]
        pad = C_CONV // 2
        win = vals["c"][r0:r0 + sub + 2 * HALO]
        acc = win[HALO:HALO + sub] * w[pad:pad + 1]
        for j in range(C_CONV):
            if j != pad:
                acc = acc + pltpu.roll(win, (pad - j) % (sub + 2 * HALO), 0)[HALO:HALO + sub] * w[j:j + 1]
        act = _silu(acc)
        qh, kh = act[:, :C_WIDTH], act[:, C_WIDTH:2 * C_WIDTH]
        l2 = lambda a: a * lax.rsqrt(_mm_sel(a * a, _same_head()) + 1e-6)
        cqkv_ref[0, r0:r0 + sub, 0:C_WIDTH] = l2(qh) * (C_HEAD_DIM ** -0.5)
        cqkv_ref[0, r0:r0 + sub, C_WIDTH:2 * C_WIDTH] = l2(kh)
        cqkv_ref[0, r0:r0 + sub, 2 * C_WIDTH:] = act[:, 2 * C_WIDTH:]

    def start(name, lo, hi):
        def go():
            vals[name] = proj(h, lo, hi)
        return go

    def finish_g():
        g_ref[0] = vals.pop("g")

    bq = OFF_B
    order = [start_c, start("a", OFF_A, OFF_B)]
    convs = [functools.partial(finish_c, r0) for r0 in range(0, t, sub)]
    later = [start("q", bq, bq + B_WIDTH), start("k", bq + B_WIDTH, bq + 2 * B_WIDTH),
             start("v", bq + 2 * B_WIDTH, bq + 3 * B_WIDTH), start("z", bq + 3 * B_WIDTH, OFF_C),
             start("g", OFF_G, D_IN_PAD)]
    fins = convs + [finish_a, finish_q, finish_k, finish_v, finish_z, finish_g]
    while later or fins:
        if later:
            order.append(later.pop(0))
        if fins:
            order.append(fins.pop(0))
    for step in order:
        step()


def _proj(x, mod, nw, w_in, tabs, lnw, aws, abias, convw, *, rope, tile):
    bsz, n, d = x.shape
    cos, sinn, sinp = tabs
    mod_map = (lambda i, b: (b, 0, 0)) if mod.shape[0] == bsz else (lambda i, b: (0, 0, 0))
    tok = lambda w: pl.BlockSpec((1, tile, w), lambda i, b: (b, i, 0))
    per = tile // HALO
    before = pl.BlockSpec((1, HALO, d), lambda i, b: (b, jnp.maximum(i * per - 1, 0), 0))
    after = pl.BlockSpec((1, HALO, d), lambda i, b: (b, jnp.minimum((i + 1) * per, n // HALO - 1), 0))
    const = lambda a: pl.BlockSpec(a.shape, lambda i, b: (0,) * a.ndim)
    tab = pl.BlockSpec((tile, B_WIDTH), lambda i, b: (i, 0))
    outs = [(A_WIDTH, MXU), (B_WIDTH, MXU), (B_WIDTH, MXU), (B_WIDTH, MXU), (B_WIDTH, F32),
            (3 * C_WIDTH, F32), (C_WIDTH, F32), (GATE_PAD, F32)]
    out_specs = [tok(w) for w, _ in outs]
    out_shape = [jax.ShapeDtypeStruct((bsz, n, w), dt) for w, dt in outs]
    out_specs[3] = pl.BlockSpec((1, B_WIDTH, tile), lambda i, b: (b, 0, i))
    out_shape[3] = jax.ShapeDtypeStruct((bsz, B_WIDTH, n), MXU)
    return pl.pallas_call(
        functools.partial(_proj_kernel, rope=rope),
        grid=(n // tile, bsz),
        in_specs=[tok(d), before, after, pl.BlockSpec((1, 1, 3 * d), mod_map), const(nw), const(w_in), tab, tab, tab,
                  const(lnw), const(aws), const(abias), const(convw)],
        out_specs=out_specs,
        out_shape=out_shape,
        compiler_params=pltpu.CompilerParams(dimension_semantics=("arbitrary", "arbitrary"),
                                             vmem_limit_bytes=VMEM_LIMIT),
        name="norm_inproj_rope" if rope else "norm_inproj_ctx",
    )(x, x, x, mod, nw, w_in, cos, sinn, sinp, lnw, aws, abias, convw)


def _interleave(gens):
    gens = list(gens)
    while gens:
        for g in list(gens):
            if next(g, StopIteration) is StopIteration:
                gens.remove(g)


def _attn_kernel(*refs, seg_lens, lam_init):
    nseg = len(seg_lens)
    q_ref, z_ref, lam_ref, sw_ref = refs[:4]
    k_refs, v_refs = refs[4:4 + nseg], refs[4 + nseg:4 + 2 * nseg]
    o_ref, s_buf, e_buf = refs[4 + 2 * nseg:]
    lf = lam_ref[...]
    lam = (jnp.exp(jnp.sum(lf[0:1] * lf[1:2], axis=-1, keepdims=True))
           - jnp.exp(jnp.sum(lf[2:3] * lf[3:4], axis=-1, keepdims=True)) + lam_init)
    tq = ATT_QTILE
    ntile = q_ref.shape[1] // tq
    pair = 2 * B_HEAD_DIM
    first = _iota((1, pair), 1) < B_HEAD_DIM
    zero = jnp.zeros((), q_ref.dtype)
    chunks, row = [], 0
    for seg, n in enumerate(seg_lens):
        for c0 in range(0, n, ATT_KCHUNK):
            chunks.append((seg, c0, row))
            row += ATT_KCHUNK
    kc = ATT_KCHUNK
    maxes, heads = {}, {}

    dens = {}

    def scores(i):
        hh, j = divmod(i, 2)
        qt, hp = divmod(hh, B_HEADS)
        lanes = slice(hp * pair, (hp + 1) * pair)
        qm = jnp.where(first if j == 0 else jnp.logical_not(first), q_ref[0, qt * tq:(qt + 1) * tq, lanes], zero)
        m, row = None, 0
        for seg, n in enumerate(seg_lens):
            for c0 in range(0, n, ATT_SCHUNK):
                size = min(ATT_SCHUNK, n - c0)
                s = _mm_nt(k_refs[seg][0, c0:c0 + size, lanes], qm)
                for r in range(0, size, kc):
                    s_buf[hh % 2, j, row:row + kc, :] = s[r:r + kc]
                    cm = jnp.max(s[r:r + kc].reshape(kc // SUBLANES, SUBLANES, tq), axis=0)
                    m = cm if m is None else jnp.maximum(m, cm)
                    row += kc
                    yield
        maxes[i] = jnp.max(m, axis=0, keepdims=True)

    def exps(i):
        hh, j = divmod(i, 2)
        den = None
        for _, _, r0 in chunks:
            e = jnp.exp2(s_buf[hh % 2, j, r0:r0 + kc, :] - maxes[i])
            e_buf[hh % 2, j, r0:r0 + kc, :] = e.astype(MXU)
            cs = jnp.sum(e.reshape(kc // SUBLANES, SUBLANES, tq), axis=0)
            den = cs if den is None else den + cs
            yield
        dens[i] = jnp.sum(den, axis=0, keepdims=True)

    def values(hh):
        qt, hp = divmod(hh, B_HEADS)
        lanes = slice(hp * pair, (hp + 1) * pair)
        d0, d1 = dens[2 * hh], dens[2 * hh + 1]
        coef = (lam * d0 * (1.0 / d1)).astype(MXU)
        pv = None
        for seg, c0, r0 in chunks:
            p = e_buf[hh % 2, 0, r0:r0 + kc, :] - coef * e_buf[hh % 2, 1, r0:r0 + kc, :]
            t = _mm(v_refs[seg][0, lanes, c0:c0 + kc], p)
            pv = t if pv is None else pv + t
            yield
        acc = (pv * (1.0 / d0)).T
        o = acc * lax.rsqrt(jnp.mean(acc * acc, axis=-1, keepdims=True) + NORM_EPS) * sw_ref[...]
        heads[hh] = o * (1.0 - lam_init)
        if hp == B_HEADS - 1:
            rows = slice(qt * tq, (qt + 1) * tq)
            o_all = jnp.concatenate([heads.pop(qt * B_HEADS + p) for p in range(B_HEADS)], axis=-1)
            o_ref[0, rows, :] = (o_all * _silu(z_ref[0, rows, :])).astype(o_ref.dtype)

    nh = 2 * B_HEADS * ntile
    for t in range(nh + 2):
        stage = []
        if t < nh:
            stage.append(scores(t))
        if 1 <= t <= nh:
            stage.append(exps(t - 1))
        if t >= 3 and t % 2 == 1:
            stage.append(values((t - 3) // 2))
        _interleave(stage)


def _attn(q, z, lam_p, subln_w, ks, vs, *, lam_init, tile):
    bsz, n, w = q.shape
    seg_lens = tuple(a.shape[1] for a in ks)
    tok = pl.BlockSpec((1, tile, w), lambda b, i: (b, i, 0))
    const = lambda a: pl.BlockSpec(a.shape, lambda b, i: (0,) * a.ndim)
    full = lambda a: pl.BlockSpec((1,) + a.shape[1:], lambda b, i: (b, 0, 0))
    return pl.pallas_call(
        functools.partial(_attn_kernel, seg_lens=seg_lens, lam_init=lam_init),
        grid=(bsz, n // tile),
        in_specs=[tok, tok, const(lam_p), const(subln_w)] + [full(a) for a in ks] + [full(a) for a in vs],
        out_specs=tok,
        out_shape=jax.ShapeDtypeStruct((bsz, n, w), MXU),
        scratch_shapes=[pltpu.VMEM((2, 2, sum(seg_lens), ATT_QTILE), F32),
                        pltpu.VMEM((2, 2, sum(seg_lens), ATT_QTILE), MXU)],
        compiler_params=pltpu.CompilerParams(dimension_semantics=("arbitrary", "arbitrary"),
                                             vmem_limit_bytes=VMEM_LIMIT),
        name="diff_attention_%dseg" % len(ks),
    )(q, z, lam_p, subln_w, *ks, *vs)


def _head_of_lane(shape, dim):
    return _iota(shape, dim) // C_HEAD_DIM


def _expand_heads(src, first):
    r, w = src.shape
    lane = _iota((r, w), 1)
    halves = [jnp.take_along_axis(src, first + (lane + off) // C_HEAD_DIM, axis=1) for off in range(0, C_WIDTH, w)]
    return jnp.concatenate(halves, axis=1)


def _split3(a):
    a1 = a.astype(MXU)
    r1 = a - a1.astype(F32)
    a2 = r1.astype(MXU)
    return a1, a2, (r1 - a2.astype(F32)).astype(MXU)


def _mm_sel(a, sel):
    hi, lo = _split(a)
    return jnp.dot(hi, sel, preferred_element_type=F32) + jnp.dot(lo, sel, preferred_element_type=F32)


def _same_head():
    return (_head_of_lane((C_WIDTH, C_WIDTH), 0) == _head_of_lane((C_WIDTH, C_WIDTH), 1)).astype(MXU)


def _gdn_prep(qkv_ref, gate_ref, blocks, off, alog_ref, dtb_ref, q_s, k_s, v_s, bf_s, bb_s, gf_s, gb_s):
    rows = GDN_BLOCK
    ri, rj = _iota((rows, rows), 0), _iota((rows, rows), 1)
    same_chunk = (ri // C_CHUNK) == (rj // C_CHUNK)
    csum = jnp.concatenate([(same_chunk & (rj <= ri)).astype(MXU), (same_chunk & (rj >= ri)).astype(MXU)], axis=0)

    for i in blocks:
        src = slice(i * rows, (i + 1) * rows)
        dst = slice(off + i * rows, off + (i + 1) * rows)
        q_s[dst, :] = qkv_ref[0, src, 0:C_WIDTH]
        k_s[dst, :] = qkv_ref[0, src, C_WIDTH:2 * C_WIDTH]
        v_s[dst, :] = qkv_ref[0, src, 2 * C_WIDTH:]
        gt = gate_ref[0, src, :]
        h = C_HEADS
        beta = _sigmoid(gt)
        bf_s[dst, :] = _expand_heads(beta, 0)
        bb_s[dst, :] = _expand_heads(beta, h)
        sp = jnp.maximum(gt + dtb_ref[...], 0.0) + jnp.log1p(jnp.exp(-jnp.abs(gt + dtb_ref[...])))
        g = -jnp.exp(alog_ref[...]) * sp
        w = GATE_PAD
        r = jnp.dot(csum, jnp.concatenate(_split3(g), axis=1), preferred_element_type=F32)
        gc = r[:, :w] + (r[:, w:2 * w] + r[:, 2 * w:])
        gf_s[dst, :] = _expand_heads(gc[:rows], 2 * h)
        gb_s[dst, :] = _expand_heads(gc[rows:], 3 * h)
        yield


def _block_diag(a, bmask):
    return jnp.where(bmask, jnp.concatenate([a] * C_HEADS, axis=0), 0.0)


def _gdn_local(chains, q_s, k_s, v_s, bufs, buf):
    c, wd = C_CHUNK, C_WIDTH
    u_buf, w_buf, a_buf, qd_buf, kd_buf, eg_buf = bufs
    ii = _iota((c, wd), 0)
    jj = _iota((c, wd), 1) % c
    eye = jj == ii
    bmask = _head_of_lane((C_HEADS * c, wd), 1) == (_iota((C_HEADS * c, wd), 0) // c)
    bd = lambda a: _block_diag(a, bmask)
    ident = jnp.where(eye, 1.0, 0.0)

    st = []
    for rows, backward, beta_s, gc_s, _, _ in chains:
        q, k, v, beta, gc = q_s[rows, :], k_s[rows, :], v_s[rows, :], beta_s[rows, :], gc_s[rows, :]
        incl = (jj >= ii) if backward else (jj <= ii)
        strict = (jj > ii) if backward else (jj < ii)
        grow = jnp.sum(jnp.where(eye, gc, 0.0), axis=0, keepdims=True)
        decay = jnp.exp(jnp.where(incl, gc - grow, NEG_BIG))
        glast = gc[0:1] if backward else gc[c - 1:c]
        st.append(dict(q=q, k=k, v=v, beta=beta, gc=gc, decay=decay, strict=strict, glast=glast, kb=k * beta,
                       egc=jnp.exp(gc)))
    yield
    for s in st:
        qk = _mm_nt(jnp.concatenate([s["kb"], s["q"]], axis=0), bd(s["k"]))
        s["lmat"] = jnp.where(s["strict"], qk[:c] * s["decay"], 0.0)
        s["attn"] = qk[c:] * s["decay"]
    yield
    in_blk = (ii // INV_BLOCK) == (jj // INV_BLOCK)
    for s in st:
        dmat = jnp.where(in_blk, s["lmat"], 0.0)
        s["t"] = ident - dmat
        s["pw"] = _mm(dmat, bd(dmat))
    yield
    nsq = int(math.log2(INV_BLOCK)) - 1
    for step in range(nsq):
        for s in st:
            b = bd(s["pw"])
            if step < nsq - 1:
                r = _mm(jnp.concatenate([s["t"], s["pw"]], axis=0), b)
                s["t"], s["pw"] = s["t"] + r[:c], r[c:]
            else:
                s["t"] = s["t"] + _mm(s["t"], b)
        yield
    size = INV_BLOCK
    while size < c:
        emask = ((ii // (2 * size)) == (jj // (2 * size))) & ((ii // size) != (jj // size))
        for s in st:
            s["y"] = _mm(jnp.where(emask, s["lmat"], 0.0), bd(s["t"]))
        yield
        for s in st:
            s["t"] = s["t"] - _mm(s["t"], bd(s["y"]))
        yield
        size *= 2
    for s, (_, _, _, _, d, slot) in zip(st, chains):
        dst = slice(slot * c, (slot + 1) * c)
        u_buf[buf, d, dst, :] = _mm(s["t"], bd(s["v"] * s["beta"]))
        a_buf[buf, d, dst, :] = s["attn"].astype(MXU)
        qd_buf[buf, d, dst, :] = (s["q"] * s["egc"]).astype(MXU)
    yield
    for s, (_, _, _, _, d, slot) in zip(st, chains):
        dst = slice(slot * c, (slot + 1) * c)
        w_buf[buf, d, dst, :] = _mm(s["t"], bd(s["kb"] * s["egc"])).astype(MXU)
        kd_buf[buf, d, dst, :] = (s["k"] * jnp.exp(s["glast"] - s["gc"])).astype(MXU)
        eg_buf[buf, d, slot * HALO:(slot + 1) * HALO, :] = jnp.broadcast_to(jnp.exp(s["glast"]), (HALO, wd))


def _gdn_scan(steps, rows, st_s, o_refs, bufs, buf):
    c, wd = C_CHUNK, C_WIDTH
    u_buf, w_buf, a_buf, qd_buf, kd_buf, eg_buf = bufs
    bmask = _head_of_lane((C_HEADS * c, wd), 1) == (_iota((C_HEADS * c, wd), 0) // c)
    smask = _head_of_lane((wd, wd), 0) == _head_of_lane((wd, wd), 1)
    states = [st_s[0], st_s[1]]
    for slots in steps:
        src = [slice(slot * c, (slot + 1) * c) for slot in slots]
        ws = [_mm(jnp.concatenate([w_buf[buf, d, r, :], qd_buf[buf, d, r, :]], axis=0), states[d])
              for d, r in enumerate(src)]
        yield
        v_new = [u_buf[buf, d, r, :] - ws[d][:c] for d, r in enumerate(src)]
        upd = [_mm_tn(kd_buf[buf, d, r, :], v_new[d]) for d, r in enumerate(src)]
        for d, (r, slot) in enumerate(zip(src, slots)):
            o_refs[d][rows[d][slot], :] = ws[d][c:] + _mm(a_buf[buf, d, r, :], _block_diag(v_new[d], bmask))
        yield
        states = [states[d] * eg_buf[buf, d, slot * HALO:slot * HALO + 1, :] + jnp.where(smask, upd[d], 0.0)
                  for d, slot in enumerate(slots)]
    st_s[0], st_s[1] = states


def _gdn_kernel(*refs, n_ctx, n_lat, ctx_out):
    (rawc_ref, gtc_ref, zc_ref, rawl_ref, gtl_ref, zl_ref, alog_ref, dtb_ref, nw_ref) = refs[:9]
    nout = 2 if ctx_out else 1
    out_refs = refs[9:9 + nout]
    (q_s, k_s, v_s, bf_s, bb_s, gf_s, gb_s, of_s, ob_s, st_s) = refs[9 + nout:19 + nout]
    scr = (q_s, k_s, v_s, bf_s, bb_s, gf_s, gb_s)
    bufs = refs[19 + nout:]
    ncb, nlb = n_ctx // GDN_BLOCK, n_lat // GDN_BLOCK
    per = GDN_BLOCK // C_CHUNK
    _interleave([_gdn_prep(rawc_ref, gtc_ref, range(ncb), 0, alog_ref, dtb_ref, *scr)])
    st_s[...] = jnp.zeros_like(st_s)

    nblk = ncb + nlb
    steps = [(t, per - 1 - t) for t in range(per)]

    def block_rows(i):
        if isinstance(i, int):
            bf = i * GDN_BLOCK
            bb = (ncb - 1 - i if i < ncb else 2 * ncb + nlb - 1 - i) * GDN_BLOCK
        else:
            bf = pl.multiple_of(i * GDN_BLOCK, GDN_BLOCK)
            bb = pl.multiple_of(jnp.where(i < ncb, ncb - 1 - i, 2 * ncb + nlb - 1 - i) * GDN_BLOCK, GDN_BLOCK)
        return ([pl.ds(bf + r * C_CHUNK, C_CHUNK) for r in range(per)],
                [pl.ds(bb + r * C_CHUNK, C_CHUNK) for r in range(per)])

    def local(i):
        rows_f, rows_b = block_rows(i)
        chains = [c for r in range(per) for c in ((rows_f[r], False, bf_s, gf_s, 0, r), (rows_b[r], True, bb_s, gb_s, 1, r))]
        for lo in range(0, len(chains), GDN_LOCKSTEP):
            yield from _gdn_local(chains[lo:lo + GDN_LOCKSTEP], q_s, k_s, v_s, bufs, i % 2)

    def scan(i):
        return _gdn_scan(steps, block_rows(i), st_s, (of_s, ob_s), bufs, i % 2)

    todo = list(range(nlb))

    def prep_for(steps_ahead):
        blocks = []
        for s in steps_ahead:
            for b in (s, 2 * ncb + nlb - 1 - s):
                if ncb <= b < nblk and b - ncb in todo:
                    todo.remove(b - ncb)
                    blocks.append(b - ncb)
        return _gdn_prep(rawl_ref, gtl_ref, blocks, n_ctx, alog_ref, dtb_ref, *scr)

    _interleave([local(0), prep_for((1, 2))])
    peeled = 0
    while todo:
        _interleave([local(peeled + 1), scan(peeled), prep_for((peeled + 3,))])
        peeled += 1

    def block(i, carry):
        _interleave([local(i + 1), scan(i)])
        return carry

    lax.fori_loop(peeled, nblk - 1, block, 0)

    same_head = _same_head()

    def finish(blocks):
        for b in blocks:
            src = slice(b * GDN_BLOCK, (b + 1) * GDN_BLOCK)
            if b >= ncb:
                z_ref, out_ref, dst = zl_ref, out_refs[0], slice((b - ncb) * GDN_BLOCK, (b - ncb + 1) * GDN_BLOCK)
            elif ctx_out:
                z_ref, out_ref, dst = zc_ref, out_refs[1], src
            else:
                continue
            o = of_s[src, :] + ob_s[src, :]
            ms = _mm_sel(o * o, same_head) * (1.0 / C_HEAD_DIM)
            y = o * lax.rsqrt(ms + NORM_EPS) * nw_ref[...]
            out_ref[0, dst, :] = (y * _silu(z_ref[0, dst, :])).astype(out_ref.dtype)
            yield

    late = sorted({nblk - 1, ncb})
    _interleave([scan(nblk - 1), finish([b for b in range(nblk) if b not in late])])
    _interleave([finish(late)])


def _gdn(rawc, gtc, zc, rawl, gtl, zl, alog, dtb, nw, *, ctx_out):
    bsz, n_ctx, _ = rawc.shape
    n_lat = rawl.shape[1]
    nt = n_ctx + n_lat
    seq = lambda a: pl.BlockSpec((1,) + a.shape[1:], lambda b: (b, 0, 0))
    const = lambda a: pl.BlockSpec(a.shape, lambda b: (0,) * a.ndim)
    out_shape = [jax.ShapeDtypeStruct((bsz, n_lat, C_WIDTH), MXU)]
    if ctx_out:
        out_shape.append(jax.ShapeDtypeStruct((bsz, n_ctx, C_WIDTH), MXU))
    big = lambda: pltpu.VMEM((nt, C_WIDTH), F32)
    loc = lambda dt: pltpu.VMEM((2, 2, GDN_BLOCK, C_WIDTH), dt)
    per = GDN_BLOCK // C_CHUNK
    res = pl.pallas_call(
        functools.partial(_gdn_kernel, n_ctx=n_ctx, n_lat=n_lat, ctx_out=ctx_out),
        grid=(bsz,),
        in_specs=[seq(rawc), seq(gtc), seq(zc), seq(rawl), seq(gtl), seq(zl),
                  const(alog), const(dtb), const(nw)],
        out_specs=[seq(s) for s in out_shape],
        out_shape=out_shape,
        scratch_shapes=[big() for _ in range(9)] + [pltpu.VMEM((2, C_WIDTH, C_WIDTH), F32)]
        + [loc(F32), loc(MXU), loc(MXU), loc(MXU), loc(MXU), pltpu.VMEM((2, 2, per * HALO, C_WIDTH), F32)],
        compiler_params=pltpu.CompilerParams(dimension_semantics=("arbitrary",), vmem_limit_bytes=VMEM_LIMIT),
        name="gated_deltanet_bidir",
    )(rawc, gtc, zc, rawl, gtl, zl, alog, dtb, nw)
    return res


def _out_kernel(x_ref, mod_ref, ya_ref, yb_ref, yc_ref, w_ref, fw_ref, o_ref, *, final):
    d = D_MODEL
    y = (jnp.dot(ya_ref[0], w_ref[0:A_WIDTH], preferred_element_type=F32)
         + jnp.dot(yb_ref[0], w_ref[A_WIDTH:A_WIDTH + B_WIDTH], preferred_element_type=F32)
         + jnp.dot(yc_ref[0], w_ref[A_WIDTH + B_WIDTH:], preferred_element_type=F32))
    x = x_ref[0] + mod_ref[0][:, 2 * d:] * y
    if final:
        x = x * lax.rsqrt(jnp.mean(x * x, axis=-1, keepdims=True) + NORM_EPS) * fw_ref[...]
    o_ref[0] = x


def _out(x, mod, ya, yb, yc, w_out, fw, *, final, tile):
    bsz, n, d = x.shape
    mod_map = (lambda b, i: (b, 0, 0)) if mod.shape[0] == bsz else (lambda b, i: (0, 0, 0))
    tok = lambda w: pl.BlockSpec((1, tile, w), lambda b, i: (b, i, 0))
    const = lambda a: pl.BlockSpec(a.shape, lambda b, i: (0,) * a.ndim)
    return pl.pallas_call(
        functools.partial(_out_kernel, final=final),
        grid=(bsz, n // tile),
        in_specs=[tok(d), pl.BlockSpec((1, 1, 3 * d), mod_map), tok(A_WIDTH), tok(B_WIDTH), tok(C_WIDTH),
                  const(w_out), const(fw)],
        out_specs=tok(d),
        out_shape=jax.ShapeDtypeStruct((bsz, n, d), F32),
        compiler_params=pltpu.CompilerParams(dimension_semantics=("arbitrary", "arbitrary"),
                                             vmem_limit_bytes=VMEM_LIMIT),
        name="outproj_residual_final" if final else "outproj_residual",
    )(x, mod, ya, yb, yc, w_out, fw)


def _rope_tables(n):
    pos = jnp.arange(n)
    half = B_HEAD_DIM // 2
    inv_freq = ROPE_THETA ** (-jnp.arange(0, half, 2, dtype=F32) / half)
    lane = np.arange(B_WIDTH)
    dd = lane % B_HEAD_DIM
    use_col = (dd // half) == 1
    second = ((dd % half) // (half // 2)) == 1
    fidx = dd % (half // 2)
    p = jnp.where(use_col[None, :], (pos % GRID_W)[:, None], (pos // GRID_W)[:, None]).astype(F32)
    ang = p * inv_freq[fidx][None, :]
    cos, sin = jnp.cos(ang), jnp.sin(ang)
    return cos, jnp.where(second[None, :], 0.0, -sin), jnp.where(second[None, :], sin, 0.0)


def kernel(x, c, ctx, c_ctx, w_ada, b_ada, norm_w, w_in, w_out, a_ln_w, a_ws, a_bs, b_lam, b_subln_w,
           c_conv_w, c_a_log, c_dt_bias, c_norm_w, final_norm_w):
    bsz, n, d = x.shape
    n_ctx = ctx.shape[1]
    depth = w_ada.shape[0]
    assert d == D_MODEL and n % GDN_BLOCK == 0 and n_ctx % GDN_BLOCK == 0 and n % GRID_W == 0
    tile = ATT_QTILE * ATT_TILES
    tile_ctx = min(ATT_QTILE, n_ctx)
    ptile = min(PROJ_TILE, n)
    otile = next(t for t in (OUT_TILE, PROJ_TILE, GDN_BLOCK) if n % t == 0)
    assert n % ptile == 0 and n % tile == 0

    rows = bsz + 1
    rows_pad = -(-rows // 8) * 8
    c_rows = jnp.concatenate([c, c_ctx[None, :], jnp.zeros((rows_pad - rows, d), F32)], axis=0)
    mods = _ada(c_rows, w_ada, b_ada)

    tabs = _rope_tables(n)
    tabs_ctx = tuple(t[:tile_ctx] for t in tabs)
    gate_pad = jnp.zeros((depth, d, GATE_PAD - N_GATE), w_in.dtype)
    w_in_p = jnp.concatenate([w_in, gate_pad], axis=-1).astype(MXU)
    w_out_b = w_out.astype(MXU)
    aws = a_ws.reshape(depth, A_GROUPS * A_CHUNK, A_CHUNK).astype(MXU)
    abias = jnp.repeat(jnp.swapaxes(a_bs, 1, 2), A_WIDTH // A_GROUPS, axis=2)
    row = lambda a: a.reshape(1, -1)

    xc = ctx
    for i in range(depth):
        lam_init = 0.8 - 0.6 * math.exp(-0.3 * i)
        ctx_out = i < depth - 1
        mod_l = mods[i, :bsz][:, None, :]
        mod_c = mods[i, bsz:bsz + 1][:, None, :]
        common = (row(norm_w[i]), w_in_p[i])
        a_par = (row(a_ln_w[i]), aws[i], abias[i], c_conv_w[i])
        ya, qb, kb, vb, zb, craw, zc, gt = _proj(x, mod_l, *common, tabs, *a_par, rope=True, tile=ptile)
        ya_c, qb_c, kb_c, vb_c, zb_c, craw_c, zc_c, gt_c = _proj(xc, mod_c, *common, tabs_ctx, *a_par,
                                                                 rope=False, tile=tile_ctx)
        lam_p, sw = b_lam[i], row(b_subln_w[i])
        yb = _attn(qb, zb, lam_p, sw, [kb_c, kb], [vb_c, vb], lam_init=lam_init, tile=tile)
        gate_row = lambda p: jnp.concatenate([jnp.zeros((2 * C_HEADS,), F32), p.reshape(-1),
                                              jnp.zeros((GATE_PAD - N_GATE,), F32)])[None, :]
        gdn_par = (gate_row(c_a_log[i]), gate_row(c_dt_bias[i]), jnp.tile(c_norm_w[i], C_HEADS)[None, :])
        yc_all = _gdn(craw_c, gt_c, zc_c, craw, gt, zc, *gdn_par, ctx_out=ctx_out)
        final = i == depth - 1
        x = _out(x, mod_l, ya, yb, yc_all[0], w_out_b[i], row(final_norm_w), final=final, tile=otile)
        if ctx_out:
            yb_c = _attn(qb_c, zb_c, lam_p, sw, [kb_c], [vb_c], lam_init=lam_init, tile=tile_ctx)
            xc = _out(xc, mod_c, ya_c, yb_c, yc_all[1], w_out_b[i], row(final_norm_w), final=False, tile=tile_ctx)
    return x
```

```python
import functools
import math

import numpy as np
import jax
import jax.numpy as jnp
from jax import lax
from jax.experimental import pallas as pl
from jax.experimental.pallas import tpu as pltpu

F32 = jnp.float32
MXU = jnp.bfloat16

D_MODEL = 1024
NORM_EPS = 1e-6
GRID_W = 64
A_WIDTH, A_GROUPS, A_CHUNK, A_LN_EPS = 256, 4, 128, 1e-5
B_HEADS, B_HEAD_DIM, B_WIDTH = 4, 64, 512
ROPE_THETA = 10000.0
C_HEADS, C_HEAD_DIM, C_WIDTH, C_CONV, C_CHUNK = 4, 64, 256, 5, 64
N_GATE = 4 * C_HEADS
GATE_PAD = 128
OFF_A, OFF_B, OFF_C, OFF_G = 0, 3 * A_WIDTH, 3 * A_WIDTH + 4 * B_WIDTH, 3 * A_WIDTH + 4 * B_WIDTH + 4 * C_WIDTH
D_IN_PAD = OFF_G + GATE_PAD
NEG_BIG = -1e30
LOG2E = math.log2(math.e)
ATT_KCHUNK = 256
ATT_SCHUNK = 256
ATT_QTILE = 256
ATT_TILES = 4
PROJ_TILE = 512
OUT_TILE = 2048
CONV_ROWS = 256
VMEM_LIMIT = 56 * 1024 * 1024
GDN_BLOCK = 256
GDN_LOCKSTEP = 8
SUBLANES = 8
HALO = SUBLANES
ADA_TILE = 512
INV_BLOCK = 8


def _mm(a, b):
    return jnp.dot(a.astype(MXU), b.astype(MXU), preferred_element_type=F32)


def _mm_nt(a, b):
    return lax.dot_general(a.astype(MXU), b.astype(MXU), (((1,), (1,)), ((), ())), preferred_element_type=F32)


def _mm_tn(a, b):
    return lax.dot_general(a.astype(MXU), b.astype(MXU), (((0,), (0,)), ((), ())), preferred_element_type=F32)


def _split(a):
    hi = a.astype(MXU)
    return hi, (a - hi.astype(F32)).astype(MXU)


def _mm3(a, b):
    ah, al = _split(a)
    bh, bl = _split(b)
    d = lambda x, y: jnp.dot(x, y, preferred_element_type=F32)
    return d(ah, bh) + (d(ah, bl) + d(al, bh))


def _sigmoid(x):
    return 1.0 / (1.0 + jnp.exp(-x))


def _silu(x):
    return x * _sigmoid(x)


def _iota(shape, dim):
    return lax.broadcasted_iota(jnp.int32, shape, dim)


def _ada_kernel(c_ref, w_ref, b_ref, o_ref):
    o_ref[0] = _mm3(_silu(c_ref[...]), w_ref[0]) + b_ref[0]


def _ada(c_rows, w_ada, b_ada):
    depth, d, d3 = w_ada.shape
    r = c_rows.shape[0]
    tn = ADA_TILE
    return pl.pallas_call(
        _ada_kernel,
        grid=(depth, d3 // tn),
        in_specs=[pl.BlockSpec((r, d), lambda l, j: (0, 0)),
                  pl.BlockSpec((1, d, tn), lambda l, j: (l, 0, j)),
                  pl.BlockSpec((1, 1, tn), lambda l, j: (l, 0, j))],
        out_specs=pl.BlockSpec((1, r, tn), lambda l, j: (l, 0, j)),
        out_shape=jax.ShapeDtypeStruct((depth, r, d3), F32),
        name="ada_modulation",
    )(c_rows, w_ada, b_ada.reshape(depth, 1, d3))


def _proj_kernel(x_ref, xp_ref, xn_ref, mod_ref, nw_ref, w_ref, cos_ref, sinn_ref, sinp_ref, lnw_ref, aws_ref,
                 abias_ref, convw_ref, ya_ref, q_ref, k_ref, v_ref, zb_ref, cqkv_ref, zc_ref, g_ref, *, rope):
    d = D_MODEL
    mod = mod_ref[0]

    def normed(x):
        y = x * lax.rsqrt(jnp.mean(x * x, axis=-1, keepdims=True) + NORM_EPS) * nw_ref[...]
        return (y * (1.0 + mod[:, d:2 * d]) + mod[:, :d]).astype(MXU)

    t = x_ref.shape[1]
    half_t = t // 2
    proj = lambda lhs, lo, hi: jnp.dot(lhs, w_ref[:, lo:hi], preferred_element_type=F32)
    vals = {}
    h0 = normed(x_ref[0, :half_t])
    pc0 = proj(jnp.concatenate([h0, normed(xp_ref[0])], axis=0), OFF_C, OFF_G)
    h1 = normed(x_ref[0, half_t:])
    pc1 = proj(jnp.concatenate([h1, normed(xn_ref[0])], axis=0), OFF_C, OFF_G)
    h = jnp.concatenate([h0, h1], axis=0)

    def finish_a():
        pa = vals.pop("a")
        u, v, z = pa[:, :A_WIDTH], pa[:, A_WIDTH:2 * A_WIDTH], pa[:, 2 * A_WIDTH:]
        vc = v - jnp.mean(v, axis=-1, keepdims=True)
        vn = (vc * lax.rsqrt(jnp.mean(vc * vc, axis=-1, keepdims=True) + A_LN_EPS) * lnw_ref[...]).astype(MXU)
        grp = _iota((A_CHUNK, A_WIDTH), 1) // (A_WIDTH // A_GROUPS)
        for c in range(t // A_CHUNK):
            rows = slice(c * A_CHUNK, (c + 1) * A_CHUNK)
            r = jnp.dot(aws_ref[...], vn[rows], preferred_element_type=F32)
            s = abias_ref[...]
            for g in range(A_GROUPS):
                s = s + jnp.where(grp == g, r[g * A_CHUNK:(g + 1) * A_CHUNK], 0.0)
            ya_ref[0, rows, :] = (u[rows] * s * _silu(z[rows])).astype(ya_ref.dtype)

    def rot(a):
        if not rope:
            return a
        half = B_HEAD_DIM // 4
        return (a * cos_ref[...] + pltpu.roll(a, B_WIDTH - half, 1) * sinn_ref[...]
                + pltpu.roll(a, half, 1) * sinp_ref[...])

    def finish_q():
        q_ref[0] = (rot(vals.pop("q")) * (B_HEAD_DIM ** -0.5 * LOG2E)).astype(q_ref.dtype)

    def finish_k():
        k_ref[0] = rot(vals.pop("k")).astype(k_ref.dtype)

    def finish_v():
        v_ref[0] = vals.pop("v").T.astype(v_ref.dtype)

    def finish_z():
        zb_ref[0] = vals.pop("z").astype(zb_ref.dtype)

    sub = min(t, CONV_ROWS)

    def start_c():
        i, last = pl.program_id(0), pl.num_programs(0) - 1
        zc_ref[0, :half_t, :] = pc0[:half_t, 3 * C_WIDTH:]
        zc_ref[0, half_t:, :] = pc1[:half_t, 3 * C_WIDTH:]
        prev = jnp.where(i == 0, 0.0, pc0[half_t:, :3 * C_WIDTH])
        nxt = jnp.where(i == last, 0.0, pc1[half_t:, :3 * C_WIDTH])
        vals["c"] = jnp.concatenate([prev, pc0[:half_t, :3 * C_WIDTH], pc1[:half_t, :3 * C_WIDTH], nxt], axis=0)

    def finish_c(r0):
        w = convw_ref[...]
        pad = C_CONV // 2
        win = vals["c"][r0:r0 + sub + 2 * HALO]
        acc = win[HALO:HALO + sub] * w[pad:pad + 1]
        for j in range(C_CONV):
            if j != pad:
                acc = acc + pltpu.roll(win, (pad - j) % (sub + 2 * HALO), 0)[HALO:HALO + sub] * w[j:j + 1]
        act = _silu(acc)
        qh, kh = act[:, :C_WIDTH], act[:, C_WIDTH:2 * C_WIDTH]
        l2 = lambda a: a * lax.rsqrt(_mm_sel(a * a, _same_head()) + 1e-6)
        cqkv_ref[0, r0:r0 + sub, 0:C_WIDTH] = l2(qh) * (C_HEAD_DIM ** -0.5)
        cqkv_ref[0, r0:r0 + sub, C_WIDTH:2 * C_WIDTH] = l2(kh)
        cqkv_ref[0, r0:r0 + sub, 2 * C_WIDTH:] = act[:, 2 * C_WIDTH:]

    def start(name, lo, hi):
        def go():
            vals[name] = proj(h, lo, hi)
        return go

    def finish_g():
        g_ref[0] = vals.pop("g")

    bq = OFF_B
    order = [start_c, start("a", OFF_A, OFF_B)]
    convs = [functools.partial(finish_c, r0) for r0 in range(0, t, sub)]
    later = [start("q", bq, bq + B_WIDTH), start("k", bq + B_WIDTH, bq + 2 * B_WIDTH),
             start("v", bq + 2 * B_WIDTH, bq + 3 * B_WIDTH), start("z", bq + 3 * B_WIDTH, OFF_C),
             start("g", OFF_G, D_IN_PAD)]
    fins = convs + [finish_a, finish_q, finish_k, finish_v, finish_z, finish_g]
    while later or fins:
        if later:
            order.append(later.pop(0))
        if fins:
            order.append(fins.pop(0))
    for step in order:
        step()


def _proj(x, mod, nw, w_in, tabs, lnw, aws, abias, convw, *, rope, tile):
    bsz, n, d = x.shape
    cos, sinn, sinp = tabs
    mod_map = (lambda i, b: (b, 0, 0)) if mod.shape[0] == bsz else (lambda i, b: (0, 0, 0))
    tok = lambda w: pl.BlockSpec((1, tile, w), lambda i, b: (b, i, 0))
    per = tile // HALO
    before = pl.BlockSpec((1, HALO, d), lambda i, b: (b, jnp.maximum(i * per - 1, 0), 0))
    after = pl.BlockSpec((1, HALO, d), lambda i, b: (b, jnp.minimum((i + 1) * per, n // HALO - 1), 0))
    const = lambda a: pl.BlockSpec(a.shape, lambda i, b: (0,) * a.ndim)
    tab = pl.BlockSpec((tile, B_WIDTH), lambda i, b: (i, 0))
    outs = [(A_WIDTH, MXU), (B_WIDTH, MXU), (B_WIDTH, MXU), (B_WIDTH, MXU), (B_WIDTH, F32),
            (3 * C_WIDTH, F32), (C_WIDTH, F32), (GATE_PAD, F32)]
    out_specs = [tok(w) for w, _ in outs]
    out_shape = [jax.ShapeDtypeStruct((bsz, n, w), dt) for w, dt in outs]
    out_specs[3] = pl.BlockSpec((1, B_WIDTH, tile), lambda i, b: (b, 0, i))
    out_shape[3] = jax.ShapeDtypeStruct((bsz, B_WIDTH, n), MXU)
    return pl.pallas_call(
        functools.partial(_proj_kernel, rope=rope),
        grid=(n // tile, bsz),
        in_specs=[tok(d), before, after, pl.BlockSpec((1, 1, 3 * d), mod_map), const(nw), const(w_in), tab, tab, tab,
                  const(lnw), const(aws), const(abias), const(convw)],
        out_specs=out_specs,
        out_shape=out_shape,
        compiler_params=pltpu.CompilerParams(dimension_semantics=("arbitrary", "arbitrary"),
                                             vmem_limit_bytes=VMEM_LIMIT),
        name="norm_inproj_rope" if rope else "norm_inproj_ctx",
    )(x, x, x, mod, nw, w_in, cos, sinn, sinp, lnw, aws, abias, convw)


def _interleave(gens):
    gens = list(gens)
    while gens:
        for g in list(gens):
            if next(g, StopIteration) is StopIteration:
                gens.remove(g)


def _attn_kernel(*refs, seg_lens, lam_init):
    nseg = len(seg_lens)
    q_ref, z_ref, lam_ref, sw_ref = refs[:4]
    k_refs, v_refs = refs[4:4 + nseg], refs[4 + nseg:4 + 2 * nseg]
    o_ref, s_buf, e_buf = refs[4 + 2 * nseg:]
    lf = lam_ref[...]
    lam = (jnp.exp(jnp.sum(lf[0:1] * lf[1:2], axis=-1, keepdims=True))
           - jnp.exp(jnp.sum(lf[2:3] * lf[3:4], axis=-1, keepdims=True)) + lam_init)
    tq = ATT_QTILE
    ntile = q_ref.shape[1] // tq
    pair = 2 * B_HEAD_DIM
    first = _iota((1, pair), 1) < B_HEAD_DIM
    zero = jnp.zeros((), q_ref.dtype)
    chunks, row = [], 0
    for seg, n in enumerate(seg_lens):
        for c0 in range(0, n, ATT_KCHUNK):
            chunks.append((seg, c0, row))
            row += ATT_KCHUNK
    kc = ATT_KCHUNK
    maxes, heads = {}, {}

    dens = {}

    def scores(i):
        hh, j = divmod(i, 2)
        qt, hp = divmod(hh, B_HEADS)
        lanes = slice(hp * pair, (hp + 1) * pair)
        qm = jnp.where(first if j == 0 else jnp.logical_not(first), q_ref[0, qt * tq:(qt + 1) * tq, lanes], zero)
        m, row = None, 0
        for seg, n in enumerate(seg_lens):
            for c0 in range(0, n, ATT_SCHUNK):
                size = min(ATT_SCHUNK, n - c0)
                s = _mm_nt(k_refs[seg][0, c0:c0 + size, lanes], qm)
                for r in range(0, size, kc):
                    s_buf[hh % 2, j, row:row + kc, :] = s[r:r + kc]
                    cm = jnp.max(s[r:r + kc].reshape(kc // SUBLANES, SUBLANES, tq), axis=0)
                    m = cm if m is None else jnp.maximum(m, cm)
                    row += kc
                    yield
        maxes[i] = jnp.max(m, axis=0, keepdims=True)

    def exps(i):
        hh, j = divmod(i, 2)
        den = None
        for _, _, r0 in chunks:
            e = jnp.exp2(s_buf[hh % 2, j, r0:r0 + kc, :] - maxes[i])
            e_buf[hh % 2, j, r0:r0 + kc, :] = e.astype(MXU)
            cs = jnp.sum(e.reshape(kc // SUBLANES, SUBLANES, tq), axis=0)
            den = cs if den is None else den + cs
            yield
        dens[i] = jnp.sum(den, axis=0, keepdims=True)

    def values(hh):
        qt, hp = divmod(hh, B_HEADS)
        lanes = slice(hp * pair, (hp + 1) * pair)
        d0, d1 = dens[2 * hh], dens[2 * hh + 1]
        coef = (lam * d0 * (1.0 / d1)).astype(MXU)
        pv = None
        for seg, c0, r0 in chunks:
            p = e_buf[hh % 2, 0, r0:r0 + kc, :] - coef * e_buf[hh % 2, 1, r0:r0 + kc, :]
            t = _mm(v_refs[seg][0, lanes, c0:c0 + kc], p)
            pv = t if pv is None else pv + t
            yield
        acc = (pv * (1.0 / d0)).T
        o = acc * lax.rsqrt(jnp.mean(acc * acc, axis=-1, keepdims=True) + NORM_EPS) * sw_ref[...]
        heads[hh] = o * (1.0 - lam_init)
        if hp == B_HEADS - 1:
            rows = slice(qt * tq, (qt + 1) * tq)
            o_all = jnp.concatenate([heads.pop(qt * B_HEADS + p) for p in range(B_HEADS)], axis=-1)
            o_ref[0, rows, :] = (o_all * _silu(z_ref[0, rows, :])).astype(o_ref.dtype)

    nh = 2 * B_HEADS * ntile
    for t in range(nh + 2):
        stage = []
        if t < nh:
            stage.append(scores(t))
        if 1 <= t <= nh:
            stage.append(exps(t - 1))
        if t >= 3 and t % 2 == 1:
            stage.append(values((t - 3) // 2))
        _interleave(stage)


def _attn(q, z, lam_p, subln_w, ks, vs, *, lam_init, tile):
    bsz, n, w = q.shape
    seg_lens = tuple(a.shape[1] for a in ks)
    tok = pl.BlockSpec((1, tile, w), lambda b, i: (b, i, 0))
    const = lambda a: pl.BlockSpec(a.shape, lambda b, i: (0,) * a.ndim)
    full = lambda a: pl.BlockSpec((1,) + a.shape[1:], lambda b, i: (b, 0, 0))
    return pl.pallas_call(
        functools.partial(_attn_kernel, seg_lens=seg_lens, lam_init=lam_init),
        grid=(bsz, n // tile),
        in_specs=[tok, tok, const(lam_p), const(subln_w)] + [full(a) for a in ks] + [full(a) for a in vs],
        out_specs=tok,
        out_shape=jax.ShapeDtypeStruct((bsz, n, w), MXU),
        scratch_shapes=[pltpu.VMEM((2, 2, sum(seg_lens), ATT_QTILE), F32),
                        pltpu.VMEM((2, 2, sum(seg_lens), ATT_QTILE), MXU)],
        compiler_params=pltpu.CompilerParams(dimension_semantics=("arbitrary", "arbitrary"),
                                             vmem_limit_bytes=VMEM_LIMIT),
        name="diff_attention_%dseg" % len(ks),
    )(q, z, lam_p, subln_w, *ks, *vs)


def _head_of_lane(shape, dim):
    return _iota(shape, dim) // C_HEAD_DIM


def _expand_heads(src, first):
    r, w = src.shape
    lane = _iota((r, w), 1)
    halves = [jnp.take_along_axis(src, first + (lane + off) // C_HEAD_DIM, axis=1) for off in range(0, C_WIDTH, w)]
    return jnp.concatenate(halves, axis=1)


def _split3(a):
    a1 = a.astype(MXU)
    r1 = a - a1.astype(F32)
    a2 = r1.astype(MXU)
    return a1, a2, (r1 - a2.astype(F32)).astype(MXU)


def _mm_sel(a, sel):
    hi, lo = _split(a)
    return jnp.dot(hi, sel, preferred_element_type=F32) + jnp.dot(lo, sel, preferred_element_type=F32)


def _same_head():
    return (_head_of_lane((C_WIDTH, C_WIDTH), 0) == _head_of_lane((C_WIDTH, C_WIDTH), 1)).astype(MXU)


def _gdn_prep(qkv_ref, gate_ref, blocks, off, alog_ref, dtb_ref, q_s, k_s, v_s, bf_s, bb_s, gf_s, gb_s):
    rows = GDN_BLOCK
    ri, rj = _iota((rows, rows), 0), _iota((rows, rows), 1)
    same_chunk = (ri // C_CHUNK) == (rj // C_CHUNK)
    csum = jnp.concatenate([(same_chunk & (rj <= ri)).astype(MXU), (same_chunk & (rj >= ri)).astype(MXU)], axis=0)

    for i in blocks:
        src = slice(i * rows, (i + 1) * rows)
        dst = slice(off + i * rows, off + (i + 1) * rows)
        q_s[dst, :] = qkv_ref[0, src, 0:C_WIDTH]
        k_s[dst, :] = qkv_ref[0, src, C_WIDTH:2 * C_WIDTH]
        v_s[dst, :] = qkv_ref[0, src, 2 * C_WIDTH:]
        gt = gate_ref[0, src, :]
        h = C_HEADS
        beta = _sigmoid(gt)
        bf_s[dst, :] = _expand_heads(beta, 0)
        bb_s[dst, :] = _expand_heads(beta, h)
        sp = jnp.maximum(gt + dtb_ref[...], 0.0) + jnp.log1p(jnp.exp(-jnp.abs(gt + dtb_ref[...])))
        g = -jnp.exp(alog_ref[...]) * sp
        w = GATE_PAD
        r = jnp.dot(csum, jnp.concatenate(_split3(g), axis=1), preferred_element_type=F32)
        gc = r[:, :w] + (r[:, w:2 * w] + r[:, 2 * w:])
        gf_s[dst, :] = _expand_heads(gc[:rows], 2 * h)
        gb_s[dst, :] = _expand_heads(gc[rows:], 3 * h)
        yield


def _block_diag(a, bmask):
    return jnp.where(bmask, jnp.concatenate([a] * C_HEADS, axis=0), 0.0)


def _gdn_local(chains, q_s, k_s, v_s, bufs, buf):
    c, wd = C_CHUNK, C_WIDTH
    u_buf, w_buf, a_buf, qd_buf, kd_buf, eg_buf = bufs
    ii = _iota((c, wd), 0)
    jj = _iota((c, wd), 1) % c
    eye = jj == ii
    bmask = _head_of_lane((C_HEADS * c, wd), 1) == (_iota((C_HEADS * c, wd), 0) // c)
    bd = lambda a: _block_diag(a, bmask)
    ident = jnp.where(eye, 1.0, 0.0)

    st = []
    for rows, backward, beta_s, gc_s, _, _ in chains:
        q, k, v, beta, gc = q_s[rows, :], k_s[rows, :], v_s[rows, :], beta_s[rows, :], gc_s[rows, :]
        incl = (jj >= ii) if backward else (jj <= ii)
        strict = (jj > ii) if backward else (jj < ii)
        grow = jnp.sum(jnp.where(eye, gc, 0.0), axis=0, keepdims=True)
        decay = jnp.exp(jnp.where(incl, gc - grow, NEG_BIG))
        glast = gc[0:1] if backward else gc[c - 1:c]
        st.append(dict(q=q, k=k, v=v, beta=beta, gc=gc, decay=decay, strict=strict, glast=glast, kb=k * beta,
                       egc=jnp.exp(gc)))
    yield
    for s in st:
        qk = _mm_nt(jnp.concatenate([s["kb"], s["q"]], axis=0), bd(s["k"]))
        s["lmat"] = jnp.where(s["strict"], qk[:c] * s["decay"], 0.0)
        s["attn"] = qk[c:] * s["decay"]
    yield
    in_blk = (ii // INV_BLOCK) == (jj // INV_BLOCK)
    for s in st:
        dmat = jnp.where(in_blk, s["lmat"], 0.0)
        s["t"] = ident - dmat
        s["pw"] = _mm(dmat, bd(dmat))
    yield
    nsq = int(math.log2(INV_BLOCK)) - 1
    for step in range(nsq):
        for s in st:
            b = bd(s["pw"])
            if step < nsq - 1:
                r = _mm(jnp.concatenate([s["t"], s["pw"]], axis=0), b)
                s["t"], s["pw"] = s["t"] + r[:c], r[c:]
            else:
                s["t"] = s["t"] + _mm(s["t"], b)
        yield
    size = INV_BLOCK
    while size < c:
        emask = ((ii // (2 * size)) == (jj // (2 * size))) & ((ii // size) != (jj // size))
        for s in st:
            s["y"] = _mm(jnp.where(emask, s["lmat"], 0.0), bd(s["t"]))
        yield
        for s in st:
            s["t"] = s["t"] - _mm(s["t"], bd(s["y"]))
        yield
        size *= 2
    for s, (_, _, _, _, d, slot) in zip(st, chains):
        dst = slice(slot * c, (slot + 1) * c)
        u_buf[buf, d, dst, :] = _mm(s["t"], bd(s["v"] * s["beta"]))
        a_buf[buf, d, dst, :] = s["attn"].astype(MXU)
        qd_buf[buf, d, dst, :] = (s["q"] * s["egc"]).astype(MXU)
    yield
    for s, (_, _, _, _, d, slot) in zip(st, chains):
        dst = slice(slot * c, (slot + 1) * c)
        w_buf[buf, d, dst, :] = _mm(s["t"], bd(s["kb"] * s["egc"])).astype(MXU)
        kd_buf[buf, d, dst, :] = (s["k"] * jnp.exp(s["glast"] - s["gc"])).astype(MXU)
        eg_buf[buf, d, slot * HALO:(slot + 1) * HALO, :] = jnp.broadcast_to(jnp.exp(s["glast"]), (HALO, wd))


def _gdn_scan(steps, rows, st_s, o_refs, bufs, buf):
    c, wd = C_CHUNK, C_WIDTH
    u_buf, w_buf, a_buf, qd_buf, kd_buf, eg_buf = bufs
    bmask = _head_of_lane((C_HEADS * c, wd), 1) == (_iota((C_HEADS * c, wd), 0) // c)
    smask = _head_of_lane((wd, wd), 0) == _head_of_lane((wd, wd), 1)
    states = [st_s[0], st_s[1]]
    for slots in steps:
        src = [slice(slot * c, (slot + 1) * c) for slot in slots]
        ws = [_mm(jnp.concatenate([w_buf[buf, d, r, :], qd_buf[buf, d, r, :]], axis=0), states[d])
              for d, r in enumerate(src)]
        yield
        v_new = [u_buf[buf, d, r, :] - ws[d][:c] for d, r in enumerate(src)]
        upd = [_mm_tn(kd_buf[buf, d, r, :], v_new[d]) for d, r in enumerate(src)]
        for d, (r, slot) in enumerate(zip(src, slots)):
            o_refs[d][rows[d][slot], :] = ws[d][c:] + _mm(a_buf[buf, d, r, :], _block_diag(v_new[d], bmask))
        yield
        states = [states[d] * eg_buf[buf, d, slot * HALO:slot * HALO + 1, :] + jnp.where(smask, upd[d], 0.0)
                  for d, slot in enumerate(slots)]
    st_s[0], st_s[1] = states


def _gdn_kernel(*refs, n_ctx, n_lat, ctx_out):
    (rawc_ref, gtc_ref, zc_ref, rawl_ref, gtl_ref, zl_ref, alog_ref, dtb_ref, nw_ref) = refs[:9]
    nout = 2 if ctx_out else 1
    out_refs = refs[9:9 + nout]
    (q_s, k_s, v_s, bf_s, bb_s, gf_s, gb_s, of_s, ob_s, st_s) = refs[9 + nout:19 + nout]
    scr = (q_s, k_s, v_s, bf_s, bb_s, gf_s, gb_s)
    bufs = refs[19 + nout:]
    ncb, nlb = n_ctx // GDN_BLOCK, n_lat // GDN_BLOCK
    per = GDN_BLOCK // C_CHUNK
    _interleave([_gdn_prep(rawc_ref, gtc_ref, range(ncb), 0, alog_ref, dtb_ref, *scr)])
    st_s[...] = jnp.zeros_like(st_s)

    nblk = ncb + nlb
    steps = [(t, per - 1 - t) for t in range(per)]

    def block_rows(i):
        if isinstance(i, int):
            bf = i * GDN_BLOCK
            bb = (ncb - 1 - i if i < ncb else 2 * ncb + nlb - 1 - i) * GDN_BLOCK
        else:
            bf = pl.multiple_of(i * GDN_BLOCK, GDN_BLOCK)
            bb = pl.multiple_of(jnp.where(i < ncb, ncb - 1 - i, 2 * ncb + nlb - 1 - i) * GDN_BLOCK, GDN_BLOCK)
        return ([pl.ds(bf + r * C_CHUNK, C_CHUNK) for r in range(per)],
                [pl.ds(bb + r * C_CHUNK, C_CHUNK) for r in range(per)])

    def local(i):
        rows_f, rows_b = block_rows(i)
        chains = [c for r in range(per) for c in ((rows_f[r], False, bf_s, gf_s, 0, r), (rows_b[r], True, bb_s, gb_s, 1, r))]
        for lo in range(0, len(chains), GDN_LOCKSTEP):
            yield from _gdn_local(chains[lo:lo + GDN_LOCKSTEP], q_s, k_s, v_s, bufs, i % 2)

    def scan(i):
        return _gdn_scan(steps, block_rows(i), st_s, (of_s, ob_s), bufs, i % 2)

    todo = list(range(nlb))

    def prep_for(steps_ahead):
        blocks = []
        for s in steps_ahead:
            for b in (s, 2 * ncb + nlb - 1 - s):
                if ncb <= b < nblk and b - ncb in todo:
                    todo.remove(b - ncb)
                    blocks.append(b - ncb)
        return _gdn_prep(rawl_ref, gtl_ref, blocks, n_ctx, alog_ref, dtb_ref, *scr)

    _interleave([local(0), prep_for((1, 2))])
    peeled = 0
    while todo:
        _interleave([local(peeled + 1), scan(peeled), prep_for((peeled + 3,))])
        peeled += 1

    def block(i, carry):
        _interleave([local(i + 1), scan(i)])
        return carry

    lax.fori_loop(peeled, nblk - 1, block, 0)

    same_head = _same_head()

    def finish(blocks):
        for b in blocks:
            src = slice(b * GDN_BLOCK, (b + 1) * GDN_BLOCK)
            if b >= ncb:
                z_ref, out_ref, dst = zl_ref, out_refs[0], slice((b - ncb) * GDN_BLOCK, (b - ncb + 1) * GDN_BLOCK)
            elif ctx_out:
                z_ref, out_ref, dst = zc_ref, out_refs[1], src
            else:
                continue
            o = of_s[src, :] + ob_s[src, :]
            ms = _mm_sel(o * o, same_head) * (1.0 / C_HEAD_DIM)
            y = o * lax.rsqrt(ms + NORM_EPS) * nw_ref[...]
            out_ref[0, dst, :] = (y * _silu(z_ref[0, dst, :])).astype(out_ref.dtype)
            yield

    late = sorted({nblk - 1, ncb})
    _interleave([scan(nblk - 1), finish([b for b in range(nblk) if b not in late])])
    _interleave([finish(late)])


def _gdn(rawc, gtc, zc, rawl, gtl, zl, alog, dtb, nw, *, ctx_out):
    bsz, n_ctx, _ = rawc.shape
    n_lat = rawl.shape[1]
    nt = n_ctx + n_lat
    seq = lambda a: pl.BlockSpec((1,) + a.shape[1:], lambda b: (b, 0, 0))
    const = lambda a: pl.BlockSpec(a.shape, lambda b: (0,) * a.ndim)
    out_shape = [jax.ShapeDtypeStruct((bsz, n_lat, C_WIDTH), MXU)]
    if ctx_out:
        out_shape.append(jax.ShapeDtypeStruct((bsz, n_ctx, C_WIDTH), MXU))
    big = lambda: pltpu.VMEM((nt, C_WIDTH), F32)
    loc = lambda dt: pltpu.VMEM((2, 2, GDN_BLOCK, C_WIDTH), dt)
    per = GDN_BLOCK // C_CHUNK
    res = pl.pallas_call(
        functools.partial(_gdn_kernel, n_ctx=n_ctx, n_lat=n_lat, ctx_out=ctx_out),
        grid=(bsz,),
        in_specs=[seq(rawc), seq(gtc), seq(zc), seq(rawl), seq(gtl), seq(zl),
                  const(alog), const(dtb), const(nw)],
        out_specs=[seq(s) for s in out_shape],
        out_shape=out_shape,
        scratch_shapes=[big() for _ in range(9)] + [pltpu.VMEM((2, C_WIDTH, C_WIDTH), F32)]
        + [loc(F32), loc(MXU), loc(MXU), loc(MXU), loc(MXU), pltpu.VMEM((2, 2, per * HALO, C_WIDTH), F32)],
        compiler_params=pltpu.CompilerParams(dimension_semantics=("arbitrary",), vmem_limit_bytes=VMEM_LIMIT),
        name="gated_deltanet_bidir",
    )(rawc, gtc, zc, rawl, gtl, zl, alog, dtb, nw)
    return res


def _out_kernel(x_ref, mod_ref, ya_ref, yb_ref, yc_ref, w_ref, fw_ref, o_ref, *, final):
    d = D_MODEL
    y = (jnp.dot(ya_ref[0], w_ref[0:A_WIDTH], preferred_element_type=F32)
         + jnp.dot(yb_ref[0], w_ref[A_WIDTH:A_WIDTH + B_WIDTH], preferred_element_type=F32)
         + jnp.dot(yc_ref[0], w_ref[A_WIDTH + B_WIDTH:], preferred_element_type=F32))
    x = x_ref[0] + mod_ref[0][:, 2 * d:] * y
    if final:
        x = x * lax.rsqrt(jnp.mean(x * x, axis=-1, keepdims=True) + NORM_EPS) * fw_ref[...]
    o_ref[0] = x


def _out(x, mod, ya, yb, yc, w_out, fw, *, final, tile):
    bsz, n, d = x.shape
    mod_map = (lambda b, i: (b, 0, 0)) if mod.shape[0] == bsz else (lambda b, i: (0, 0, 0))
    tok = lambda w: pl.BlockSpec((1, tile, w), lambda b, i: (b, i, 0))
    const = lambda a: pl.BlockSpec(a.shape, lambda b, i: (0,) * a.ndim)
    return pl.pallas_call(
        functools.partial(_out_kernel, final=final),
        grid=(bsz, n // tile),
        in_specs=[tok(d), pl.BlockSpec((1, 1, 3 * d), mod_map), tok(A_WIDTH), tok(B_WIDTH), tok(C_WIDTH),
                  const(w_out), const(fw)],
        out_specs=tok(d),
        out_shape=jax.ShapeDtypeStruct((bsz, n, d), F32),
        compiler_params=pltpu.CompilerParams(dimension_semantics=("arbitrary", "arbitrary"),
                                             vmem_limit_bytes=VMEM_LIMIT),
        name="outproj_residual_final" if final else "outproj_residual",
    )(x, mod, ya, yb, yc, w_out, fw)


def _rope_tables(n):
    pos = jnp.arange(n)
    half = B_HEAD_DIM // 2
    inv_freq = ROPE_THETA ** (-jnp.arange(0, half, 2, dtype=F32) / half)
    lane = np.arange(B_WIDTH)
    dd = lane % B_HEAD_DIM
    use_col = (dd // half) == 1
    second = ((dd % half) // (half // 2)) == 1
    fidx = dd % (half // 2)
    p = jnp.where(use_col[None, :], (pos % GRID_W)[:, None], (pos // GRID_W)[:, None]).astype(F32)
    ang = p * inv_freq[fidx][None, :]
    cos, sin = jnp.cos(ang), jnp.sin(ang)
    return cos, jnp.where(second[None, :], 0.0, -sin), jnp.where(second[None, :], sin, 0.0)


def kernel(x, c, ctx, c_ctx, w_ada, b_ada, norm_w, w_in, w_out, a_ln_w, a_ws, a_bs, b_lam, b_subln_w,
           c_conv_w, c_a_log, c_dt_bias, c_norm_w, final_norm_w):
    bsz, n, d = x.shape
    n_ctx = ctx.shape[1]
    depth = w_ada.shape[0]
    assert d == D_MODEL and n % GDN_BLOCK == 0 and n_ctx % GDN_BLOCK == 0 and n % GRID_W == 0
    tile = ATT_QTILE * ATT_TILES
    tile_ctx = min(ATT_QTILE, n_ctx)
    ptile = min(PROJ_TILE, n)
    otile = next(t for t in (OUT_TILE, PROJ_TILE, GDN_BLOCK) if n % t == 0)
    assert n % ptile == 0 and n % tile == 0

    rows = bsz + 1
    rows_pad = -(-rows // 8) * 8
    c_rows = jnp.concatenate([c, c_ctx[None, :], jnp.zeros((rows_pad - rows, d), F32)], axis=0)
    mods = _ada(c_rows, w_ada, b_ada)

    tabs = _rope_tables(n)
    tabs_ctx = tuple(t[:tile_ctx] for t in tabs)
    gate_pad = jnp.zeros((depth, d, GATE_PAD - N_GATE), w_in.dtype)
    w_in_p = jnp.concatenate([w_in, gate_pad], axis=-1).astype(MXU)
    w_out_b = w_out.astype(MXU)
    aws = a_ws.reshape(depth, A_GROUPS * A_CHUNK, A_CHUNK).astype(MXU)
    abias = jnp.repeat(jnp.swapaxes(a_bs, 1, 2), A_WIDTH // A_GROUPS, axis=2)
    row = lambda a: a.reshape(1, -1)

    xc = ctx
    for i in range(depth):
        lam_init = 0.8 - 0.6 * math.exp(-0.3 * i)
        ctx_out = i < depth - 1
        mod_l = mods[i, :bsz][:, None, :]
        mod_c = mods[i, bsz:bsz + 1][:, None, :]
        common = (row(norm_w[i]), w_in_p[i])
        a_par = (row(a_ln_w[i]), aws[i], abias[i], c_conv_w[i])
        ya, qb, kb, vb, zb, craw, zc, gt = _proj(x, mod_l, *common, tabs, *a_par, rope=True, tile=ptile)
        ya_c, qb_c, kb_c, vb_c, zb_c, craw_c, zc_c, gt_c = _proj(xc, mod_c, *common, tabs_ctx, *a_par,
                                                                 rope=False, tile=tile_ctx)
        lam_p, sw = b_lam[i], row(b_subln_w[i])
        yb = _attn(qb, zb, lam_p, sw, [kb_c, kb], [vb_c, vb], lam_init=lam_init, tile=tile)
        gate_row = lambda p: jnp.concatenate([jnp.zeros((2 * C_HEADS,), F32), p.reshape(-1),
                                              jnp.zeros((GATE_PAD - N_GATE,), F32)])[None, :]
        gdn_par = (gate_row(c_a_log[i]), gate_row(c_dt_bias[i]), jnp.tile(c_norm_w[i], C_HEADS)[None, :])
        yc_all = _gdn(craw_c, gt_c, zc_c, craw, gt, zc, *gdn_par, ctx_out=ctx_out)
        final = i == depth - 1
        x = _out(x, mod_l, ya, yb, yc_all[0], w_out_b[i], row(final_norm_w), final=final, tile=otile)
        if ctx_out:
            yb_c = _attn(qb_c, zb_c, lam_p, sw, [kb_c], [vb_c], lam_init=lam_init, tile=tile_ctx)
            xc = _out(xc, mod_c, ya_c, yb_c, yc_all[1], w_out_b[i], row(final_norm_w), final=False, tile=tile_ctx)
    return x
```
